```python
import jax
import jax.numpy as jnp
from jax import lax
import numpy as np

D_MODEL = 2048
BATCH = 4
SEQ = 8192
DEPTH = 1

EPS = 1e-6
NEG = -1e30
Q_BLOCK = 128

MLA_HEADS = 8
MLA_Q_LORA = 512
MLA_KV_LORA = 256
MLA_NOPE = 128
MLA_ROPE = 64
MLA_V = 128
ROPE_THETA = 10000.0

NSA_HEADS = 16
NSA_GROUPS = 2
NSA_HPG = NSA_HEADS // NSA_GROUPS
NSA_DK = 64
CMP_LEN = 32
CMP_STRIDE = 16
CMP_HIDDEN = 128
SLC_LEN = 64
SLC_TOPK = 16
WINDOW = 512
FORCE_SCORE = 1e4

MIX_WIDTH = MLA_HEADS * MLA_V + NSA_HEADS * NSA_DK
IN_SIZES = (MLA_Q_LORA, MLA_KV_LORA, MLA_ROPE, NSA_HEADS * NSA_DK) + (NSA_GROUPS * NSA_DK,) * 6 + (3 * NSA_HEADS,)
IN_COLS = sum(IN_SIZES)
D_FF = -(-8 * D_MODEL // (3 * 256)) * 256

kernel_name = 'hybrid_mla_nsa_parallel_heads'


def rmsnorm(x, g):
    xf = x.astype(jnp.float32)
    y = xf * lax.rsqrt(jnp.mean(xf * xf, axis=-1, keepdims=True) + EPS)
    return (y * g.astype(jnp.float32)).astype(x.dtype)


def apply_rope(x, pos):
    d = x.shape[-1]
    inv = ROPE_THETA ** (-jnp.arange(0, d, 2, dtype=jnp.float32) / d)
    ang = pos[:, None] * inv[None, :]
    cos = jnp.cos(ang)[None, :, None, :].astype(x.dtype)
    sin = jnp.sin(ang)[None, :, None, :].astype(x.dtype)
    x1, x2 = x[..., : d // 2], x[..., d // 2:]
    return jnp.concatenate([x1 * cos - x2 * sin, x1 * sin + x2 * cos], axis=-1)


def mla_group(c_q_raw, c_kv_raw, k_rope_raw, g_q, g_kv, w_uq, w_uk, w_uv):
    B, S, _ = c_q_raw.shape
    pos = jnp.arange(S, dtype=jnp.float32)
    c_q = rmsnorm(c_q_raw, g_q)
    c_kv = rmsnorm(c_kv_raw, g_kv)
    q = (c_q @ w_uq).reshape(B, S, MLA_HEADS, MLA_NOPE + MLA_ROPE)
    q_nope = q[..., :MLA_NOPE]
    q_rope = apply_rope(q[..., MLA_NOPE:], pos)
    k_rope = apply_rope(k_rope_raw[:, :, None, :], pos)[:, :, 0]
    k_nope = (c_kv @ w_uk).reshape(B, S, MLA_HEADS, MLA_NOPE)
    v = (c_kv @ w_uv).reshape(B, S, MLA_HEADS, MLA_V)
    scale = (MLA_NOPE + MLA_ROPE) ** -0.5
    key_pos = jnp.arange(S)

    def block(i):
        start = i * Q_BLOCK
        qn = lax.dynamic_slice_in_dim(q_nope, start, Q_BLOCK, axis=1)
        qr = lax.dynamic_slice_in_dim(q_rope, start, Q_BLOCK, axis=1)
        s = jnp.einsum('bqhd,bkhd->bhqk', qn, k_nope) + jnp.einsum('bqhd,bkd->bhqk', qr, k_rope)
        s = s.astype(jnp.float32) * scale
        qpos = start + jnp.arange(Q_BLOCK)
        s = jnp.where(key_pos[None, :] <= qpos[:, None], s, NEG)
        p = jax.nn.softmax(s, axis=-1).astype(v.dtype)
        return jnp.einsum('bhqk,bkhd->bqhd', p, v)

    o = lax.map(block, jnp.arange(S // Q_BLOCK))
    return o.transpose(1, 0, 2, 3, 4).reshape(B, S, MLA_HEADS * MLA_V)


def compress_blocks(k, idx, pos_emb, w1, w2):
    B = k.shape[0]
    n_cmp = idx.shape[0]
    blocks = k[:, idx] + pos_emb[None, None, :, None, :]
    blocks = blocks.transpose(0, 1, 3, 2, 4).reshape(B, n_cmp, NSA_GROUPS, CMP_LEN * NSA_DK)
    return jax.nn.silu(blocks @ w1) @ w2


def nsa_group(q_raw, k_c, v_c, k_s, v_s, k_w, v_w, g_raw, pos_k, pos_v, w_ck1, w_ck2, w_cv1, w_cv2):
    B, S, _ = q_raw.shape
    G, HPG, DK = NSA_GROUPS, NSA_HPG, NSA_DK
    q = q_raw.reshape(B, S, G, HPG, DK)
    gates = jax.nn.sigmoid(g_raw).reshape(B, S, G, HPG, 3)
    k_c = k_c.reshape(B, S, G, DK)
    v_c = v_c.reshape(B, S, G, DK)
    k_s_t = k_s.reshape(B, S, G, DK).transpose(0, 2, 1, 3)
    v_s_t = v_s.reshape(B, S, G, DK).transpose(0, 2, 1, 3)
    pad = ((0, 0), (WINDOW, 0), (0, 0), (0, 0))
    k_w_pad = jnp.pad(k_w.reshape(B, S, G, DK), pad)
    v_w_pad = jnp.pad(v_w.reshape(B, S, G, DK), pad)
    scale = DK ** -0.5

    n_cmp = (S - CMP_LEN) // CMP_STRIDE + 1
    cmp_start = CMP_STRIDE * jnp.arange(n_cmp)
    idx = cmp_start[:, None] + jnp.arange(CMP_LEN)[None, :]
    kc = compress_blocks(k_c, idx, pos_k, w_ck1, w_ck2)
    vc = compress_blocks(v_c, idx, pos_v, w_cv1, w_cv2)
    cmp_end = cmp_start + CMP_LEN - 1

    n_slc = S // SLC_LEN
    topk = min(SLC_TOPK, n_slc)
    slc_start = SLC_LEN * jnp.arange(n_slc)
    overlap = jnp.clip(
        jnp.minimum(cmp_start[:, None] + CMP_LEN, slc_start[None, :] + SLC_LEN)
        - jnp.maximum(cmp_start[:, None], slc_start[None, :]), 0, None
    ).astype(jnp.float32) / CMP_STRIDE
    blk_ids = jnp.arange(n_slc)
    in_blk = jnp.arange(SLC_LEN)

    slopes = (2.0 ** (-8.0 * jnp.arange(1, NSA_HEADS + 1, dtype=jnp.float32) / NSA_HEADS)).reshape(G, HPG)
    slopes5 = slopes[None, :, :, None, None]
    gather = jax.vmap(jax.vmap(lambda src, ix: src[ix]))

    def block(i):
        start = i * Q_BLOCK
        qb = lax.dynamic_slice_in_dim(q, start, Q_BLOCK, axis=1)
        t = start + jnp.arange(Q_BLOCK)

        dist_c = t[:, None] - cmp_end[None, :]
        valid_c = dist_c >= 0
        s = jnp.einsum('bqghd,bcgd->bghqc', qb, kc).astype(jnp.float32) * scale
        s = jnp.where(valid_c, s - slopes5 * dist_c.astype(jnp.float32), NEG)
        p_c = jax.nn.softmax(s, axis=-1) * valid_c
        o_c = jnp.einsum('bghqc,bcgd->bqghd', p_c.astype(vc.dtype), vc)

        imp = jnp.einsum('bghqc,cn->bgqn', p_c, overlap)
        blk_t = t // SLC_LEN
        valid_s = blk_ids[None, :] <= blk_t[:, None]
        forced = (blk_ids[None, :] == 0) | (blk_ids[None, :] == blk_t[:, None]) | (blk_ids[None, :] == blk_t[:, None] - 1)
        imp = jnp.where(forced, FORCE_SCORE, jnp.where(valid_s, imp, -1.0))
        _, sel = lax.top_k(imp, topk)
        tok = (sel[..., None] * SLC_LEN + in_blk).reshape(B, G, Q_BLOCK * topk * SLC_LEN)
        nk = topk * SLC_LEN
        ks_g = gather(k_s_t, tok).reshape(B, G, Q_BLOCK, nk, DK)
        vs_g = gather(v_s_t, tok).reshape(B, G, Q_BLOCK, nk, DK)
        dist_s = (t[None, None, :, None] - tok.reshape(B, G, Q_BLOCK, nk))[:, :, None]
        s = jnp.einsum('bqghd,bgqkd->bghqk', qb, ks_g).astype(jnp.float32) * scale
        s = jnp.where(dist_s >= 0, s - slopes5 * dist_s.astype(jnp.float32), NEG)
        p_s = jax.nn.softmax(s, axis=-1).astype(vs_g.dtype)
        o_s = jnp.einsum('bghqk,bgqkd->bqghd', p_s, vs_g)

        kwb = lax.dynamic_slice_in_dim(k_w_pad, start, Q_BLOCK + WINDOW, axis=1)
        vwb = lax.dynamic_slice_in_dim(v_w_pad, start, Q_BLOCK + WINDOW, axis=1)
        s_pos = start - WINDOW + jnp.arange(Q_BLOCK + WINDOW)
        dist_w = t[:, None] - s_pos[None, :]
        valid_w = (dist_w >= 0) & (dist_w < WINDOW) & (s_pos[None, :] >= 0)
        s = jnp.einsum('bqghd,bkgd->bghqk', qb, kwb).astype(jnp.float32) * scale
        s = jnp.where(valid_w, s - slopes5 * dist_w.astype(jnp.float32), NEG)
        p_w = jax.nn.softmax(s, axis=-1).astype(vwb.dtype)
        o_w = jnp.einsum('bghqk,bkgd->bqghd', p_w, vwb)

        gb = lax.dynamic_slice_in_dim(gates, start, Q_BLOCK, axis=1)
        return gb[..., 0:1] * o_c + gb[..., 1:2] * o_s + gb[..., 2:3] * o_w

    o = lax.map(block, jnp.arange(S // Q_BLOCK))
    return o.transpose(1, 0, 2, 3, 4, 5).reshape(B, S, NSA_HEADS * DK)


def setup_inputs(seed: int = 0) -> dict:
    key = jax.random.key(seed)
    ks = jax.random.split(key, 24)
    f32 = jnp.float32

    def nrm(k, shape, fan_in):
        return jax.random.normal(k, shape, f32) * (fan_in ** -0.5)

    def gain(k, shape):
        return 1.0 + 0.01 * jax.random.normal(k, shape, f32)

    L = DEPTH
    return {
        'x': jax.random.normal(ks[0], (BATCH, SEQ, D_MODEL), f32),
        'attn_norm_g': gain(ks[1], (L, D_MODEL)),
        'w_in': nrm(ks[2], (L, D_MODEL, IN_COLS), D_MODEL),
        'mla_q_norm_g': gain(ks[3], (L, MLA_Q_LORA)),
        'mla_kv_norm_g': gain(ks[4], (L, MLA_KV_LORA)),
        'w_uq': nrm(ks[5], (L, MLA_Q_LORA, MLA_HEADS * (MLA_NOPE + MLA_ROPE)), MLA_Q_LORA),
        'w_uk': nrm(ks[6], (L, MLA_KV_LORA, MLA_HEADS * MLA_NOPE), MLA_KV_LORA),
        'w_uv': nrm(ks[7], (L, MLA_KV_LORA, MLA_HEADS * MLA_V), MLA_KV_LORA),
        'cmp_pos_k': 0.1 * jax.random.normal(ks[8], (L, CMP_LEN, NSA_DK), f32),
        'cmp_pos_v': 0.1 * jax.random.normal(ks[9], (L, CMP_LEN, NSA_DK), f32),
        'w_cmp_k1': nrm(ks[10], (L, CMP_LEN * NSA_DK, CMP_HIDDEN), CMP_LEN * NSA_DK),
        'w_cmp_k2': nrm(ks[11], (L, CMP_HIDDEN, NSA_DK), CMP_HIDDEN),
        'w_cmp_v1': nrm(ks[12], (L, CMP_LEN * NSA_DK, CMP_HIDDEN), CMP_LEN * NSA_DK),
        'w_cmp_v2': nrm(ks[13], (L, CMP_HIDDEN, NSA_DK), CMP_HIDDEN),
        'w_o': nrm(ks[14], (L, MIX_WIDTH, D_MODEL), MIX_WIDTH),
        'ffn_norm_g': gain(ks[15], (L, D_MODEL)),
        'w_gate': nrm(ks[16], (L, D_MODEL, D_FF), D_MODEL),
        'w_up': nrm(ks[17], (L, D_MODEL, D_FF), D_MODEL),
        'w_down': nrm(ks[18], (L, D_FF, D_MODEL), D_FF),
        'final_norm_g': gain(ks[19], (D_MODEL,)),
    }


def reference(x, attn_norm_g, w_in, mla_q_norm_g, mla_kv_norm_g, w_uq, w_uk, w_uv,
              cmp_pos_k, cmp_pos_v, w_cmp_k1, w_cmp_k2, w_cmp_v1, w_cmp_v2,
              w_o, ffn_norm_g, w_gate, w_up, w_down, final_norm_g):
    offsets = np.cumsum(IN_SIZES)[:-1].tolist()
    for l in range(DEPTH):
        h = rmsnorm(x, attn_norm_g[l])
        proj = h @ w_in[l]
        (c_q, c_kv, k_rope, nsa_q, k_c, v_c, k_s, v_s, k_w, v_w, g_raw) = jnp.split(proj, offsets, axis=-1)
        o_mla = mla_group(c_q, c_kv, k_rope, mla_q_norm_g[l], mla_kv_norm_g[l], w_uq[l], w_uk[l], w_uv[l])
        o_nsa = nsa_group(nsa_q, k_c, v_c, k_s, v_s, k_w, v_w, g_raw, cmp_pos_k[l], cmp_pos_v[l],
                          w_cmp_k1[l], w_cmp_k2[l], w_cmp_v1[l], w_cmp_v2[l])
        x = x + jnp.concatenate([o_mla, o_nsa], axis=-1) @ w_o[l]
        h = rmsnorm(x, ffn_norm_g[l])
        x = x + (jax.nn.silu(h @ w_gate[l]) * (h @ w_up[l])) @ w_down[l]
    return rmsnorm(x, final_norm_g)
```

```python
import functools

import numpy as np
import jax
import jax.numpy as jnp
from jax import lax
from jax.experimental import pallas as pl
from jax.experimental.pallas import tpu as pltpu

F32 = jnp.float32
BF16 = jnp.bfloat16

EPS = 1e-6
NEG = -1e30
LANES = 128

MLA_HEADS = 8
MLA_Q_LORA = 512
MLA_KV_LORA = 256
MLA_NOPE = 128
MLA_ROPE = 64
MLA_V = 128
ROPE_THETA = 10000.0

NSA_HEADS = 16
NSA_GROUPS = 2
NSA_HPG = NSA_HEADS // NSA_GROUPS
NSA_DK = 64
CMP_LEN = 32
CMP_STRIDE = 16
CMP_HIDDEN = 128
SLC_LEN = 64
SLC_TOPK = 16
WINDOW = 512
FORCE_SCORE = 1e4
NSA_TILE = 128
NS_PAD = 128

VMEM_LIMIT = 56 * 1024 * 1024


def _params(n_axes):
    return pltpu.CompilerParams(dimension_semantics=("arbitrary",) * n_axes, vmem_limit_bytes=VMEM_LIMIT)


def _resident(shape):
    zeros = (0,) * len(shape)
    return pl.BlockSpec(shape, lambda *_: zeros, pipeline_mode=pl.Buffered(1))


def _rmsnorm(x, g):
    return x * lax.rsqrt(jnp.mean(x * x, axis=-1, keepdims=True) + EPS) * g


def _dot(a, b):
    return jnp.dot(a, b, preferred_element_type=F32)


def _dot_nt(a, b):
    return lax.dot_general(a, b, (((1,), (1,)), ((), ())), preferred_element_type=F32)


_IN_COLS = dict(cq=(0, 512), ckv=(512, 768), kra=(768, 896), krb=(896, 1024), q=(1024, 3072),
                kvc=(3072, 3328), kvsw=(3328, 4352), gate=(4352, 4608))


def _inproj_kernel(x_ref, g_ref, w_ref, cos_ref, sin_ref, cq_ref, ckv_ref, kr_ref, q_ref, kvc_ref, kvsw_ref,
                   gate_ref):
    h = _rmsnorm(x_ref[...], g_ref[...]).astype(BF16)

    def mm(name):
        lo, hi = _IN_COLS[name]
        return _dot(h, w_ref[:, lo:hi])

    cq_ref[...] = mm("cq")
    ckv_ref[...] = mm("ckv")
    kr_ref[...] = (mm("kra") * cos_ref[...] + mm("krb") * sin_ref[...]).astype(BF16)
    q_ref[...] = mm("q").astype(BF16)
    kvc_ref[...] = mm("kvc")
    kvsw_ref[...] = mm("kvsw").astype(BF16)
    gate_ref[...] = mm("gate")


def _in_proj(x2, g, w, cos_t, sin_t, seq, tm=256):
    n, d = x2.shape
    nt = seq // tm
    widths = {k: hi - lo for k, (lo, hi) in _IN_COLS.items()}
    row = lambda c: pl.BlockSpec((tm, c), lambda i: (i, 0))
    tab = pl.BlockSpec((tm, LANES), lambda i: (i % nt, 0))
    outs = [("cq", F32), ("ckv", F32), ("kra", BF16), ("q", BF16), ("kvc", F32), ("kvsw", BF16), ("gate", F32)]
    return pl.pallas_call(
        _inproj_kernel,
        grid=(n // tm,),
        in_specs=[row(d), _resident((1, d)), _resident(w.shape), tab, tab],
        out_specs=[row(widths[k]) for k, _ in outs],
        out_shape=[jax.ShapeDtypeStruct((n, widths[k]), dt) for k, dt in outs],
        compiler_params=_params(1),
        name="in_proj",
    )(x2, g, w, cos_t, sin_t)


def _mlaup_kernel(cq_ref, ckv_ref, gq_ref, gkv_ref, wq_ref, wkv_ref, cos_ref, sin_ref, qn_ref, qr_ref, kn_ref,
                  v_ref, *, scale):
    cqn = _rmsnorm(cq_ref[...], gq_ref[...]).astype(BF16)
    ckvn = _rmsnorm(ckv_ref[...], gkv_ref[...]).astype(BF16)
    hn = MLA_HEADS * MLA_NOPE
    hr = MLA_HEADS * MLA_ROPE
    qn_ref[...] = (_dot(cqn, wq_ref[:, :hn]) * scale).astype(BF16)
    a = _dot(cqn, wq_ref[:, hn:hn + hr])
    b = _dot(cqn, wq_ref[:, hn + hr:])
    cos = cos_ref[...]
    sin = sin_ref[...]
    for p in range(hr // LANES):
        sl = slice(p * LANES, (p + 1) * LANES)
        qr_ref[:, sl] = ((a[:, sl] * cos + b[:, sl] * sin) * scale).astype(BF16)
    kn_ref[...] = _dot(ckvn, wkv_ref[:, :hn]).astype(BF16)
    v_ref[...] = _dot(ckvn, wkv_ref[:, hn:]).astype(BF16)


def _mla_up(cq, ckv, gq, gkv, wq, wkv, cos_t, sin_t, seq, tm=512):
    n = cq.shape[0]
    nt = seq // tm
    hn = MLA_HEADS * MLA_NOPE
    hr = MLA_HEADS * MLA_ROPE
    row = lambda c: pl.BlockSpec((tm, c), lambda i: (i, 0))
    tab = pl.BlockSpec((tm, LANES), lambda i: (i % nt, 0))
    scale = (MLA_NOPE + MLA_ROPE) ** -0.5
    return pl.pallas_call(
        functools.partial(_mlaup_kernel, scale=scale),
        grid=(n // tm,),
        in_specs=[row(MLA_Q_LORA), row(MLA_KV_LORA), _resident(gq.shape), _resident(gkv.shape),
                  _resident(wq.shape), _resident(wkv.shape), tab, tab],
        out_specs=[row(hn), row(hr), row(hn), row(MLA_HEADS * MLA_V)],
        out_shape=[jax.ShapeDtypeStruct((n, hn), BF16), jax.ShapeDtypeStruct((n, hr), BF16),
                   jax.ShapeDtypeStruct((n, hn), BF16), jax.ShapeDtypeStruct((n, MLA_HEADS * MLA_V), BF16)],
        compiler_params=_params(1),
        name="mla_up",
    )(cq, ckv, gq, gkv, wq, wkv, cos_t, sin_t)


def _mla_attn_kernel(qn_ref, qr_ref, kn_ref, kr_ref, v_ref, o_ref, m_ref, l_ref, acc_ref, *, tq):
    i = pl.program_id(2)
    lane = lax.broadcasted_iota(jnp.int32, (tq, LANES), 1)
    row = lax.broadcasted_iota(jnp.int32, (tq, tq), 0)
    col = lax.broadcasted_iota(jnp.int32, (tq, tq), 1)
    qr = qr_ref[...]
    zero = jnp.zeros_like(qr)
    for hh in range(2):
        hs = slice(hh * LANES, (hh + 1) * LANES)
        own = (lane < MLA_ROPE) if hh == 0 else (lane >= MLA_ROPE)
        q_cat = jnp.concatenate([qn_ref[:, hs], jnp.where(own, qr, zero)], axis=1)
        m_ref[...] = jnp.full_like(m_ref, NEG)
        l_ref[...] = jnp.zeros_like(l_ref)
        acc_ref[...] = jnp.zeros_like(acc_ref)

        def step(j, diagonal):
            ks = pl.ds(pl.multiple_of(j * tq, tq), tq)
            k_cat = jnp.concatenate([kn_ref[ks, hs], kr_ref[ks, :]], axis=1)
            s = _dot_nt(q_cat, k_cat)
            if diagonal:
                s = jnp.where(row >= col, s, NEG)
            m_prev = m_ref[...]
            m_new = jnp.maximum(m_prev, jnp.max(s, axis=-1, keepdims=True))
            alpha = jnp.exp(m_prev - m_new)
            p = jnp.exp(s - m_new)
            l_ref[...] = alpha * l_ref[...] + jnp.sum(p, axis=-1, keepdims=True)
            acc_ref[...] = alpha * acc_ref[...] + _dot(p.astype(BF16), v_ref[ks, hs])
            m_ref[...] = m_new

        def body(j, carry):
            step(j, False)
            return carry

        lax.fori_loop(0, i, body, 0)
        step(i, True)
        o_ref[:, hs] = (acc_ref[...] / l_ref[...]).astype(BF16)


def _mla_attn(qn, qr, kn, kr, v, batch, seq, tq=512):
    n = qn.shape[0]
    nq = seq // tq
    pairs = MLA_HEADS // 2
    pw = 2 * LANES
    qspec = lambda c: pl.BlockSpec((tq, c), lambda b, p, i: (b * nq + i, p))
    kspec = pl.BlockSpec((seq, pw), lambda b, p, i: (b, p))
    return pl.pallas_call(
        functools.partial(_mla_attn_kernel, tq=tq),
        grid=(batch, pairs, nq),
        in_specs=[qspec(pw), qspec(LANES), kspec, pl.BlockSpec((seq, LANES), lambda b, p, i: (b, 0)), kspec],
        out_specs=qspec(pw),
        out_shape=jax.ShapeDtypeStruct((n, MLA_HEADS * MLA_V), BF16),
        scratch_shapes=[pltpu.VMEM((tq, 1), F32), pltpu.VMEM((tq, 1), F32), pltpu.VMEM((tq, LANES), F32)],
        compiler_params=_params(3),
        name="mla_attn",
    )(qn, qr, kn, kr, v)


def _compress_kernel(r_ref, pos_ref, w1_ref, w2_ref, o_ref):
    half = CMP_STRIDE * NSA_DK
    r = r_ref[0, 0, 0]
    pos = pos_ref[0]
    a = _dot((r + pos[:, :half]).astype(BF16), w1_ref[0, :half, :])
    b = _dot((r + pos[:, half:]).astype(BF16), w1_ref[0, half:, :])
    hid = a + pltpu.roll(b, b.shape[0] - 1, 0)
    act = hid * jax.nn.sigmoid(hid)
    o_ref[0, 0, 0] = _dot(act.astype(BF16), w2_ref[0]).astype(BF16)


def _compress(r, pos, w1, w2):
    _, batch, groups, nc, width = r.shape
    return pl.pallas_call(
        _compress_kernel,
        grid=(2, batch, groups),
        in_specs=[pl.BlockSpec((1, 1, 1, nc, width), lambda t, b, g: (t, b, g, 0, 0)),
                  pl.BlockSpec((1, 1, 2 * width), lambda t, b, g: (t, 0, 0)),
                  pl.BlockSpec((1, 2 * width, CMP_HIDDEN), lambda t, b, g: (t, 0, 0)),
                  pl.BlockSpec((1, CMP_HIDDEN, LANES), lambda t, b, g: (t, 0, 0))],
        out_specs=pl.BlockSpec((1, 1, 1, nc, LANES), lambda t, b, g: (t, b, g, 0, 0)),
        out_shape=jax.ShapeDtypeStruct((2, batch, groups, nc, LANES), BF16),
        compiler_params=_params(3),
        name="compress",
    )(r, pos, w1, w2)


def _split3(x):
    hi = x.astype(BF16)
    r = x - hi.astype(F32)
    mid = r.astype(BF16)
    lo = (r - mid.astype(F32)).astype(BF16)
    return hi, mid, lo


def _nsa_cmp_kernel(slopes_ref, q_ref, kc_ref, vc_ref, ov_ref, gate_ref, oc_ref, sel_ref, flag_ref, *, n_cmp):
    g = pl.program_id(1)
    i = pl.program_id(2)
    t = NSA_TILE
    kc = kc_ref[0, 0, 0]
    vc = vc_ref[0, 0, 0]
    ncp = kc.shape[0]
    row = lax.broadcasted_iota(jnp.int32, (t, ncp), 0)
    c = lax.broadcasted_iota(jnp.int32, (t, ncp), 1)
    dist = i * t + row - (CMP_STRIDE * c + (CMP_LEN - 1))
    validf = jnp.where(dist >= 0, jnp.where(c < n_cmp, 1.0, 0.0), 0.0)
    valid = validf > 0.5
    distf = dist.astype(F32)
    gates = jax.nn.sigmoid(gate_ref[...])
    psum = jnp.zeros((t, ncp), F32)
    for h in range(NSA_HPG):
        hs = slice(h * LANES, (h + 1) * LANES)
        s = _dot_nt(q_ref[:, hs], kc)
        s = jnp.where(valid, s - slopes_ref[g * NSA_HPG + h] * distf, NEG)
        e = jnp.exp(s - jnp.max(s, axis=-1, keepdims=True))
        p = e * (validf / jnp.sum(e, axis=-1, keepdims=True))
        oc_ref[:, hs] = (gates[:, 3 * h:3 * h + 1] * _dot(p.astype(BF16), vc)).astype(BF16)
        psum = psum + p

    ov = ov_ref[...]
    hi, mid, lo = _split3(psum)
    imp = _dot(hi, ov) + _dot(mid, ov) + _dot(lo, ov)

    qrow = lax.broadcasted_iota(jnp.int32, (t, NS_PAD), 0)
    blk = lax.broadcasted_iota(jnp.int32, (t, NS_PAD), 1)
    blk_t = (i * t + qrow) // SLC_LEN
    forced = jnp.where(blk == 0, 1.0, 0.0) + jnp.where(blk == blk_t, 1.0, 0.0) + jnp.where(blk == blk_t - 1, 1.0, 0.0)
    imp = jnp.where(forced > 0.5, FORCE_SCORE, jnp.where(blk <= blk_t, imp, -1.0))
    blkf = blk.astype(F32)
    sel = jnp.zeros((t, NS_PAD), F32)
    for _ in range(SLC_TOPK):
        best = jnp.max(imp, axis=-1, keepdims=True)
        pick = jnp.min(jnp.where(imp == best, blkf, float(NS_PAD)), axis=-1, keepdims=True)
        hit = blkf == pick
        sel = jnp.where(hit, 1.0, sel)
        imp = jnp.where(hit, -3e38, imp)
    sel_ref[...] = sel.astype(BF16)
    flag_ref[0] = jnp.max(sel, axis=0, keepdims=True).astype(jnp.int32)


def _nsa_cmp(slopes, q, kvc, ov, gates, batch, seq, n_cmp):
    t = NSA_TILE
    nq = seq // t
    n = q.shape[0]
    gw = NSA_HPG * LANES
    ncp = kvc.shape[3]
    rows = lambda c: pl.BlockSpec((t, c), lambda b, g, i, *_: (b * nq + i, g))
    kv = lambda which: pl.BlockSpec((1, 1, 1, ncp, LANES), lambda b, g, i, *_: (which, b, g, 0, 0))
    grid_spec = pltpu.PrefetchScalarGridSpec(
        num_scalar_prefetch=1,
        grid=(batch, NSA_GROUPS, nq),
        in_specs=[rows(gw), kv(0), kv(1), pl.BlockSpec((ncp, NS_PAD), lambda b, g, i, *_: (0, 0)), rows(LANES)],
        out_specs=[rows(gw), rows(NS_PAD),
                   pl.BlockSpec((1, 1, NS_PAD), lambda b, g, i, *_: ((b * NSA_GROUPS + g) * nq + i, 0, 0))],
    )
    return pl.pallas_call(
        functools.partial(_nsa_cmp_kernel, n_cmp=n_cmp),
        grid_spec=grid_spec,
        out_shape=[jax.ShapeDtypeStruct((n, NSA_GROUPS * gw), BF16),
                   jax.ShapeDtypeStruct((n, NSA_GROUPS * NS_PAD), BF16),
                   jax.ShapeDtypeStruct((batch * NSA_GROUPS * nq, 1, NS_PAD), jnp.int32)],
        compiler_params=_params(3),
        name="nsa_cmp",
    )(slopes, q, kvc, kvc, ov, gates)


FLAG_BITS = 16


def _nsa_attn_kernel(words_ref, slopes_ref, q_ref, sel_ref, oc_ref, gate_ref, ks_ref, vs_ref, kw_ref, vw_ref, o_ref,
                     q_scr, m_s, l_s, acc_s, m_w, l_w, acc_w, *, nq, words_per_step):
    b = pl.program_id(0)
    g = pl.program_id(1)
    i = pl.program_id(2)
    t = NSA_TILE
    hpg = NSA_HPG
    for h in range(hpg):
        q_scr[h * t:(h + 1) * t, :] = q_ref[:, h * LANES:(h + 1) * LANES]
    for m_ref, l_ref, acc_ref in ((m_s, l_s, acc_s), (m_w, l_w, acc_w)):
        m_ref[...] = jnp.full_like(m_ref, NEG)
        l_ref[...] = jnp.zeros_like(l_ref)
        acc_ref[...] = jnp.zeros_like(acc_ref)

    row = lax.broadcasted_iota(jnp.int32, (t, t), 0)
    col = lax.broadcasted_iota(jnp.int32, (t, t), 1)
    hidx = lax.broadcasted_iota(jnp.int32, (hpg, 1, 1), 0)
    slope3 = jnp.zeros((hpg, 1, 1), F32)
    for h in range(hpg):
        slope3 = jnp.where(hidx == h, slopes_ref[g * hpg + h], slope3)
    sel = sel_ref[...]
    blocks_per_tile = t // SLC_LEN

    def attend(k_ref, v_ref, j, maskf_fn, m_ref, l_ref, acc_ref):
        ks = pl.ds(pl.multiple_of(j * t, t), t)
        s = _dot_nt(q_scr[...], k_ref[ks, :])
        dist = (i - j) * t + row - col
        keep = maskf_fn(dist, j) > 0.5
        s3 = s.reshape(hpg, t, t) - slope3 * dist.astype(F32)[None]
        s3 = jnp.where(keep[None], s3, NEG)
        m_prev = m_ref[...]
        m_new = jnp.maximum(m_prev, jnp.max(s3, axis=-1, keepdims=True))
        alpha = jnp.exp(m_prev - m_new)
        p = jnp.exp(s3 - m_new)
        l_ref[...] = alpha * l_ref[...] + jnp.sum(p, axis=-1, keepdims=True)
        pv = _dot(p.reshape(hpg * t, t).astype(BF16), v_ref[ks, :])
        acc_ref[...] = alpha * acc_ref[...] + pv.reshape(hpg, t, LANES)
        m_ref[...] = m_new

    def slc_maskf(dist, j):
        expand = jnp.where(row == blocks_per_tile * j + col // SLC_LEN, 1.0, 0.0).astype(BF16)
        return jnp.where(dist >= 0, _dot(sel, expand), 0.0)

    def win_maskf(dist, j):
        return jnp.where(dist >= 0, jnp.where(dist < WINDOW, 1.0, 0.0), 0.0)

    step = (b * NSA_GROUPS + g) * nq + i

    def slc_body(j, carry):
        word = words_ref[step * words_per_step + j // FLAG_BITS]
        active = (lax.shift_right_logical(word, j % FLAG_BITS) & 1) == 1

        @pl.when(active)
        def _():
            attend(ks_ref, vs_ref, j, slc_maskf, m_s, l_s, acc_s)

        return carry

    lax.fori_loop(0, i + 1, slc_body, 0)

    def win_body(j, carry):
        attend(kw_ref, vw_ref, j, win_maskf, m_w, l_w, acc_w)
        return carry

    lax.fori_loop(jnp.maximum(i - WINDOW // t, 0), i + 1, win_body, 0)

    gates = jax.nn.sigmoid(gate_ref[...])
    for h in range(hpg):
        hs = slice(h * LANES, (h + 1) * LANES)
        o_s = acc_s[h] / l_s[h]
        o_w = acc_w[h] / l_w[h]
        o = oc_ref[:, hs].astype(F32) + gates[:, 3 * h + 1:3 * h + 2] * o_s + gates[:, 3 * h + 2:3 * h + 3] * o_w
        o_ref[:, hs] = o.astype(BF16)


def _nsa_attn(words, slopes, q, sel, oc, gates, kvsw, batch, seq):
    t = NSA_TILE
    nq = seq // t
    n = q.shape[0]
    gw = NSA_HPG * LANES
    words_per_step = words.shape[0] // (batch * NSA_GROUPS * nq)
    rows = lambda c: pl.BlockSpec((t, c), lambda b, g, i, *_: (b * nq + i, g))
    kv = lambda which: pl.BlockSpec((seq, LANES), lambda b, g, i, *_: (b, which * NSA_GROUPS + g))
    grid_spec = pltpu.PrefetchScalarGridSpec(
        num_scalar_prefetch=2,
        grid=(batch, NSA_GROUPS, nq),
        in_specs=[rows(gw), rows(NS_PAD), rows(gw), rows(LANES), kv(0), kv(1), kv(2), kv(3)],
        out_specs=rows(gw),
        scratch_shapes=[pltpu.VMEM((NSA_HPG * t, LANES), BF16)] + 2 * [
            pltpu.VMEM((NSA_HPG, t, 1), F32), pltpu.VMEM((NSA_HPG, t, 1), F32), pltpu.VMEM((NSA_HPG, t, LANES), F32)],
    )
    return pl.pallas_call(
        functools.partial(_nsa_attn_kernel, nq=nq, words_per_step=words_per_step),
        grid_spec=grid_spec,
        out_shape=jax.ShapeDtypeStruct((n, NSA_GROUPS * gw), BF16),
        compiler_params=_params(3),
        name="nsa_attn",
    )(words, slopes, q, sel, oc, gates, kvsw, kvsw, kvsw, kvsw)


def _outproj_kernel(x_ref, om_ref, on_ref, wm_ref, wn_ref, o_ref):
    o_ref[...] = x_ref[...] + _dot(om_ref[...], wm_ref[...]) + _dot(on_ref[...], wn_ref[...])


def _out_proj(x2, o_mla, o_nsa, wm, wn, tm=512):
    n, d = x2.shape
    row = lambda c: pl.BlockSpec((tm, c), lambda i: (i, 0))
    return pl.pallas_call(
        _outproj_kernel,
        grid=(n // tm,),
        in_specs=[row(d), row(o_mla.shape[1]), row(o_nsa.shape[1]), _resident(wm.shape), _resident(wn.shape)],
        out_specs=row(d),
        out_shape=jax.ShapeDtypeStruct((n, d), F32),
        compiler_params=_params(1),
        name="out_proj",
    )(x2, o_mla, o_nsa, wm, wn)


def _ffn_kernel(x_ref, g_ref, wg_ref, wu_ref, wd_ref, gf_ref, o_ref, h_scr, acc_scr):
    f = pl.program_id(1)

    @pl.when(f == 0)
    def _():
        h_scr[...] = _rmsnorm(x_ref[...], g_ref[...]).astype(BF16)
        acc_scr[...] = jnp.zeros_like(acc_scr)

    h = h_scr[...]
    gate = _dot(h, wg_ref[...])
    act = (gate * jax.nn.sigmoid(gate)) * _dot(h, wu_ref[...])
    acc_scr[...] += _dot(act.astype(BF16), wd_ref[...])

    @pl.when(f == pl.num_programs(1) - 1)
    def _():
        o_ref[...] = _rmsnorm(x_ref[...] + acc_scr[...], gf_ref[...])


def _ffn(x1, g, wg, wu, wd, gf, tm=512, tf=512):
    n, d = x1.shape
    dff = wg.shape[1]
    return pl.pallas_call(
        _ffn_kernel,
        grid=(n // tm, dff // tf),
        in_specs=[pl.BlockSpec((tm, d), lambda i, f: (i, 0)), _resident((1, d)),
                  pl.BlockSpec((d, tf), lambda i, f: (0, f)), pl.BlockSpec((d, tf), lambda i, f: (0, f)),
                  pl.BlockSpec((tf, d), lambda i, f: (f, 0)), _resident((1, d))],
        out_specs=pl.BlockSpec((tm, d), lambda i, f: (i, 0)),
        out_shape=jax.ShapeDtypeStruct((n, d), F32),
        scratch_shapes=[pltpu.VMEM((tm, d), BF16), pltpu.VMEM((tm, d), F32)],
        compiler_params=_params(2),
        name="ffn",
    )(x1, g, wg, wu, wd, gf)


def _pad_cols(w, width):
    return jnp.pad(w, ((0, 0), (0, width - w.shape[1])))


def _rot_cols(w):
    half = w.shape[1] // 2
    return jnp.concatenate([-w[:, half:], w[:, :half]], axis=1)


def _fused_in_weight(w_in):
    sizes = (MLA_Q_LORA, MLA_KV_LORA, MLA_ROPE, NSA_HEADS * NSA_DK) + (NSA_GROUPS * NSA_DK,) * 6 + (3 * NSA_HEADS,)
    offs = np.cumsum(sizes)[:-1].tolist()
    cq, ckv, kr, q, kc, vc, ks, vs, kw, vw, gate = jnp.split(w_in, offs, axis=1)
    d = w_in.shape[0]
    q_pad = _pad_cols((q * (NSA_DK ** -0.5)).reshape(d * NSA_HEADS, NSA_DK), LANES).reshape(d, NSA_HEADS * LANES)
    per_group = lambda w, c: _pad_cols(w.reshape(d * NSA_GROUPS, c), LANES).reshape(d, NSA_GROUPS * LANES)
    kr_rot = _rot_cols(kr)
    cols = [cq, ckv, kr, kr, kr_rot, kr_rot, q_pad, kc, vc] + [per_group(w, NSA_DK) for w in (ks, vs, kw, vw)]
    cols.append(per_group(gate, 3 * NSA_HPG))
    w = jnp.concatenate(cols, axis=1).astype(BF16)
    assert w.shape[1] == _IN_COLS["gate"][1]
    return w


def _fused_uq_weight(w_uq):
    d = w_uq.shape[0]
    w = w_uq.reshape(d, MLA_HEADS, MLA_NOPE + MLA_ROPE)
    nope = w[:, :, :MLA_NOPE].reshape(d, MLA_HEADS * MLA_NOPE)
    rope = w[:, :, MLA_NOPE:]
    rope_rot = jnp.concatenate([-rope[:, :, MLA_ROPE // 2:], rope[:, :, :MLA_ROPE // 2]], axis=2)
    flat = lambda r: r.reshape(d, MLA_HEADS * MLA_ROPE)
    return jnp.concatenate([nope, flat(rope), flat(rope_rot)], axis=1).astype(BF16)


def _rope_tables(seq):
    inv = ROPE_THETA ** (-jnp.arange(0, MLA_ROPE, 2, dtype=F32) / MLA_ROPE)
    ang = jnp.arange(seq, dtype=F32)[:, None] * inv[None, :]
    reps = 2 * LANES // MLA_ROPE
    return jnp.tile(jnp.cos(ang), (1, reps)), jnp.tile(jnp.sin(ang), (1, reps))


def _overlap_matrix(seq, n_cmp, ncp):
    cmp_start = CMP_STRIDE * np.arange(n_cmp)
    slc_start = SLC_LEN * np.arange(seq // SLC_LEN)
    ov = np.clip(np.minimum(cmp_start[:, None] + CMP_LEN, slc_start[None, :] + SLC_LEN)
                 - np.maximum(cmp_start[:, None], slc_start[None, :]), 0, None).astype(np.float32) / CMP_STRIDE
    out = np.zeros((ncp, NS_PAD), np.float32)
    out[:n_cmp, :ov.shape[1]] = ov
    return jnp.asarray(out, BF16)


def _pack_flags(flags):
    t_blocks = NSA_TILE // SLC_LEN
    steps = flags.shape[0]
    tiles = flags.reshape(steps, NS_PAD // t_blocks, t_blocks).max(axis=-1)
    words = tiles.reshape(steps, -1, FLAG_BITS) * (1 << jnp.arange(FLAG_BITS, dtype=jnp.int32))
    return words.sum(axis=-1).astype(jnp.int32).reshape(-1)


def kernel(x, attn_norm_g, w_in, mla_q_norm_g, mla_kv_norm_g, w_uq, w_uk, w_uv, cmp_pos_k, cmp_pos_v, w_cmp_k1,
           w_cmp_k2, w_cmp_v1, w_cmp_v2, w_o, ffn_norm_g, w_gate, w_up, w_down, final_norm_g):
    batch, seq, d = x.shape
    n = batch * seq
    depth = w_in.shape[0]
    assert depth == 1, "the final RMSNorm is fused into the FFN kernel of a single layer"
    assert seq % 512 == 0 and seq // SLC_LEN <= NS_PAD
    n_cmp = (seq - CMP_LEN) // CMP_STRIDE + 1
    ncp = seq // CMP_STRIDE
    cos_t, sin_t = _rope_tables(seq)
    ov = _overlap_matrix(seq, n_cmp, ncp)
    slopes = 2.0 ** (-8.0 * jnp.arange(1, NSA_HEADS + 1, dtype=F32) / NSA_HEADS)
    x2 = x.reshape(n, d)
    for l in range(depth):
        cq, ckv, kr, q, kvc, kvsw, gates = _in_proj(
            x2, attn_norm_g[l][None], _fused_in_weight(w_in[l]), cos_t, sin_t, seq)

        wkv = jnp.concatenate([w_uk[l], w_uv[l]], axis=1).astype(BF16)
        qn, qr, kn, v = _mla_up(cq, ckv, mla_q_norm_g[l][None], mla_kv_norm_g[l][None], _fused_uq_weight(w_uq[l]),
                                wkv, cos_t, sin_t, seq)
        o_mla = _mla_attn(qn, qr, kn, kr, v, batch, seq)

        r = kvc.reshape(batch, seq, 2, NSA_GROUPS, NSA_DK).transpose(2, 0, 3, 1, 4)
        r = r.reshape(2, batch, NSA_GROUPS, ncp, CMP_STRIDE * NSA_DK)
        pos = jnp.stack([cmp_pos_k[l].reshape(1, -1), cmp_pos_v[l].reshape(1, -1)])
        w1 = jnp.stack([w_cmp_k1[l], w_cmp_v1[l]]).astype(BF16)
        w2 = jnp.stack([_pad_cols(w_cmp_k2[l], LANES), _pad_cols(w_cmp_v2[l], LANES)]).astype(BF16)
        kvc_cmp = _compress(r, pos, w1, w2)

        oc, sel, flags = _nsa_cmp(slopes, q, kvc_cmp, ov, gates, batch, seq, n_cmp)
        o_nsa = _nsa_attn(_pack_flags(flags), slopes, q, sel, oc, gates, kvsw, batch, seq)

        wm = w_o[l][:MLA_HEADS * MLA_V].astype(BF16)
        wn = w_o[l][MLA_HEADS * MLA_V:].reshape(NSA_HEADS, NSA_DK, d)
        wn = jnp.pad(wn, ((0, 0), (0, LANES - NSA_DK), (0, 0))).reshape(NSA_HEADS * LANES, d).astype(BF16)
        x1 = _out_proj(x2, o_mla, o_nsa, wm, wn)

        x2 = _ffn(x1, ffn_norm_g[l][None], w_gate[l].astype(BF16), w_up[l].astype(BF16), w_down[l].astype(BF16),
                  final_norm_g[None])
    return x2.reshape(batch, seq, d)
```

```python
import functools
import math

import numpy as np
import jax
import jax.numpy as jnp
from jax import lax
from jax.experimental import pallas as pl
from jax.experimental.pallas import tpu as pltpu

F32 = jnp.float32
BF16 = jnp.bfloat16

EPS = 1e-6
NEG = -1e30
LOG2E = math.log2(math.e)
LANES = 128

MLA_HEADS = 8
MLA_Q_LORA = 512
MLA_KV_LORA = 256
MLA_NOPE = 128
MLA_ROPE = 64
MLA_V = 128
ROPE_THETA = 10000.0
MLA_TK = 256

NSA_HEADS = 16
NSA_GROUPS = 2
NSA_HPG = NSA_HEADS // NSA_GROUPS
NSA_DK = 64
CMP_LEN = 32
CMP_STRIDE = 16
CMP_HIDDEN = 128
SLC_LEN = 64
SLC_TOPK = 16
WINDOW = 512
FORCE_SCORE = 1e4
NSA_TILE = 128
NS_PAD = 128
N_GATES = 3 * NSA_HPG
POS_SPLIT = 64

VMEM_LIMIT = 56 * 1024 * 1024


def _params(n_axes):
    return pltpu.CompilerParams(dimension_semantics=("arbitrary",) * n_axes, vmem_limit_bytes=VMEM_LIMIT)


def _resident(shape):
    zeros = (0,) * len(shape)
    return pl.BlockSpec(shape, lambda *_: zeros, pipeline_mode=pl.Buffered(1))


def _rmsnorm(x, g):
    return x * lax.rsqrt(jnp.mean(x * x, axis=-1, keepdims=True) + EPS) * g


def _dot(a, b):
    return jnp.dot(a, b, preferred_element_type=F32)


def _dot_nt(a, b):
    return lax.dot_general(a, b, (((1,), (1,)), ((), ())), preferred_element_type=F32)


def _flash_update(s, v_t, m_ref, l_ref, acc_ref, idx):
    m_prev = m_ref[idx]
    m_new = jnp.maximum(m_prev, jnp.max(s, axis=0, keepdims=True))
    alpha = jnp.exp2(m_prev - m_new)
    p = jnp.exp2(s - m_new)
    l_ref[idx] = alpha * l_ref[idx] + jnp.sum(p, axis=0, keepdims=True)
    acc_ref[idx] = alpha * acc_ref[idx] + _dot(v_t, p.astype(BF16))
    m_ref[idx] = m_new


_IN_COLS = dict(cq=(0, 512), ckv=(512, 768), kra=(768, 896), krb=(896, 1024), q=(1024, 3072),
                kvc=(3072, 3328), ksw=(3328, 3840), gate=(3840, 4096))
_VT_ROWS = 2 * NSA_GROUPS * NSA_DK


def _inproj_kernel(x_ref, g_ref, w_ref, wt_ref, qfeat_ref, cos_ref, sin_ref, kfeat_ref, cq_ref, ckv_ref, kr_ref,
                   q_ref, kvc_ref, ksw_ref, gate_ref, vt_ref, gt_ref):
    h = _rmsnorm(x_ref[...], g_ref[...]).astype(BF16)

    def mm(name):
        lo, hi = _IN_COLS[name]
        return _dot(h, w_ref[:, lo:hi])

    cq_ref[...] = mm("cq")
    ckv_ref[...] = mm("ckv")
    kr_ref[...] = (mm("kra") * cos_ref[...] + mm("krb") * sin_ref[...]).astype(BF16)
    q_ref[...] = (mm("q") + qfeat_ref[...]).astype(BF16)
    kvc_ref[...] = mm("kvc")
    kfeat = kfeat_ref[...]
    ksw_ref[...] = (mm("ksw") + jnp.concatenate([kfeat] * (2 * NSA_GROUPS), axis=1)).astype(BF16)
    gate_ref[...] = mm("gate")
    t = _dot_nt(wt_ref[...], h)
    for c in range(vt_ref.shape[0]):
        cs = slice(c * NSA_TILE, (c + 1) * NSA_TILE)
        vt_ref[c] = t[:_VT_ROWS, cs].astype(BF16)
        gt_ref[c] = t[_VT_ROWS:_VT_ROWS + NSA_GROUPS * N_GATES, cs]


def _in_proj(x2, g, w, wt, qfeat, cos_t, sin_t, kfeat, seq, tm=256):
    n, d = x2.shape
    nt = seq // tm
    widths = {k: hi - lo for k, (lo, hi) in _IN_COLS.items()}
    row = lambda c: pl.BlockSpec((tm, c), lambda i: (i, 0))
    tab = pl.BlockSpec((tm, LANES), lambda i: (i % nt, 0))
    outs = [("cq", F32), ("ckv", F32), ("kra", BF16), ("q", BF16), ("kvc", F32), ("ksw", BF16), ("gate", F32)]
    tiles = tm // NSA_TILE
    tspec = lambda rows: pl.BlockSpec((tiles, rows, NSA_TILE), lambda i: (i, 0, 0))
    return pl.pallas_call(
        _inproj_kernel,
        grid=(n // tm,),
        in_specs=[row(d), _resident((1, d)), _resident(w.shape), _resident(wt.shape), _resident(qfeat.shape),
                  tab, tab, tab],
        out_specs=[row(widths[k]) for k, _ in outs] + [tspec(_VT_ROWS), tspec(NSA_GROUPS * N_GATES)],
        out_shape=[jax.ShapeDtypeStruct((n, widths[k]), dt) for k, dt in outs] + [
            jax.ShapeDtypeStruct((n // NSA_TILE, _VT_ROWS, NSA_TILE), BF16),
            jax.ShapeDtypeStruct((n // NSA_TILE, NSA_GROUPS * N_GATES, NSA_TILE), F32)],
        compiler_params=_params(1),
        name="in_proj",
    )(x2, g, w, wt, qfeat, cos_t, sin_t, kfeat)


def _mlaup_kernel(cq_ref, ckv_ref, gq_ref, gkv_ref, wq_ref, wk_ref, wvt_ref, cos_ref, sin_ref, qn_ref, qr_ref,
                  kn_ref, vt_ref, *, scale):
    cqn = _rmsnorm(cq_ref[...], gq_ref[...]).astype(BF16)
    ckvn = _rmsnorm(ckv_ref[...], gkv_ref[...]).astype(BF16)
    hn = MLA_HEADS * MLA_NOPE
    hr = MLA_HEADS * MLA_ROPE
    qn_ref[...] = (_dot(cqn, wq_ref[:, :hn]) * scale).astype(BF16)
    a = _dot(cqn, wq_ref[:, hn:hn + hr])
    b = _dot(cqn, wq_ref[:, hn + hr:])
    cos = cos_ref[...]
    sin = sin_ref[...]
    for p in range(hr // LANES):
        sl = slice(p * LANES, (p + 1) * LANES)
        qr_ref[:, sl] = ((a[:, sl] * cos + b[:, sl] * sin) * scale).astype(BF16)
    kn_ref[...] = _dot(ckvn, wk_ref[...]).astype(BF16)
    v_t = _dot_nt(wvt_ref[...], ckvn)
    for c in range(vt_ref.shape[0]):
        vt_ref[c] = v_t[:, c * MLA_TK:(c + 1) * MLA_TK].astype(BF16)


def _mla_up(cq, ckv, gq, gkv, wq, wk, wvt, cos_t, sin_t, seq, tm=512):
    n = cq.shape[0]
    nt = seq // tm
    hn = MLA_HEADS * MLA_NOPE
    hr = MLA_HEADS * MLA_ROPE
    hv = MLA_HEADS * MLA_V
    row = lambda c: pl.BlockSpec((tm, c), lambda i: (i, 0))
    tab = pl.BlockSpec((tm, LANES), lambda i: (i % nt, 0))
    scale = (MLA_NOPE + MLA_ROPE) ** -0.5 * LOG2E
    return pl.pallas_call(
        functools.partial(_mlaup_kernel, scale=scale),
        grid=(n // tm,),
        in_specs=[row(MLA_Q_LORA), row(MLA_KV_LORA), _resident(gq.shape), _resident(gkv.shape),
                  _resident(wq.shape), _resident(wk.shape), _resident(wvt.shape), tab, tab],
        out_specs=[row(hn), row(hr), row(hn), pl.BlockSpec((tm // MLA_TK, hv, MLA_TK), lambda i: (i, 0, 0))],
        out_shape=[jax.ShapeDtypeStruct((n, hn), BF16), jax.ShapeDtypeStruct((n, hr), BF16),
                   jax.ShapeDtypeStruct((n, hn), BF16), jax.ShapeDtypeStruct((n // MLA_TK, hv, MLA_TK), BF16)],
        compiler_params=_params(1),
        name="mla_up",
    )(cq, ckv, gq, gkv, wq, wk, wvt, cos_t, sin_t)


def _mla_attn_kernel(qn_ref, qr_ref, kn_ref, kr_ref, vt_ref, o_ref, m_ref, l_ref, acc_ref, *, tq):
    i = pl.program_id(2)
    tk = MLA_TK
    lane = lax.broadcasted_iota(jnp.int32, (tq, LANES), 1)
    krow = lax.broadcasted_iota(jnp.int32, (tk, tq), 0)
    qcol = lax.broadcasted_iota(jnp.int32, (tk, tq), 1)
    qr = qr_ref[...]
    zero = jnp.zeros_like(qr)
    q_cat = [jnp.concatenate([qn_ref[:, hh * LANES:(hh + 1) * LANES],
                              jnp.where((lane < MLA_ROPE) == (hh == 0), qr, zero)], axis=1) for hh in range(2)]
    m_ref[...] = jnp.full_like(m_ref, NEG)
    l_ref[...] = jnp.zeros_like(l_ref)
    acc_ref[...] = jnp.zeros_like(acc_ref)

    def step(j, diagonal):
        ks = pl.ds(pl.multiple_of(j * tk, tk), tk)
        k_rope = kr_ref[ks, :]
        for hh in range(2):
            hs = slice(hh * LANES, (hh + 1) * LANES)
            s = _dot_nt(jnp.concatenate([kn_ref[ks, hs], k_rope], axis=1), q_cat[hh])
            if diagonal:
                s = jnp.where(j * tk + krow <= i * tq + qcol, s, NEG)
            _flash_update(s, vt_ref[0, j, hs, :], m_ref, l_ref, acc_ref, hh)

    def body(j, carry):
        step(j, False)
        return carry

    full = i * (tq // tk)
    lax.fori_loop(0, full, body, 0)
    for d in range(tq // tk):
        step(full + d, True)
    for hh in range(2):
        o_t = acc_ref[hh] / l_ref[hh]
        for c in range(tq // LANES):
            o_ref[c * LANES:(c + 1) * LANES, hh * LANES:(hh + 1) * LANES] = (
                o_t[:, c * LANES:(c + 1) * LANES].T.astype(BF16))


def _mla_attn(qn, qr, kn, kr, vt, batch, seq, tq=512):
    n = qn.shape[0]
    nq = seq // tq
    nk = seq // MLA_TK
    pairs = MLA_HEADS // 2
    pw = 2 * LANES
    qspec = lambda c: pl.BlockSpec((tq, c), lambda b, p, i: (b * nq + i, p))
    vt4 = vt.reshape(batch, nk, MLA_HEADS * MLA_V, MLA_TK)
    return pl.pallas_call(
        functools.partial(_mla_attn_kernel, tq=tq),
        grid=(batch, pairs, nq),
        in_specs=[qspec(pw), qspec(LANES), pl.BlockSpec((seq, pw), lambda b, p, i: (b, p)),
                  pl.BlockSpec((seq, LANES), lambda b, p, i: (b, 0)),
                  pl.BlockSpec((1, nk, pw, MLA_TK), lambda b, p, i: (b, 0, p, 0))],
        out_specs=qspec(pw),
        out_shape=jax.ShapeDtypeStruct((n, MLA_HEADS * MLA_V), BF16),
        scratch_shapes=[pltpu.VMEM((2, 1, tq), F32), pltpu.VMEM((2, 1, tq), F32), pltpu.VMEM((2, MLA_V, tq), F32)],
        compiler_params=_params(3),
        name="mla_attn",
    )(qn, qr, kn, kr, vt4)


def _compress_kernel(r_ref, pos_ref, w1_ref, w2_ref, o_ref):
    half = CMP_STRIDE * NSA_DK
    r = r_ref[0, 0, 0]
    pos = pos_ref[0]
    a = _dot((r + pos[:, :half]).astype(BF16), w1_ref[0, :half, :])
    b = _dot((r + pos[:, half:]).astype(BF16), w1_ref[0, half:, :])
    hid = a + pltpu.roll(b, b.shape[0] - 1, 0)
    act = hid * jax.nn.sigmoid(hid)
    o_ref[0, 0, 0] = _dot(act.astype(BF16), w2_ref[0]).astype(BF16)


def _compress(r, pos, w1, w2):
    _, batch, groups, nc, width = r.shape
    return pl.pallas_call(
        _compress_kernel,
        grid=(2, batch, groups),
        in_specs=[pl.BlockSpec((1, 1, 1, nc, width), lambda t, b, g: (t, b, g, 0, 0)),
                  pl.BlockSpec((1, 1, 2 * width), lambda t, b, g: (t, 0, 0)),
                  pl.BlockSpec((1, 2 * width, CMP_HIDDEN), lambda t, b, g: (t, 0, 0)),
                  pl.BlockSpec((1, CMP_HIDDEN, LANES), lambda t, b, g: (t, 0, 0))],
        out_specs=pl.BlockSpec((1, 1, 1, nc, LANES), lambda t, b, g: (t, b, g, 0, 0)),
        out_shape=jax.ShapeDtypeStruct((2, batch, groups, nc, LANES), BF16),
        compiler_params=_params(3),
        name="compress",
    )(r, pos, w1, w2)


def _split3(x):
    hi = x.astype(BF16)
    r = x - hi.astype(F32)
    mid = r.astype(BF16)
    lo = (r - mid.astype(F32)).astype(BF16)
    return hi, mid, lo


def _nsa_cmp_kernel(slopes_ref, q_ref, kc_ref, vc_ref, ov_ref, gate_ref, oc_ref, selt_ref, flag_ref, *, n_cmp):
    g = pl.program_id(1)
    i = pl.program_id(2)
    t = NSA_TILE
    kc = kc_ref[0, 0, 0]
    vc = vc_ref[0, 0, 0]
    ncp = kc.shape[0]
    row = lax.broadcasted_iota(jnp.int32, (t, ncp), 0)
    c = lax.broadcasted_iota(jnp.int32, (t, ncp), 1)
    dist = i * t + row - (CMP_STRIDE * c + (CMP_LEN - 1))
    validf = jnp.where(dist >= 0, jnp.where(c < n_cmp, 1.0, 0.0), 0.0)
    valid = validf > 0.5
    distf = dist.astype(F32)
    gates = jax.nn.sigmoid(gate_ref[...])
    psum = jnp.zeros((t, ncp), F32)
    for h in range(NSA_HPG):
        s = _dot_nt(q_ref[:, h * LANES:(h + 1) * LANES], kc)
        s = jnp.where(valid, s - slopes_ref[g * NSA_HPG + h] * distf, NEG)
        e = jnp.exp2(s - jnp.max(s, axis=-1, keepdims=True))
        p = e * (validf / jnp.sum(e, axis=-1, keepdims=True))
        o = gates[:, 3 * h:3 * h + 1] * _dot(p.astype(BF16), vc)
        if h % 2 == 0:
            o_even = o
        else:
            oc_ref[:, (h // 2) * LANES:(h // 2 + 1) * LANES] = (o_even + pltpu.roll(o, NSA_DK, 1)).astype(BF16)
        psum = psum + p

    ov = ov_ref[...]
    hi, mid, lo = _split3(psum)
    imp = _dot(hi, ov) + _dot(mid, ov) + _dot(lo, ov)

    qrow = lax.broadcasted_iota(jnp.int32, (t, NS_PAD), 0)
    blk = lax.broadcasted_iota(jnp.int32, (t, NS_PAD), 1)
    blk_t = (i * t + qrow) // SLC_LEN
    forced = jnp.where(blk == 0, 1.0, 0.0) + jnp.where(blk == blk_t, 1.0, 0.0) + jnp.where(blk == blk_t - 1, 1.0, 0.0)
    imp = jnp.where(forced > 0.5, FORCE_SCORE, jnp.where(blk <= blk_t, imp, -1.0))
    blkf = blk.astype(F32)
    sel = jnp.zeros((t, NS_PAD), F32)
    for _ in range(SLC_TOPK):
        best = jnp.max(imp, axis=-1, keepdims=True)
        pick = jnp.min(jnp.where(imp == best, blkf, float(NS_PAD)), axis=-1, keepdims=True)
        hit = blkf == pick
        sel = jnp.where(hit, 1.0, sel)
        imp = jnp.where(hit, -3e38, imp)
    selt_ref[0, 0] = sel.T
    flag_ref[0] = jnp.max(sel, axis=0, keepdims=True).astype(jnp.int32)


def _nsa_cmp(slopes, q, kvc, ov, gates, batch, seq, n_cmp):
    t = NSA_TILE
    nq = seq // t
    n = q.shape[0]
    ncp = kvc.shape[3]
    rows = lambda c: pl.BlockSpec((t, c), lambda b, g, i, *_: (b * nq + i, g))
    kv = lambda which: pl.BlockSpec((1, 1, 1, ncp, LANES), lambda b, g, i, *_: (which, b, g, 0, 0))
    grid_spec = pltpu.PrefetchScalarGridSpec(
        num_scalar_prefetch=1,
        grid=(batch, NSA_GROUPS, nq),
        in_specs=[rows(NSA_HPG * LANES), kv(0), kv(1), pl.BlockSpec((ncp, NS_PAD), lambda b, g, i, *_: (0, 0)),
                  rows(LANES)],
        out_specs=[rows(NSA_HPG * NSA_DK),
                   pl.BlockSpec((1, 1, NS_PAD, t), lambda b, g, i, *_: (b, g, 0, i)),
                   pl.BlockSpec((1, 1, NS_PAD), lambda b, g, i, *_: ((b * NSA_GROUPS + g) * nq + i, 0, 0))],
    )
    return pl.pallas_call(
        functools.partial(_nsa_cmp_kernel, n_cmp=n_cmp),
        grid_spec=grid_spec,
        out_shape=[jax.ShapeDtypeStruct((n, NSA_HEADS * NSA_DK), BF16),
                   jax.ShapeDtypeStruct((batch, NSA_GROUPS, NS_PAD, seq), F32),
                   jax.ShapeDtypeStruct((batch * NSA_GROUPS * nq, 1, NS_PAD), jnp.int32)],
        compiler_params=_params(3),
        name="nsa_cmp",
    )(slopes, q, kvc, kvc, ov, gates)


FLAG_BITS = 16


def _nsa_attn_kernel(words_ref, q_ref, selt_ref, oc_ref, gt_ref, ks_ref, kw_ref, vst_ref, vwt_ref, o_ref,
                     q_scr, m_ref, l_ref, acc_ref, *, nq, words_per_step):
    b = pl.program_id(0)
    g = pl.program_id(1)
    i = pl.program_id(2)
    t = NSA_TILE
    hpg = NSA_HPG
    for h in range(hpg):
        q_scr[h * t:(h + 1) * t, :] = q_ref[:, h * LANES:(h + 1) * LANES]
    m_ref[...] = jnp.full_like(m_ref, NEG)
    l_ref[...] = jnp.zeros_like(l_ref)
    acc_ref[...] = jnp.zeros_like(acc_ref)

    krow = lax.broadcasted_iota(jnp.int32, (t, t), 0)
    qcol = lax.broadcasted_iota(jnp.int32, (t, t), 1)
    blocks_per_tile = t // SLC_LEN
    slc, win = 0, 1

    def attend(branch, k_ref, vt_ref, j, keepf):
        ks = pl.ds(pl.multiple_of(j * t, t), t)
        bias = jnp.where(keepf > 0.5, 0.0, NEG)
        s = _dot_nt(k_ref[ks, :], q_scr[...]) + jnp.concatenate([bias] * hpg, axis=1)
        _flash_update(s, vt_ref[0, j], m_ref, l_ref, acc_ref, branch)

    step = (b * NSA_GROUPS + g) * nq + i

    def slc_body(j, carry):
        word = words_ref[step * words_per_step + j // FLAG_BITS]
        active = (lax.shift_right_logical(word, j % FLAG_BITS) & 1) == 1

        @pl.when(active)
        def _():
            dist = (i - j) * t + qcol - krow
            sel_rows = [selt_ref[0, 0, pl.ds(blocks_per_tile * j + r, 1), :] for r in range(blocks_per_tile)]
            picked = sel_rows[-1]
            for r in range(blocks_per_tile - 2, -1, -1):
                picked = jnp.where(krow < (r + 1) * SLC_LEN, sel_rows[r], picked)
            attend(slc, ks_ref, vst_ref, j, jnp.where(dist >= 0, picked, 0.0))

        return carry

    lax.fori_loop(0, i + 1, slc_body, 0)

    def win_body(j, carry):
        dist = (i - j) * t + qcol - krow
        attend(win, kw_ref, vwt_ref, j, jnp.where(dist >= 0, jnp.where(dist < WINDOW, 1.0, 0.0), 0.0))
        return carry

    lax.fori_loop(jnp.maximum(i - WINDOW // t, 0), i + 1, win_body, 0)

    gates = jax.nn.sigmoid(gt_ref[0])
    inv_s = 1.0 / l_ref[slc]
    inv_w = 1.0 / l_ref[win]
    for pair in range(hpg // 2):
        halves = []
        for h in (2 * pair, 2 * pair + 1):
            hs = slice(h * t, (h + 1) * t)
            halves.append(gates[3 * h + 1:3 * h + 2, :] * (acc_ref[slc, :, hs] * inv_s[:, hs])
                          + gates[3 * h + 2:3 * h + 3, :] * (acc_ref[win, :, hs] * inv_w[:, hs]))
        ps = slice(pair * LANES, (pair + 1) * LANES)
        o_ref[:, ps] = (oc_ref[:, ps].astype(F32) + jnp.concatenate(halves, axis=0).T).astype(BF16)


def _nsa_attn(words, q, selt, oc, gt, ksw, vt, batch, seq):
    t = NSA_TILE
    nq = seq // t
    n = q.shape[0]
    words_per_step = words.shape[0] // (batch * NSA_GROUPS * nq)
    rows = lambda c: pl.BlockSpec((t, c), lambda b, g, i, *_: (b * nq + i, g))
    key = lambda which: pl.BlockSpec((seq, LANES), lambda b, g, i, *_: (b, which * NSA_GROUPS + g))
    vt4 = vt.reshape(batch, nq, _VT_ROWS, t)
    val = lambda which: pl.BlockSpec((1, nq, NSA_DK, t), lambda b, g, i, *_: (b, 0, which * NSA_GROUPS + g, 0))
    grid_spec = pltpu.PrefetchScalarGridSpec(
        num_scalar_prefetch=1,
        grid=(batch, NSA_GROUPS, nq),
        in_specs=[rows(NSA_HPG * LANES),
                  pl.BlockSpec((1, 1, NS_PAD, t), lambda b, g, i, *_: (b, g, 0, i)),
                  rows(NSA_HPG * NSA_DK),
                  pl.BlockSpec((1, N_GATES, t), lambda b, g, i, *_: (b * nq + i, g, 0)),
                  key(0), key(1), val(0), val(1)],
        out_specs=rows(NSA_HPG * NSA_DK),
        scratch_shapes=[pltpu.VMEM((NSA_HPG * t, LANES), BF16), pltpu.VMEM((2, 1, NSA_HPG * t), F32),
                        pltpu.VMEM((2, 1, NSA_HPG * t), F32), pltpu.VMEM((2, NSA_DK, NSA_HPG * t), F32)],
    )
    return pl.pallas_call(
        functools.partial(_nsa_attn_kernel, nq=nq, words_per_step=words_per_step),
        grid_spec=grid_spec,
        out_shape=jax.ShapeDtypeStruct((n, NSA_HEADS * NSA_DK), BF16),
        compiler_params=_params(3),
        name="nsa_attn",
    )(words, q, selt, oc, gt, ksw, ksw, vt4, vt4)


def _outproj_kernel(x_ref, om_ref, on_ref, wm_ref, wn_ref, o_ref):
    o_ref[...] = x_ref[...] + _dot(om_ref[...], wm_ref[...]) + _dot(on_ref[...], wn_ref[...])


def _out_proj(x2, o_mla, o_nsa, wm, wn, tm=512):
    n, d = x2.shape
    row = lambda c: pl.BlockSpec((tm, c), lambda i: (i, 0))
    return pl.pallas_call(
        _outproj_kernel,
        grid=(n // tm,),
        in_specs=[row(d), row(o_mla.shape[1]), row(o_nsa.shape[1]), _resident(wm.shape), _resident(wn.shape)],
        out_specs=row(d),
        out_shape=jax.ShapeDtypeStruct((n, d), F32),
        compiler_params=_params(1),
        name="out_proj",
    )(x2, o_mla, o_nsa, wm, wn)


def _ffn_kernel(x_ref, g_ref, wg_ref, wu_ref, wd_ref, gf_ref, o_ref, h_scr, acc_scr):
    f = pl.program_id(1)

    @pl.when(f == 0)
    def _():
        h_scr[...] = _rmsnorm(x_ref[...], g_ref[...]).astype(BF16)
        acc_scr[...] = jnp.zeros_like(acc_scr)

    h = h_scr[...]
    gate = _dot(h, wg_ref[...])
    act = (gate * jax.nn.sigmoid(gate)) * _dot(h, wu_ref[...])
    acc_scr[...] += _dot(act.astype(BF16), wd_ref[...])

    @pl.when(f == pl.num_programs(1) - 1)
    def _():
        o_ref[...] = _rmsnorm(x_ref[...] + acc_scr[...], gf_ref[...])


def _ffn(x1, g, wg, wu, wd, gf, tm=512, tf=512):
    n, d = x1.shape
    dff = wg.shape[1]
    return pl.pallas_call(
        _ffn_kernel,
        grid=(n // tm, dff // tf),
        in_specs=[pl.BlockSpec((tm, d), lambda i, f: (i, 0)), _resident((1, d)),
                  pl.BlockSpec((d, tf), lambda i, f: (0, f)), pl.BlockSpec((d, tf), lambda i, f: (0, f)),
                  pl.BlockSpec((tf, d), lambda i, f: (f, 0)), _resident((1, d))],
        out_specs=pl.BlockSpec((tm, d), lambda i, f: (i, 0)),
        out_shape=jax.ShapeDtypeStruct((n, d), F32),
        scratch_shapes=[pltpu.VMEM((tm, d), BF16), pltpu.VMEM((tm, d), F32)],
        compiler_params=_params(2),
        name="ffn",
    )(x1, g, wg, wu, wd, gf)


def _pad_cols(w, width):
    return jnp.pad(w, ((0, 0), (0, width - w.shape[1])))


def _rot_cols(w):
    half = w.shape[1] // 2
    return jnp.concatenate([-w[:, half:], w[:, :half]], axis=1)


def _fused_in_weights(w_in):
    sizes = (MLA_Q_LORA, MLA_KV_LORA, MLA_ROPE, NSA_HEADS * NSA_DK) + (NSA_GROUPS * NSA_DK,) * 6 + (3 * NSA_HEADS,)
    offs = np.cumsum(sizes)[:-1].tolist()
    cq, ckv, kr, q, kc, vc, ks, vs, kw, vw, gate = jnp.split(w_in, offs, axis=1)
    d = w_in.shape[0]
    q_pad = _pad_cols((q * (NSA_DK ** -0.5 * LOG2E)).reshape(d * NSA_HEADS, NSA_DK), LANES)
    q_pad = q_pad.reshape(d, NSA_HEADS * LANES)
    per_group = lambda w, c: _pad_cols(w.reshape(d * NSA_GROUPS, c), LANES).reshape(d, NSA_GROUPS * LANES)
    kr_rot = _rot_cols(kr)
    cols = [cq, ckv, kr, kr, kr_rot, kr_rot, q_pad, kc, vc, per_group(ks, NSA_DK), per_group(kw, NSA_DK),
            per_group(gate, N_GATES)]
    w = jnp.concatenate(cols, axis=1).astype(BF16)
    assert w.shape[1] == _IN_COLS["gate"][1]
    rows_t = jnp.concatenate([vs, vw, gate], axis=1).T
    pad = -rows_t.shape[0] % 16
    return w, jnp.pad(rows_t, ((0, pad), (0, 0))).astype(BF16)


def _alibi_features(slopes2, seq):
    s1 = slopes2.astype(BF16).astype(F32)
    s2 = (slopes2 - s1).astype(BF16).astype(F32)
    s3 = (slopes2 - s1 - s2).astype(BF16).astype(F32)
    pieces = jnp.stack([s1, s2, s3, s1, s2, s3], axis=1)
    qfeat = jnp.pad(pieces, ((0, 0), (NSA_DK, LANES - NSA_DK - 6))).reshape(1, -1)
    pos = jnp.arange(seq, dtype=jnp.int32)
    hi = (POS_SPLIT * (pos // POS_SPLIT)).astype(F32)
    lo = (pos % POS_SPLIT).astype(F32)
    kfeat = jnp.pad(jnp.stack([hi, hi, hi, lo, lo, lo], axis=1), ((0, 0), (NSA_DK, LANES - NSA_DK - 6)))
    return qfeat, kfeat


def _fused_uq_weight(w_uq):
    d = w_uq.shape[0]
    w = w_uq.reshape(d, MLA_HEADS, MLA_NOPE + MLA_ROPE)
    nope = w[:, :, :MLA_NOPE].reshape(d, MLA_HEADS * MLA_NOPE)
    rope = w[:, :, MLA_NOPE:]
    rope_rot = jnp.concatenate([-rope[:, :, MLA_ROPE // 2:], rope[:, :, :MLA_ROPE // 2]], axis=2)
    flat = lambda r: r.reshape(d, MLA_HEADS * MLA_ROPE)
    return jnp.concatenate([nope, flat(rope), flat(rope_rot)], axis=1).astype(BF16)


def _rope_tables(seq):
    inv = ROPE_THETA ** (-jnp.arange(0, MLA_ROPE, 2, dtype=F32) / MLA_ROPE)
    ang = jnp.arange(seq, dtype=F32)[:, None] * inv[None, :]
    reps = 2 * LANES // MLA_ROPE
    return jnp.tile(jnp.cos(ang), (1, reps)), jnp.tile(jnp.sin(ang), (1, reps))


def _overlap_matrix(seq, n_cmp, ncp):
    cmp_start = CMP_STRIDE * np.arange(n_cmp)
    slc_start = SLC_LEN * np.arange(seq // SLC_LEN)
    ov = np.clip(np.minimum(cmp_start[:, None] + CMP_LEN, slc_start[None, :] + SLC_LEN)
                 - np.maximum(cmp_start[:, None], slc_start[None, :]), 0, None).astype(np.float32) / CMP_STRIDE
    out = np.zeros((ncp, NS_PAD), np.float32)
    out[:n_cmp, :ov.shape[1]] = ov
    return jnp.asarray(out, BF16)


def _pack_flags(flags):
    t_blocks = NSA_TILE // SLC_LEN
    steps = flags.shape[0]
    tiles = flags.reshape(steps, NS_PAD // t_blocks, t_blocks).max(axis=-1)
    words = tiles.reshape(steps, -1, FLAG_BITS) * (1 << jnp.arange(FLAG_BITS, dtype=jnp.int32))
    return words.sum(axis=-1).astype(jnp.int32).reshape(-1)


def kernel(x, attn_norm_g, w_in, mla_q_norm_g, mla_kv_norm_g, w_uq, w_uk, w_uv, cmp_pos_k, cmp_pos_v, w_cmp_k1,
           w_cmp_k2, w_cmp_v1, w_cmp_v2, w_o, ffn_norm_g, w_gate, w_up, w_down, final_norm_g):
    batch, seq, d = x.shape
    n = batch * seq
    assert w_in.shape[0] == 1, "the final RMSNorm is fused into the FFN kernel of a single layer"
    assert seq % 512 == 0 and seq // SLC_LEN <= NS_PAD and seq // POS_SPLIT <= 256
    n_cmp = (seq - CMP_LEN) // CMP_STRIDE + 1
    ncp = seq // CMP_STRIDE
    cos_t, sin_t = _rope_tables(seq)
    ov = _overlap_matrix(seq, n_cmp, ncp)
    slopes2 = 2.0 ** (-8.0 * jnp.arange(1, NSA_HEADS + 1, dtype=F32) / NSA_HEADS) * LOG2E
    qfeat, kfeat = _alibi_features(slopes2, seq)
    x2 = x.reshape(n, d)

    w_fused, w_fused_t = _fused_in_weights(w_in[0])
    cq, ckv, kr, q, kvc, ksw, gates, vt, gt = _in_proj(
        x2, attn_norm_g[0][None], w_fused, w_fused_t, qfeat, cos_t, sin_t, kfeat, seq)

    qn, qr, kn, vt_mla = _mla_up(cq, ckv, mla_q_norm_g[0][None], mla_kv_norm_g[0][None], _fused_uq_weight(w_uq[0]),
                                 w_uk[0].astype(BF16), w_uv[0].T.astype(BF16), cos_t, sin_t, seq)
    o_mla = _mla_attn(qn, qr, kn, kr, vt_mla, batch, seq)

    r = kvc.reshape(batch, seq, 2, NSA_GROUPS, NSA_DK).transpose(2, 0, 3, 1, 4)
    r = r.reshape(2, batch, NSA_GROUPS, ncp, CMP_STRIDE * NSA_DK)
    pos = jnp.stack([cmp_pos_k[0].reshape(1, -1), cmp_pos_v[0].reshape(1, -1)])
    w1 = jnp.stack([w_cmp_k1[0], w_cmp_v1[0]]).astype(BF16)
    w2 = jnp.stack([_pad_cols(w_cmp_k2[0], LANES), _pad_cols(w_cmp_v2[0], LANES)]).astype(BF16)
    kvc_cmp = _compress(r, pos, w1, w2)

    oc, selt, flags = _nsa_cmp(slopes2, q, kvc_cmp, ov, gates, batch, seq, n_cmp)
    o_nsa = _nsa_attn(_pack_flags(flags), q, selt, oc, gt, ksw, vt, batch, seq)

    split = MLA_HEADS * MLA_V
    x1 = _out_proj(x2, o_mla, o_nsa, w_o[0][:split].astype(BF16), w_o[0][split:].astype(BF16))
    out = _ffn(x1, ffn_norm_g[0][None], w_gate[0].astype(BF16), w_up[0].astype(BF16), w_down[0].astype(BF16),
               final_norm_g[None])
    return out.reshape(batch, seq, d)
```

```python
import functools
import math

import numpy as np
import jax
import jax.numpy as jnp
from jax import lax
from jax.experimental import pallas as pl
from jax.experimental.pallas import tpu as pltpu

F32 = jnp.float32
BF16 = jnp.bfloat16

EPS = 1e-6
NEG = -1e30
LOG2E = math.log2(math.e)
LANES = 128

MLA_HEADS = 8
MLA_Q_LORA = 512
MLA_KV_LORA = 256
MLA_NOPE = 128
MLA_ROPE = 64
MLA_V = 128
ROPE_THETA = 10000.0
MLA_TK = 256

NSA_HEADS = 16
NSA_GROUPS = 2
NSA_HPG = NSA_HEADS // NSA_GROUPS
NSA_DK = 64
CMP_LEN = 32
CMP_STRIDE = 16
CMP_HIDDEN = 128
SLC_LEN = 64
SLC_TOPK = 16
WINDOW = 512
FORCE_SCORE = 1e4
NSA_TILE = 128
NS_PAD = 128
N_GATES = 3 * NSA_HPG
POS_SPLIT = 64

VMEM_LIMIT = 56 * 1024 * 1024


def _params(n_axes):
    return pltpu.CompilerParams(dimension_semantics=("arbitrary",) * n_axes, vmem_limit_bytes=VMEM_LIMIT)


def _resident(shape):
    zeros = (0,) * len(shape)
    return pl.BlockSpec(shape, lambda *_: zeros, pipeline_mode=pl.Buffered(1))


def _rmsnorm(x, g):
    return x * lax.rsqrt(jnp.mean(x * x, axis=-1, keepdims=True) + EPS) * g


def _dot(a, b):
    return jnp.dot(a, b, preferred_element_type=F32)


def _dot_nt(a, b):
    return lax.dot_general(a, b, (((1,), (1,)), ((), ())), preferred_element_type=F32)


def _flash_weights(s, m_ref, l_ref, idx):
    m_prev = m_ref[idx]
    m_new = jnp.maximum(m_prev, jnp.max(s, axis=0, keepdims=True))
    alpha = jnp.exp2(m_prev - m_new)
    p = jnp.exp2(s - m_new)
    l_ref[idx] = alpha * l_ref[idx] + jnp.sum(p, axis=0, keepdims=True)
    m_ref[idx] = m_new
    return alpha, p.astype(BF16)


def _flash_update(s, v_t, m_ref, l_ref, acc_ref, idx):
    alpha, p = _flash_weights(s, m_ref, l_ref, idx)
    acc_ref[idx] = alpha * acc_ref[idx] + _dot(v_t, p)


_IN_COLS = dict(cq=(0, 512), ckv=(512, 768), kra=(768, 896), krb=(896, 1024), q=(1024, 3072),
                kvc=(3072, 3328), ksw=(3328, 3840), gate=(3840, 4096))
_VT_ROWS = 2 * NSA_GROUPS * NSA_DK


def _inproj_kernel(x_ref, g_ref, w_ref, wt_ref, qfeat_ref, cos_ref, sin_ref, kfeat_ref, cq_ref, ckv_ref, kr_ref,
                   q_ref, kvc_ref, ksw_ref, gate_ref, vt_ref, gt_ref):
    h = _rmsnorm(x_ref[...], g_ref[...]).astype(BF16)

    def mm(name):
        lo, hi = _IN_COLS[name]
        return _dot(h, w_ref[:, lo:hi])

    cq_ref[...] = mm("cq")
    ckv_ref[...] = mm("ckv")
    kr_ref[...] = (mm("kra") * cos_ref[...] + mm("krb") * sin_ref[...]).astype(BF16)
    q_ref[...] = (mm("q") + qfeat_ref[...]).astype(BF16)
    kvc_ref[...] = mm("kvc")
    kfeat = kfeat_ref[...]
    ksw_ref[...] = (mm("ksw") + jnp.concatenate([kfeat] * (2 * NSA_GROUPS), axis=1)).astype(BF16)
    gate_ref[...] = mm("gate")
    t = _dot_nt(wt_ref[...], h)
    for c in range(vt_ref.shape[0]):
        cs = slice(c * NSA_TILE, (c + 1) * NSA_TILE)
        vt_ref[c] = t[:_VT_ROWS, cs].astype(BF16)
        gt_ref[c] = t[_VT_ROWS:_VT_ROWS + NSA_GROUPS * N_GATES, cs]


def _in_proj(x2, g, w, wt, qfeat, cos_t, sin_t, kfeat, seq, tm=256):
    n, d = x2.shape
    nt = seq // tm
    widths = {k: hi - lo for k, (lo, hi) in _IN_COLS.items()}
    row = lambda c: pl.BlockSpec((tm, c), lambda i: (i, 0))
    tab = pl.BlockSpec((tm, LANES), lambda i: (i % nt, 0))
    outs = [("cq", F32), ("ckv", F32), ("kra", BF16), ("q", BF16), ("kvc", F32), ("ksw", BF16), ("gate", F32)]
    tiles = tm // NSA_TILE
    tspec = lambda rows: pl.BlockSpec((tiles, rows, NSA_TILE), lambda i: (i, 0, 0))
    return pl.pallas_call(
        _inproj_kernel,
        grid=(n // tm,),
        in_specs=[row(d), _resident((1, d)), _resident(w.shape), _resident(wt.shape), _resident(qfeat.shape),
                  tab, tab, tab],
        out_specs=[row(widths[k]) for k, _ in outs] + [tspec(_VT_ROWS), tspec(NSA_GROUPS * N_GATES)],
        out_shape=[jax.ShapeDtypeStruct((n, widths[k]), dt) for k, dt in outs] + [
            jax.ShapeDtypeStruct((n // NSA_TILE, _VT_ROWS, NSA_TILE), BF16),
            jax.ShapeDtypeStruct((n // NSA_TILE, NSA_GROUPS * N_GATES, NSA_TILE), F32)],
        compiler_params=_params(1),
        name="in_proj",
    )(x2, g, w, wt, qfeat, cos_t, sin_t, kfeat)


def _mlaup_kernel(cq_ref, ckv_ref, gq_ref, gkv_ref, wq_ref, wk_ref, wvt_ref, cos_ref, sin_ref, qn_ref, qr_ref,
                  kn_ref, vt_ref, *, scale):
    cqn = _rmsnorm(cq_ref[...], gq_ref[...]).astype(BF16)
    ckvn = _rmsnorm(ckv_ref[...], gkv_ref[...]).astype(BF16)
    hn = MLA_HEADS * MLA_NOPE
    hr = MLA_HEADS * MLA_ROPE
    qn_ref[...] = (_dot(cqn, wq_ref[:, :hn]) * scale).astype(BF16)
    a = _dot(cqn, wq_ref[:, hn:hn + hr])
    b = _dot(cqn, wq_ref[:, hn + hr:])
    cos = cos_ref[...]
    sin = sin_ref[...]
    for p in range(hr // LANES):
        sl = slice(p * LANES, (p + 1) * LANES)
        qr_ref[:, sl] = ((a[:, sl] * cos + b[:, sl] * sin) * scale).astype(BF16)
    kn_ref[...] = _dot(ckvn, wk_ref[...]).astype(BF16)
    v_t = _dot_nt(wvt_ref[...], ckvn)
    for c in range(vt_ref.shape[0]):
        vt_ref[c] = v_t[:, c * MLA_TK:(c + 1) * MLA_TK].astype(BF16)


def _mla_up(cq, ckv, gq, gkv, wq, wk, wvt, cos_t, sin_t, seq, tm=512):
    n = cq.shape[0]
    nt = seq // tm
    hn = MLA_HEADS * MLA_NOPE
    hr = MLA_HEADS * MLA_ROPE
    hv = MLA_HEADS * MLA_V
    row = lambda c: pl.BlockSpec((tm, c), lambda i: (i, 0))
    tab = pl.BlockSpec((tm, LANES), lambda i: (i % nt, 0))
    scale = (MLA_NOPE + MLA_ROPE) ** -0.5 * LOG2E
    return pl.pallas_call(
        functools.partial(_mlaup_kernel, scale=scale),
        grid=(n // tm,),
        in_specs=[row(MLA_Q_LORA), row(MLA_KV_LORA), _resident(gq.shape), _resident(gkv.shape),
                  _resident(wq.shape), _resident(wk.shape), _resident(wvt.shape), tab, tab],
        out_specs=[row(hn), row(hr), row(hn), pl.BlockSpec((tm // MLA_TK, hv, MLA_TK), lambda i: (i, 0, 0))],
        out_shape=[jax.ShapeDtypeStruct((n, hn), BF16), jax.ShapeDtypeStruct((n, hr), BF16),
                   jax.ShapeDtypeStruct((n, hn), BF16), jax.ShapeDtypeStruct((n // MLA_TK, hv, MLA_TK), BF16)],
        compiler_params=_params(1),
        name="mla_up",
    )(cq, ckv, gq, gkv, wq, wk, wvt, cos_t, sin_t)


def _mla_attn_kernel(qn_ref, qr_ref, kn_ref, kr_ref, vt_ref, o_ref, m_ref, l_ref, acc_ref, s_scr, *, tq):
    i = pl.program_id(2)
    tk = MLA_TK
    lane = lax.broadcasted_iota(jnp.int32, (tq, LANES), 1)
    krow = lax.broadcasted_iota(jnp.int32, (tk, tq), 0)
    qcol = lax.broadcasted_iota(jnp.int32, (tk, tq), 1)
    qr = qr_ref[...]
    zero = jnp.zeros_like(qr)
    q_cat = [jnp.concatenate([qn_ref[:, hh * LANES:(hh + 1) * LANES],
                              jnp.where((lane < MLA_ROPE) == (hh == 0), qr, zero)], axis=1) for hh in range(2)]
    m_ref[...] = jnp.full_like(m_ref, NEG)
    l_ref[...] = jnp.zeros_like(l_ref)
    acc_ref[...] = jnp.zeros_like(acc_ref)

    def scores(j, slot):
        ks = pl.ds(pl.multiple_of(j * tk, tk), tk)
        k_rope = kr_ref[ks, :]
        for hh in range(2):
            hs = slice(hh * LANES, (hh + 1) * LANES)
            s_scr[slot, hh] = _dot_nt(jnp.concatenate([kn_ref[ks, hs], k_rope], axis=1), q_cat[hh])

    def consume(j, slot, diagonal):
        weights = []
        for hh in range(2):
            s = s_scr[slot, hh]
            if diagonal:
                s = jnp.where(j * tk + krow <= i * tq + qcol, s, NEG)
            weights.append(_flash_weights(s, m_ref, l_ref, hh))
        for hh in range(2):
            alpha, p = weights[hh]
            acc_ref[hh] = alpha * acc_ref[hh] + _dot(vt_ref[0, j, hh * LANES:(hh + 1) * LANES, :], p)

    def body(u, carry):
        j = 2 * u
        scores(j + 1, 1)
        consume(j, 0, False)
        scores(j + 2, 0)
        consume(j + 1, 1, False)
        return carry

    assert tq == 2 * tk
    scores(0, 0)
    lax.fori_loop(0, i, body, 0)
    scores(2 * i + 1, 1)
    consume(2 * i, 0, True)
    consume(2 * i + 1, 1, True)
    for hh in range(2):
        o_t = acc_ref[hh] / l_ref[hh]
        for c in range(tq // LANES):
            o_ref[c * LANES:(c + 1) * LANES, hh * LANES:(hh + 1) * LANES] = (
                o_t[:, c * LANES:(c + 1) * LANES].T.astype(BF16))


def _mla_attn(qn, qr, kn, kr, vt, batch, seq, tq=512):
    n = qn.shape[0]
    nq = seq // tq
    nk = seq // MLA_TK
    pairs = MLA_HEADS // 2
    pw = 2 * LANES
    qspec = lambda c: pl.BlockSpec((tq, c), lambda b, p, i: (b * nq + i, p))
    vt4 = vt.reshape(batch, nk, MLA_HEADS * MLA_V, MLA_TK)
    return pl.pallas_call(
        functools.partial(_mla_attn_kernel, tq=tq),
        grid=(batch, pairs, nq),
        in_specs=[qspec(pw), qspec(LANES), pl.BlockSpec((seq, pw), lambda b, p, i: (b, p)),
                  pl.BlockSpec((seq, LANES), lambda b, p, i: (b, 0)),
                  pl.BlockSpec((1, nk, pw, MLA_TK), lambda b, p, i: (b, 0, p, 0))],
        out_specs=qspec(pw),
        out_shape=jax.ShapeDtypeStruct((n, MLA_HEADS * MLA_V), BF16),
        scratch_shapes=[pltpu.VMEM((2, 1, tq), F32), pltpu.VMEM((2, 1, tq), F32), pltpu.VMEM((2, MLA_V, tq), F32),
                        pltpu.VMEM((2, 2, MLA_TK, tq), F32)],
        compiler_params=_params(3),
        name="mla_attn",
    )(qn, qr, kn, kr, vt4)


def _compress_kernel(r_ref, pos_ref, w1_ref, w2_ref, o_ref):
    half = CMP_STRIDE * NSA_DK
    r = r_ref[0, 0, 0]
    pos = pos_ref[0]
    a = _dot((r + pos[:, :half]).astype(BF16), w1_ref[0, :half, :])
    b = _dot((r + pos[:, half:]).astype(BF16), w1_ref[0, half:, :])
    hid = a + pltpu.roll(b, b.shape[0] - 1, 0)
    act = hid * jax.nn.sigmoid(hid)
    o_ref[0, 0, 0] = _dot(act.astype(BF16), w2_ref[0]).astype(BF16)


def _compress(r, pos, w1, w2):
    _, batch, groups, nc, width = r.shape
    return pl.pallas_call(
        _compress_kernel,
        grid=(2, batch, groups),
        in_specs=[pl.BlockSpec((1, 1, 1, nc, width), lambda t, b, g: (t, b, g, 0, 0)),
                  pl.BlockSpec((1, 1, 2 * width), lambda t, b, g: (t, 0, 0)),
                  pl.BlockSpec((1, 2 * width, CMP_HIDDEN), lambda t, b, g: (t, 0, 0)),
                  pl.BlockSpec((1, CMP_HIDDEN, LANES), lambda t, b, g: (t, 0, 0))],
        out_specs=pl.BlockSpec((1, 1, 1, nc, LANES), lambda t, b, g: (t, b, g, 0, 0)),
        out_shape=jax.ShapeDtypeStruct((2, batch, groups, nc, LANES), BF16),
        compiler_params=_params(3),
        name="compress",
    )(r, pos, w1, w2)


def _split3(x):
    hi = x.astype(BF16)
    r = x - hi.astype(F32)
    mid = r.astype(BF16)
    lo = (r - mid.astype(F32)).astype(BF16)
    return hi, mid, lo


def _nsa_cmp_kernel(slopes_ref, q_ref, kc_ref, vc_ref, ov_ref, gate_ref, oc_ref, selt_ref, flag_ref, *, n_cmp):
    g = pl.program_id(1)
    i = pl.program_id(2)
    t = NSA_TILE
    kc = kc_ref[0, 0, 0]
    vc = vc_ref[0, 0, 0]
    ncp = kc.shape[0]
    row = lax.broadcasted_iota(jnp.int32, (t, ncp), 0)
    c = lax.broadcasted_iota(jnp.int32, (t, ncp), 1)
    dist = i * t + row - (CMP_STRIDE * c + (CMP_LEN - 1))
    validf = jnp.where(dist >= 0, jnp.where(c < n_cmp, 1.0, 0.0), 0.0)
    valid = validf > 0.5
    distf = dist.astype(F32)
    gates = jax.nn.sigmoid(gate_ref[...])
    psum = jnp.zeros((t, ncp), F32)
    for h in range(NSA_HPG):
        s = _dot_nt(q_ref[:, h * LANES:(h + 1) * LANES], kc)
        s = jnp.where(valid, s - slopes_ref[g * NSA_HPG + h] * distf, NEG)
        e = jnp.exp2(s - jnp.max(s, axis=-1, keepdims=True))
        p = e * (validf / jnp.sum(e, axis=-1, keepdims=True))
        o = gates[:, 3 * h:3 * h + 1] * _dot(p.astype(BF16), vc)
        if h % 2 == 0:
            o_even = o
        else:
            oc_ref[:, (h // 2) * LANES:(h // 2 + 1) * LANES] = (o_even + pltpu.roll(o, NSA_DK, 1)).astype(BF16)
        psum = psum + p

    ov = ov_ref[...]
    hi, mid, lo = _split3(psum)
    imp = _dot(hi, ov) + _dot(mid, ov) + _dot(lo, ov)

    qrow = lax.broadcasted_iota(jnp.int32, (t, NS_PAD), 0)
    blk = lax.broadcasted_iota(jnp.int32, (t, NS_PAD), 1)
    blk_t = (i * t + qrow) // SLC_LEN
    forced = jnp.where(blk == 0, 1.0, 0.0) + jnp.where(blk == blk_t, 1.0, 0.0) + jnp.where(blk == blk_t - 1, 1.0, 0.0)
    imp = jnp.where(forced > 0.5, FORCE_SCORE, jnp.where(blk <= blk_t, imp, -1.0))
    blkf = blk.astype(F32)
    sel = jnp.zeros((t, NS_PAD), F32)
    for _ in range(SLC_TOPK):
        best = jnp.max(imp, axis=-1, keepdims=True)
        pick = jnp.min(jnp.where(imp == best, blkf, float(NS_PAD)), axis=-1, keepdims=True)
        hit = blkf == pick
        sel = jnp.where(hit, 1.0, sel)
        imp = jnp.where(hit, -3e38, imp)
    selt_ref[0, 0] = sel.T
    flag_ref[0] = jnp.max(sel, axis=0, keepdims=True).astype(jnp.int32)


def _nsa_cmp(slopes, q, kvc, ov, gates, batch, seq, n_cmp):
    t = NSA_TILE
    nq = seq // t
    n = q.shape[0]
    ncp = kvc.shape[3]
    rows = lambda c: pl.BlockSpec((t, c), lambda b, g, i, *_: (b * nq + i, g))
    kv = lambda which: pl.BlockSpec((1, 1, 1, ncp, LANES), lambda b, g, i, *_: (which, b, g, 0, 0))
    grid_spec = pltpu.PrefetchScalarGridSpec(
        num_scalar_prefetch=1,
        grid=(batch, NSA_GROUPS, nq),
        in_specs=[rows(NSA_HPG * LANES), kv(0), kv(1), pl.BlockSpec((ncp, NS_PAD), lambda b, g, i, *_: (0, 0)),
                  rows(LANES)],
        out_specs=[rows(NSA_HPG * NSA_DK),
                   pl.BlockSpec((1, 1, NS_PAD, t), lambda b, g, i, *_: (b, g, 0, i)),
                   pl.BlockSpec((1, 1, NS_PAD), lambda b, g, i, *_: ((b * NSA_GROUPS + g) * nq + i, 0, 0))],
    )
    return pl.pallas_call(
        functools.partial(_nsa_cmp_kernel, n_cmp=n_cmp),
        grid_spec=grid_spec,
        out_shape=[jax.ShapeDtypeStruct((n, NSA_HEADS * NSA_DK), BF16),
                   jax.ShapeDtypeStruct((batch, NSA_GROUPS, NS_PAD, seq), F32),
                   jax.ShapeDtypeStruct((batch * NSA_GROUPS * nq, 1, NS_PAD), jnp.int32)],
        compiler_params=_params(3),
        name="nsa_cmp",
    )(slopes, q, kvc, kvc, ov, gates)


IDS_PER_WORD = 4


def _nsa_attn_kernel(counts_ref, words_ref, q_ref, selt_ref, oc_ref, gt_ref, ks_ref, kw_ref, vst_ref, vwt_ref, o_ref,
                     q_scr, m_ref, l_ref, acc_ref, s_scr, *, nq, words_per_step):
    b = pl.program_id(0)
    g = pl.program_id(1)
    i = pl.program_id(2)
    t = NSA_TILE
    hpg = NSA_HPG
    for h in range(hpg):
        q_scr[h * t:(h + 1) * t, :] = q_ref[:, h * LANES:(h + 1) * LANES]
    m_ref[...] = jnp.full_like(m_ref, NEG)
    l_ref[...] = jnp.zeros_like(l_ref)
    acc_ref[...] = jnp.zeros_like(acc_ref)

    krow = lax.broadcasted_iota(jnp.int32, (t, t), 0)
    qcol = lax.broadcasted_iota(jnp.int32, (t, t), 1)
    blocks_per_tile = t // SLC_LEN
    slc, win = 0, 1
    step = (b * NSA_GROUPS + g) * nq + i

    def run_branch(branch, count, tile_at, keepf_fn, k_ref, vt_ref):
        def scores(n, slot):
            j = tile_at(jnp.minimum(n, count - 1))
            s_scr[slot] = _dot_nt(k_ref[pl.ds(pl.multiple_of(j * t, t), t), :], q_scr[...])

        def consume(n, slot):
            j = tile_at(n)
            bias = jnp.where(keepf_fn(j) > 0.5, 0.0, NEG)
            s = s_scr[slot] + jnp.concatenate([bias] * hpg, axis=1)
            _flash_update(s, vt_ref[0, j], m_ref, l_ref, acc_ref, branch)

        def body(u, carry):
            n = 2 * u
            scores(n + 1, 1)
            consume(n, 0)
            scores(n + 2, 0)
            consume(n + 1, 1)
            return carry

        scores(0, 0)
        lax.fori_loop(0, count // 2, body, 0)

        @pl.when(count % 2 == 1)
        def _():
            consume(count - 1, 0)

    def slc_tile(n):
        word = words_ref[step * words_per_step + n // IDS_PER_WORD]
        return lax.shift_right_logical(word, 8 * (n % IDS_PER_WORD)) & 255

    def slc_keepf(j):
        dist = (i - j) * t + qcol - krow
        sel_rows = [selt_ref[0, 0, pl.ds(blocks_per_tile * j + r, 1), :] for r in range(blocks_per_tile)]
        picked = sel_rows[-1]
        for r in range(blocks_per_tile - 2, -1, -1):
            picked = jnp.where(krow < (r + 1) * SLC_LEN, sel_rows[r], picked)
        return jnp.where(dist >= 0, picked, 0.0)

    def win_keepf(j):
        dist = (i - j) * t + qcol - krow
        return jnp.where(dist >= 0, jnp.where(dist < WINDOW, 1.0, 0.0), 0.0)

    run_branch(slc, counts_ref[step], slc_tile, slc_keepf, ks_ref, vst_ref)
    first = jnp.maximum(i - WINDOW // t, 0)
    run_branch(win, i + 1 - first, lambda n: first + n, win_keepf, kw_ref, vwt_ref)

    gates = jax.nn.sigmoid(gt_ref[0])
    inv_s = 1.0 / l_ref[slc]
    inv_w = 1.0 / l_ref[win]
    for pair in range(hpg // 2):
        halves = []
        for h in (2 * pair, 2 * pair + 1):
            hs = slice(h * t, (h + 1) * t)
            halves.append(gates[3 * h + 1:3 * h + 2, :] * (acc_ref[slc, :, hs] * inv_s[:, hs])
                          + gates[3 * h + 2:3 * h + 3, :] * (acc_ref[win, :, hs] * inv_w[:, hs]))
        ps = slice(pair * LANES, (pair + 1) * LANES)
        o_ref[:, ps] = (oc_ref[:, ps].astype(F32) + jnp.concatenate(halves, axis=0).T).astype(BF16)


def _nsa_attn(counts, words, q, selt, oc, gt, ksw, vt, batch, seq):
    t = NSA_TILE
    nq = seq // t
    n = q.shape[0]
    words_per_step = words.shape[0] // (batch * NSA_GROUPS * nq)
    rows = lambda c: pl.BlockSpec((t, c), lambda b, g, i, *_: (b * nq + i, g))
    key = lambda which: pl.BlockSpec((seq, LANES), lambda b, g, i, *_: (b, which * NSA_GROUPS + g))
    vt4 = vt.reshape(batch, nq, _VT_ROWS, t)
    val = lambda which: pl.BlockSpec((1, nq, NSA_DK, t), lambda b, g, i, *_: (b, 0, which * NSA_GROUPS + g, 0))
    grid_spec = pltpu.PrefetchScalarGridSpec(
        num_scalar_prefetch=2,
        grid=(batch, NSA_GROUPS, nq),
        in_specs=[rows(NSA_HPG * LANES),
                  pl.BlockSpec((1, 1, NS_PAD, t), lambda b, g, i, *_: (b, g, 0, i)),
                  rows(NSA_HPG * NSA_DK),
                  pl.BlockSpec((1, N_GATES, t), lambda b, g, i, *_: (b * nq + i, g, 0)),
                  key(0), key(1), val(0), val(1)],
        out_specs=rows(NSA_HPG * NSA_DK),
        scratch_shapes=[pltpu.VMEM((NSA_HPG * t, LANES), BF16), pltpu.VMEM((2, 1, NSA_HPG * t), F32),
                        pltpu.VMEM((2, 1, NSA_HPG * t), F32), pltpu.VMEM((2, NSA_DK, NSA_HPG * t), F32),
                        pltpu.VMEM((2, t, NSA_HPG * t), F32)],
    )
    return pl.pallas_call(
        functools.partial(_nsa_attn_kernel, nq=nq, words_per_step=words_per_step),
        grid_spec=grid_spec,
        out_shape=jax.ShapeDtypeStruct((n, NSA_HEADS * NSA_DK), BF16),
        compiler_params=_params(3),
        name="nsa_attn",
    )(counts, words, q, selt, oc, gt, ksw, ksw, vt4, vt4)


def _outproj_kernel(x_ref, om_ref, on_ref, wm_ref, wn_ref, o_ref):
    o_ref[...] = x_ref[...] + _dot(om_ref[...], wm_ref[...]) + _dot(on_ref[...], wn_ref[...])


def _out_proj(x2, o_mla, o_nsa, wm, wn, tm=512):
    n, d = x2.shape
    row = lambda c: pl.BlockSpec((tm, c), lambda i: (i, 0))
    return pl.pallas_call(
        _outproj_kernel,
        grid=(n // tm,),
        in_specs=[row(d), row(o_mla.shape[1]), row(o_nsa.shape[1]), _resident(wm.shape), _resident(wn.shape)],
        out_specs=row(d),
        out_shape=jax.ShapeDtypeStruct((n, d), F32),
        compiler_params=_params(1),
        name="out_proj",
    )(x2, o_mla, o_nsa, wm, wn)


def _ffn_kernel(x_ref, g_ref, wg_ref, wu_ref, wd_ref, gf_ref, o_ref, h_scr, acc_scr):
    f = pl.program_id(1)

    @pl.when(f == 0)
    def _():
        h_scr[...] = _rmsnorm(x_ref[...], g_ref[...]).astype(BF16)
        acc_scr[...] = jnp.zeros_like(acc_scr)

    h = h_scr[...]
    gate = _dot(h, wg_ref[...])
    act = (gate * jax.nn.sigmoid(gate)) * _dot(h, wu_ref[...])
    acc_scr[...] += _dot(act.astype(BF16), wd_ref[...])

    @pl.when(f == pl.num_programs(1) - 1)
    def _():
        o_ref[...] = _rmsnorm(x_ref[...] + acc_scr[...], gf_ref[...])


def _ffn(x1, g, wg, wu, wd, gf, tm=512, tf=512):
    n, d = x1.shape
    dff = wg.shape[1]
    return pl.pallas_call(
        _ffn_kernel,
        grid=(n // tm, dff // tf),
        in_specs=[pl.BlockSpec((tm, d), lambda i, f: (i, 0)), _resident((1, d)),
                  pl.BlockSpec((d, tf), lambda i, f: (0, f)), pl.BlockSpec((d, tf), lambda i, f: (0, f)),
                  pl.BlockSpec((tf, d), lambda i, f: (f, 0)), _resident((1, d))],
        out_specs=pl.BlockSpec((tm, d), lambda i, f: (i, 0)),
        out_shape=jax.ShapeDtypeStruct((n, d), F32),
        scratch_shapes=[pltpu.VMEM((tm, d), BF16), pltpu.VMEM((tm, d), F32)],
        compiler_params=_params(2),
        name="ffn",
    )(x1, g, wg, wu, wd, gf)


def _pad_cols(w, width):
    return jnp.pad(w, ((0, 0), (0, width - w.shape[1])))


def _rot_cols(w):
    half = w.shape[1] // 2
    return jnp.concatenate([-w[:, half:], w[:, :half]], axis=1)


def _fused_in_weights(w_in):
    sizes = (MLA_Q_LORA, MLA_KV_LORA, MLA_ROPE, NSA_HEADS * NSA_DK) + (NSA_GROUPS * NSA_DK,) * 6 + (3 * NSA_HEADS,)
    offs = np.cumsum(sizes)[:-1].tolist()
    cq, ckv, kr, q, kc, vc, ks, vs, kw, vw, gate = jnp.split(w_in, offs, axis=1)
    d = w_in.shape[0]
    q_pad = _pad_cols((q * (NSA_DK ** -0.5 * LOG2E)).reshape(d * NSA_HEADS, NSA_DK), LANES)
    q_pad = q_pad.reshape(d, NSA_HEADS * LANES)
    per_group = lambda w, c: _pad_cols(w.reshape(d * NSA_GROUPS, c), LANES).reshape(d, NSA_GROUPS * LANES)
    kr_rot = _rot_cols(kr)
    cols = [cq, ckv, kr, kr, kr_rot, kr_rot, q_pad, kc, vc, per_group(ks, NSA_DK), per_group(kw, NSA_DK),
            per_group(gate, N_GATES)]
    w = jnp.concatenate(cols, axis=1).astype(BF16)
    assert w.shape[1] == _IN_COLS["gate"][1]
    rows_t = jnp.concatenate([vs, vw, gate], axis=1).T
    pad = -rows_t.shape[0] % 16
    return w, jnp.pad(rows_t, ((0, pad), (0, 0))).astype(BF16)


def _alibi_features(slopes2, seq):
    s1 = slopes2.astype(BF16).astype(F32)
    s2 = (slopes2 - s1).astype(BF16).astype(F32)
    s3 = (slopes2 - s1 - s2).astype(BF16).astype(F32)
    pieces = jnp.stack([s1, s2, s3, s1, s2, s3], axis=1)
    qfeat = jnp.pad(pieces, ((0, 0), (NSA_DK, LANES - NSA_DK - 6))).reshape(1, -1)
    pos = jnp.arange(seq, dtype=jnp.int32)
    hi = (POS_SPLIT * (pos // POS_SPLIT)).astype(F32)
    lo = (pos % POS_SPLIT).astype(F32)
    kfeat = jnp.pad(jnp.stack([hi, hi, hi, lo, lo, lo], axis=1), ((0, 0), (NSA_DK, LANES - NSA_DK - 6)))
    return qfeat, kfeat


def _fused_uq_weight(w_uq):
    d = w_uq.shape[0]
    w = w_uq.reshape(d, MLA_HEADS, MLA_NOPE + MLA_ROPE)
    nope = w[:, :, :MLA_NOPE].reshape(d, MLA_HEADS * MLA_NOPE)
    rope = w[:, :, MLA_NOPE:]
    rope_rot = jnp.concatenate([-rope[:, :, MLA_ROPE // 2:], rope[:, :, :MLA_ROPE // 2]], axis=2)
    flat = lambda r: r.reshape(d, MLA_HEADS * MLA_ROPE)
    return jnp.concatenate([nope, flat(rope), flat(rope_rot)], axis=1).astype(BF16)


def _rope_tables(seq):
    inv = ROPE_THETA ** (-jnp.arange(0, MLA_ROPE, 2, dtype=F32) / MLA_ROPE)
    ang = jnp.arange(seq, dtype=F32)[:, None] * inv[None, :]
    reps = 2 * LANES // MLA_ROPE
    return jnp.tile(jnp.cos(ang), (1, reps)), jnp.tile(jnp.sin(ang), (1, reps))


def _overlap_matrix(seq, n_cmp, ncp):
    cmp_start = CMP_STRIDE * np.arange(n_cmp)
    slc_start = SLC_LEN * np.arange(seq // SLC_LEN)
    ov = np.clip(np.minimum(cmp_start[:, None] + CMP_LEN, slc_start[None, :] + SLC_LEN)
                 - np.maximum(cmp_start[:, None], slc_start[None, :]), 0, None).astype(np.float32) / CMP_STRIDE
    out = np.zeros((ncp, NS_PAD), np.float32)
    out[:n_cmp, :ov.shape[1]] = ov
    return jnp.asarray(out, BF16)


def _active_tiles(flags, batch, seq):
    t_blocks = NSA_TILE // SLC_LEN
    nq = seq // NSA_TILE
    steps = flags.shape[0]
    tiles = flags.reshape(steps, NS_PAD // t_blocks, t_blocks).max(axis=-1)
    tile_id = jnp.arange(tiles.shape[1], dtype=jnp.int32)[None, :]
    q_tile = (jnp.arange(steps, dtype=jnp.int32) % nq)[:, None]
    active = (tiles > 0) & (tile_id <= q_tile)
    order = jnp.argsort(jnp.where(active, tile_id, tile_id + tiles.shape[1]), axis=-1).astype(jnp.int32)
    shifts = 8 * jnp.arange(IDS_PER_WORD, dtype=jnp.int32)
    words = (order.reshape(steps, -1, IDS_PER_WORD) << shifts).sum(axis=-1).astype(jnp.int32)
    return active.sum(axis=-1).astype(jnp.int32), words.reshape(-1)


def kernel(x, attn_norm_g, w_in, mla_q_norm_g, mla_kv_norm_g, w_uq, w_uk, w_uv, cmp_pos_k, cmp_pos_v, w_cmp_k1,
           w_cmp_k2, w_cmp_v1, w_cmp_v2, w_o, ffn_norm_g, w_gate, w_up, w_down, final_norm_g):
    batch, seq, d = x.shape
    n = batch * seq
    assert w_in.shape[0] == 1, "the final RMSNorm is fused into the FFN kernel of a single layer"
    assert seq % 512 == 0 and seq // SLC_LEN <= NS_PAD and seq // POS_SPLIT <= 256
    n_cmp = (seq - CMP_LEN) // CMP_STRIDE + 1
    ncp = seq // CMP_STRIDE
    cos_t, sin_t = _rope_tables(seq)
    ov = _overlap_matrix(seq, n_cmp, ncp)
    slopes2 = 2.0 ** (-8.0 * jnp.arange(1, NSA_HEADS + 1, dtype=F32) / NSA_HEADS) * LOG2E
    qfeat, kfeat = _alibi_features(slopes2, seq)
    x2 = x.reshape(n, d)

    w_fused, w_fused_t = _fused_in_weights(w_in[0])
    cq, ckv, kr, q, kvc, ksw, gates, vt, gt = _in_proj(
        x2, attn_norm_g[0][None], w_fused, w_fused_t, qfeat, cos_t, sin_t, kfeat, seq)

    qn, qr, kn, vt_mla = _mla_up(cq, ckv, mla_q_norm_g[0][None], mla_kv_norm_g[0][None], _fused_uq_weight(w_uq[0]),
                                 w_uk[0].astype(BF16), w_uv[0].T.astype(BF16), cos_t, sin_t, seq)
    o_mla = _mla_attn(qn, qr, kn, kr, vt_mla, batch, seq)

    r = kvc.reshape(batch, seq, 2, NSA_GROUPS, NSA_DK).transpose(2, 0, 3, 1, 4)
    r = r.reshape(2, batch, NSA_GROUPS, ncp, CMP_STRIDE * NSA_DK)
    pos = jnp.stack([cmp_pos_k[0].reshape(1, -1), cmp_pos_v[0].reshape(1, -1)])
    w1 = jnp.stack([w_cmp_k1[0], w_cmp_v1[0]]).astype(BF16)
    w2 = jnp.stack([_pad_cols(w_cmp_k2[0], LANES), _pad_cols(w_cmp_v2[0], LANES)]).astype(BF16)
    kvc_cmp = _compress(r, pos, w1, w2)

    oc, selt, flags = _nsa_cmp(slopes2, q, kvc_cmp, ov, gates, batch, seq, n_cmp)
    counts, words = _active_tiles(flags.reshape(flags.shape[0], NS_PAD), batch, seq)
    o_nsa = _nsa_attn(counts, words, q, selt, oc, gt, ksw, vt, batch, seq)

    split = MLA_HEADS * MLA_V
    x1 = _out_proj(x2, o_mla, o_nsa, w_o[0][:split].astype(BF16), w_o[0][split:].astype(BF16))
    out = _ffn(x1, ffn_norm_g[0][None], w_gate[0].astype(BF16), w_up[0].astype(BF16), w_down[0].astype(BF16),
               final_norm_g[None])
    return out.reshape(batch, seq, d)
```

```python
import functools
import math

import numpy as np
import jax
import jax.numpy as jnp
from jax import lax
from jax.experimental import pallas as pl
from jax.experimental.pallas import tpu as pltpu

F32 = jnp.float32
BF16 = jnp.bfloat16

EPS = 1e-6
NEG = -1e30
LOG2E = math.log2(math.e)
LANES = 128

MLA_HEADS = 8
MLA_Q_LORA = 512
MLA_KV_LORA = 256
MLA_NOPE = 128
MLA_ROPE = 64
MLA_V = 128
ROPE_THETA = 10000.0
MLA_TK = 256

NSA_HEADS = 16
NSA_GROUPS = 2
NSA_HPG = NSA_HEADS // NSA_GROUPS
NSA_DK = 64
CMP_LEN = 32
CMP_STRIDE = 16
CMP_HIDDEN = 128
SLC_LEN = 64
SLC_TOPK = 16
WINDOW = 512
FORCE_SCORE = 1e4
NSA_TILE = 128
NS_PAD = 128
N_GATES = 3 * NSA_HPG
POS_SPLIT = 64

VMEM_LIMIT = 56 * 1024 * 1024


def _params(n_axes):
    return pltpu.CompilerParams(dimension_semantics=("arbitrary",) * n_axes, vmem_limit_bytes=VMEM_LIMIT)


def _resident(shape):
    zeros = (0,) * len(shape)
    return pl.BlockSpec(shape, lambda *_: zeros, pipeline_mode=pl.Buffered(1))


def _rmsnorm(x, g):
    return x * lax.rsqrt(jnp.mean(x * x, axis=-1, keepdims=True) + EPS) * g


def _dot(a, b):
    return jnp.dot(a, b, preferred_element_type=F32)


def _dot_nt(a, b):
    return lax.dot_general(a, b, (((1,), (1,)), ((), ())), preferred_element_type=F32)


def _flash_weights(s, m_ref, l_ref, idx):
    m_prev = m_ref[idx]
    m_new = jnp.maximum(m_prev, jnp.max(s, axis=0, keepdims=True))
    alpha = jnp.exp2(m_prev - m_new)
    p = jnp.exp2(s - m_new)
    l_ref[idx] = alpha * l_ref[idx] + jnp.sum(p, axis=0, keepdims=True)
    m_ref[idx] = m_new
    return alpha, p.astype(BF16)


def _flash_update(s, v_t, m_ref, l_ref, acc_ref, idx):
    alpha, p = _flash_weights(s, m_ref, l_ref, idx)
    acc_ref[idx] = alpha * acc_ref[idx] + _dot(v_t, p)


def _skewed_flash(count, tile_at, keepf_fn, k_tile, vt_tile, q_scr, s_scr, m_ref, l_ref, acc_ref, idx):
    reps = q_scr.shape[0] // s_scr.shape[1]

    def scores(n, slot):
        s_scr[slot] = _dot_nt(k_tile(tile_at(jnp.minimum(n, count - 1))), q_scr[...])

    def consume(n, slot):
        j = tile_at(n)
        bias = jnp.where(keepf_fn(j) > 0.5, 0.0, NEG)
        _flash_update(s_scr[slot] + jnp.concatenate([bias] * reps, axis=1), vt_tile(j), m_ref, l_ref, acc_ref, idx)

    def body(u, carry):
        n = 2 * u
        scores(n + 1, 1)
        consume(n, 0)
        scores(n + 2, 0)
        consume(n + 1, 1)
        return carry

    scores(0, 0)
    lax.fori_loop(0, count // 2, body, 0)

    @pl.when(count % 2 == 1)
    def _():
        consume(count - 1, 0)


_IN_COLS = dict(cq=(0, 512), ckv=(512, 768), kra=(768, 896), krb=(896, 1024), q=(1024, 3072),
                kvc=(3072, 3328), ksw=(3328, 3840))
_VT_ROWS = 2 * NSA_GROUPS * NSA_DK


def _inproj_kernel(x_ref, g_ref, w_ref, wt_ref, qfeat_ref, cos_ref, sin_ref, kfeat_ref, cq_ref, ckv_ref, kr_ref,
                   q_ref, kvc_ref, ksw_ref, vt_ref, gt_ref):
    h = _rmsnorm(x_ref[...], g_ref[...]).astype(BF16)

    def mm(name):
        lo, hi = _IN_COLS[name]
        return _dot(h, w_ref[:, lo:hi])

    cq_ref[...] = mm("cq")
    ckv_ref[...] = mm("ckv")
    kr_ref[...] = (mm("kra") * cos_ref[...] + mm("krb") * sin_ref[...]).astype(BF16)
    q_ref[...] = (mm("q") + qfeat_ref[...]).astype(BF16)
    kvc_ref[...] = mm("kvc")
    kfeat = kfeat_ref[...]
    ksw_ref[...] = (mm("ksw") + jnp.concatenate([kfeat] * (2 * NSA_GROUPS), axis=1)).astype(BF16)
    t = _dot_nt(wt_ref[...], h)
    for c in range(vt_ref.shape[0]):
        cs = slice(c * NSA_TILE, (c + 1) * NSA_TILE)
        vt_ref[c] = t[:_VT_ROWS, cs].astype(BF16)
        gt_ref[c] = t[_VT_ROWS:_VT_ROWS + NSA_GROUPS * N_GATES, cs]


def _in_proj(x2, g, w, wt, qfeat, cos_t, sin_t, kfeat, seq, tm=256):
    n, d = x2.shape
    nt = seq // tm
    widths = {k: hi - lo for k, (lo, hi) in _IN_COLS.items()}
    row = lambda c: pl.BlockSpec((tm, c), lambda i: (i, 0))
    tab = pl.BlockSpec((tm, LANES), lambda i: (i % nt, 0))
    outs = [("cq", F32), ("ckv", F32), ("kra", BF16), ("q", BF16), ("kvc", F32), ("ksw", BF16)]
    tiles = tm // NSA_TILE
    tspec = lambda rows: pl.BlockSpec((tiles, rows, NSA_TILE), lambda i: (i, 0, 0))
    return pl.pallas_call(
        _inproj_kernel,
        grid=(n // tm,),
        in_specs=[row(d), _resident((1, d)), _resident(w.shape), _resident(wt.shape), _resident(qfeat.shape),
                  tab, tab, tab],
        out_specs=[row(widths[k]) for k, _ in outs] + [tspec(_VT_ROWS), tspec(NSA_GROUPS * N_GATES)],
        out_shape=[jax.ShapeDtypeStruct((n, widths[k]), dt) for k, dt in outs] + [
            jax.ShapeDtypeStruct((n // NSA_TILE, _VT_ROWS, NSA_TILE), BF16),
            jax.ShapeDtypeStruct((n // NSA_TILE, NSA_GROUPS * N_GATES, NSA_TILE), F32)],
        compiler_params=_params(1),
        name="in_proj",
    )(x2, g, w, wt, qfeat, cos_t, sin_t, kfeat)


def _mlaup_kernel(cq_ref, ckv_ref, gq_ref, gkv_ref, wq_ref, wk_ref, wvt_ref, cos_ref, sin_ref, qn_ref, qr_ref,
                  kn_ref, vt_ref, *, scale):
    cqn = _rmsnorm(cq_ref[...], gq_ref[...]).astype(BF16)
    ckvn = _rmsnorm(ckv_ref[...], gkv_ref[...]).astype(BF16)
    hn = MLA_HEADS * MLA_NOPE
    hr = MLA_HEADS * MLA_ROPE
    qn_ref[...] = (_dot(cqn, wq_ref[:, :hn]) * scale).astype(BF16)
    a = _dot(cqn, wq_ref[:, hn:hn + hr])
    b = _dot(cqn, wq_ref[:, hn + hr:])
    cos = cos_ref[...]
    sin = sin_ref[...]
    for p in range(hr // LANES):
        sl = slice(p * LANES, (p + 1) * LANES)
        qr_ref[:, sl] = ((a[:, sl] * cos + b[:, sl] * sin) * scale).astype(BF16)
    kn_ref[...] = _dot(ckvn, wk_ref[...]).astype(BF16)
    v_t = _dot_nt(wvt_ref[...], ckvn)
    for c in range(vt_ref.shape[0]):
        vt_ref[c] = v_t[:, c * MLA_TK:(c + 1) * MLA_TK].astype(BF16)


def _mla_up(cq, ckv, gq, gkv, wq, wk, wvt, cos_t, sin_t, seq, tm=512):
    n = cq.shape[0]
    nt = seq // tm
    hn = MLA_HEADS * MLA_NOPE
    hr = MLA_HEADS * MLA_ROPE
    hv = MLA_HEADS * MLA_V
    row = lambda c: pl.BlockSpec((tm, c), lambda i: (i, 0))
    tab = pl.BlockSpec((tm, LANES), lambda i: (i % nt, 0))
    scale = (MLA_NOPE + MLA_ROPE) ** -0.5 * LOG2E
    return pl.pallas_call(
        functools.partial(_mlaup_kernel, scale=scale),
        grid=(n // tm,),
        in_specs=[row(MLA_Q_LORA), row(MLA_KV_LORA), _resident(gq.shape), _resident(gkv.shape),
                  _resident(wq.shape), _resident(wk.shape), _resident(wvt.shape), tab, tab],
        out_specs=[row(hn), row(hr), row(hn), pl.BlockSpec((tm // MLA_TK, hv, MLA_TK), lambda i: (i, 0, 0))],
        out_shape=[jax.ShapeDtypeStruct((n, hn), BF16), jax.ShapeDtypeStruct((n, hr), BF16),
                   jax.ShapeDtypeStruct((n, hn), BF16), jax.ShapeDtypeStruct((n // MLA_TK, hv, MLA_TK), BF16)],
        compiler_params=_params(1),
        name="mla_up",
    )(cq, ckv, gq, gkv, wq, wk, wvt, cos_t, sin_t)


def _mla_attn_kernel(qn_ref, qr_ref, kn_ref, kr_ref, vt_ref, o_ref, m_ref, l_ref, acc_ref, s_scr, *, tq):
    i = pl.program_id(2)
    tk = MLA_TK
    lane = lax.broadcasted_iota(jnp.int32, (tq, LANES), 1)
    krow = lax.broadcasted_iota(jnp.int32, (tk, tq), 0)
    qcol = lax.broadcasted_iota(jnp.int32, (tk, tq), 1)
    qr = qr_ref[...]
    zero = jnp.zeros_like(qr)
    q_cat = [jnp.concatenate([qn_ref[:, hh * LANES:(hh + 1) * LANES],
                              jnp.where((lane < MLA_ROPE) == (hh == 0), qr, zero)], axis=1) for hh in range(2)]
    m_ref[...] = jnp.full_like(m_ref, NEG)
    l_ref[...] = jnp.zeros_like(l_ref)
    acc_ref[...] = jnp.zeros_like(acc_ref)

    def scores(j, slot):
        ks = pl.ds(pl.multiple_of(j * tk, tk), tk)
        k_rope = kr_ref[ks, :]
        for hh in range(2):
            hs = slice(hh * LANES, (hh + 1) * LANES)
            s_scr[slot, hh] = _dot_nt(jnp.concatenate([kn_ref[ks, hs], k_rope], axis=1), q_cat[hh])

    def consume(j, slot, diagonal):
        weights = []
        for hh in range(2):
            s = s_scr[slot, hh]
            if diagonal:
                s = jnp.where(j * tk + krow <= i * tq + qcol, s, NEG)
            weights.append(_flash_weights(s, m_ref, l_ref, hh))
        for hh in range(2):
            alpha, p = weights[hh]
            acc_ref[hh] = alpha * acc_ref[hh] + _dot(vt_ref[0, j, hh * LANES:(hh + 1) * LANES, :], p)

    def body(u, carry):
        j = 2 * u
        scores(j + 1, 1)
        consume(j, 0, False)
        scores(j + 2, 0)
        consume(j + 1, 1, False)
        return carry

    assert tq == 2 * tk
    scores(0, 0)
    lax.fori_loop(0, i, body, 0)
    scores(2 * i + 1, 1)
    consume(2 * i, 0, True)
    consume(2 * i + 1, 1, True)
    for hh in range(2):
        o_t = acc_ref[hh] / l_ref[hh]
        for c in range(tq // LANES):
            o_ref[c * LANES:(c + 1) * LANES, hh * LANES:(hh + 1) * LANES] = (
                o_t[:, c * LANES:(c + 1) * LANES].T.astype(BF16))


def _mla_attn(qn, qr, kn, kr, vt, batch, seq, tq=512):
    n = qn.shape[0]
    nq = seq // tq
    nk = seq // MLA_TK
    pairs = MLA_HEADS // 2
    pw = 2 * LANES
    qspec = lambda c: pl.BlockSpec((tq, c), lambda b, p, i: (b * nq + i, p))
    vt4 = vt.reshape(batch, nk, MLA_HEADS * MLA_V, MLA_TK)
    return pl.pallas_call(
        functools.partial(_mla_attn_kernel, tq=tq),
        grid=(batch, pairs, nq),
        in_specs=[qspec(pw), qspec(LANES), pl.BlockSpec((seq, pw), lambda b, p, i: (b, p)),
                  pl.BlockSpec((seq, LANES), lambda b, p, i: (b, 0)),
                  pl.BlockSpec((1, nk, pw, MLA_TK), lambda b, p, i: (b, 0, p, 0))],
        out_specs=qspec(pw),
        out_shape=jax.ShapeDtypeStruct((n, MLA_HEADS * MLA_V), BF16),
        scratch_shapes=[pltpu.VMEM((2, 1, tq), F32), pltpu.VMEM((2, 1, tq), F32), pltpu.VMEM((2, MLA_V, tq), F32),
                        pltpu.VMEM((2, 2, MLA_TK, tq), F32)],
        compiler_params=_params(3),
        name="mla_attn",
    )(qn, qr, kn, kr, vt4)


def _compress_kernel(r_ref, pos_ref, w1_ref, w2_ref, w2t_ref, feat_ref, o_ref, ot_ref):
    half = CMP_STRIDE * NSA_DK
    r = r_ref[0, 0, 0]
    pos = pos_ref[0]
    a = _dot((r + pos[:, :half]).astype(BF16), w1_ref[0, :half, :])
    b = _dot((r + pos[:, half:]).astype(BF16), w1_ref[0, half:, :])
    hid = a + pltpu.roll(b, b.shape[0] - 1, 0)
    act = (hid * jax.nn.sigmoid(hid)).astype(BF16)
    o_ref[0, 0, 0] = (_dot(act, w2_ref[0]) + feat_ref[...]).astype(BF16)
    ot_ref[0, 0, 0] = _dot_nt(w2t_ref[0], act).astype(BF16)


def _compress(r, pos, w1, w2, w2t, feat):
    _, batch, groups, nc, width = r.shape
    return pl.pallas_call(
        _compress_kernel,
        grid=(2, batch, groups),
        in_specs=[pl.BlockSpec((1, 1, 1, nc, width), lambda t, b, g: (t, b, g, 0, 0)),
                  pl.BlockSpec((1, 1, 2 * width), lambda t, b, g: (t, 0, 0)),
                  pl.BlockSpec((1, 2 * width, CMP_HIDDEN), lambda t, b, g: (t, 0, 0)),
                  pl.BlockSpec((1, CMP_HIDDEN, LANES), lambda t, b, g: (t, 0, 0)),
                  pl.BlockSpec((1, NSA_DK, CMP_HIDDEN), lambda t, b, g: (t, 0, 0)),
                  pl.BlockSpec((nc, LANES), lambda t, b, g: (0, 0))],
        out_specs=[pl.BlockSpec((1, 1, 1, nc, LANES), lambda t, b, g: (t, b, g, 0, 0)),
                   pl.BlockSpec((1, 1, 1, NSA_DK, nc), lambda t, b, g: (t, b, g, 0, 0))],
        out_shape=[jax.ShapeDtypeStruct((2, batch, groups, nc, LANES), BF16),
                   jax.ShapeDtypeStruct((2, batch, groups, NSA_DK, nc), BF16)],
        compiler_params=_params(3),
        name="compress",
    )(r, pos, w1, w2, w2t, feat)


def _nsa_cmp_kernel(q_ref, kc_ref, vaug_ref, gt_ref, oc_ref, selt_ref, flag_ref, q_scr, m_ref, l_ref, acc_ref, s_scr,
                    *, n_cmp):
    i = pl.program_id(2)
    t = NSA_TILE
    hpg = NSA_HPG
    for h in range(hpg):
        q_scr[h * t:(h + 1) * t, :] = q_ref[:, h * LANES:(h + 1) * LANES]
    m_ref[...] = jnp.full_like(m_ref, NEG)
    l_ref[...] = jnp.zeros_like(l_ref)
    acc_ref[...] = jnp.zeros_like(acc_ref)

    crow = lax.broadcasted_iota(jnp.int32, (t, t), 0)
    qcol = lax.broadcasted_iota(jnp.int32, (t, t), 1)
    tq = i * t + qcol

    def keepf(j):
        c = j * t + crow
        return jnp.where(CMP_STRIDE * c + (CMP_LEN - 1) <= tq, jnp.where(c < n_cmp, 1.0, 0.0), 0.0)

    count = jnp.minimum((i * t + t - CMP_LEN) // (CMP_STRIDE * t) + 1, kc_ref.shape[3] // t)
    _skewed_flash(count, lambda n: n, keepf, lambda j: kc_ref[0, 0, 0, pl.ds(pl.multiple_of(j * t, t), t), :],
                  lambda j: vaug_ref[0, 0, j], q_scr, s_scr, m_ref, l_ref, acc_ref, 0)

    some = jnp.where(tq[:1] >= CMP_LEN - 1, 1.0, 0.0)
    inv = jnp.concatenate([some] * hpg, axis=1) / l_ref[0]
    gates = jax.nn.sigmoid(gt_ref[0])
    imp = jnp.zeros((NS_PAD, t), F32)
    for pair in range(hpg // 2):
        halves = []
        for h in (2 * pair, 2 * pair + 1):
            hs = slice(h * t, (h + 1) * t)
            halves.append(gates[3 * h:3 * h + 1, :] * (acc_ref[0, :NSA_DK, hs] * inv[:, hs]))
            imp = imp + acc_ref[0, NSA_DK:, hs] * inv[:, hs]
        oc_ref[:, pair * LANES:(pair + 1) * LANES] = jnp.concatenate(halves, axis=0).T.astype(BF16)

    blk = lax.broadcasted_iota(jnp.int32, (NS_PAD, t), 0)
    blk_t = (i * t + lax.broadcasted_iota(jnp.int32, (NS_PAD, t), 1)) // SLC_LEN
    forced = jnp.where(blk == 0, 1.0, 0.0) + jnp.where(blk == blk_t, 1.0, 0.0) + jnp.where(blk == blk_t - 1, 1.0, 0.0)
    imp = jnp.where(forced > 0.5, FORCE_SCORE, jnp.where(blk <= blk_t, imp, -1.0))
    blkf = blk.astype(F32)
    sel = jnp.zeros((NS_PAD, t), F32)
    for _ in range(SLC_TOPK):
        best = jnp.max(imp, axis=0, keepdims=True)
        pick = jnp.min(jnp.where(imp == best, blkf, float(NS_PAD)), axis=0, keepdims=True)
        hit = blkf == pick
        sel = jnp.where(hit, 1.0, sel)
        imp = jnp.where(hit, -3e38, imp)
    selt_ref[0, 0] = sel
    flag_ref[0] = jnp.max(sel, axis=1, keepdims=True).astype(jnp.int32)


def _nsa_cmp(q, kc, vaug, gt, batch, seq, n_cmp):
    t = NSA_TILE
    nq = seq // t
    n = q.shape[0]
    ncp = kc.shape[3]
    rows = lambda c: pl.BlockSpec((t, c), lambda b, g, i: (b * nq + i, g))
    return pl.pallas_call(
        functools.partial(_nsa_cmp_kernel, n_cmp=n_cmp),
        grid=(batch, NSA_GROUPS, nq),
        in_specs=[rows(NSA_HPG * LANES),
                  pl.BlockSpec((1, 1, 1, ncp, LANES), lambda b, g, i: (0, b, g, 0, 0)),
                  pl.BlockSpec((1, 1) + vaug.shape[2:], lambda b, g, i: (b, g, 0, 0, 0)),
                  pl.BlockSpec((1, N_GATES, t), lambda b, g, i: (b * nq + i, g, 0))],
        out_specs=[rows(NSA_HPG * NSA_DK),
                   pl.BlockSpec((1, 1, NS_PAD, t), lambda b, g, i: (b, g, 0, i)),
                   pl.BlockSpec((1, NS_PAD, 1), lambda b, g, i: ((b * NSA_GROUPS + g) * nq + i, 0, 0))],
        out_shape=[jax.ShapeDtypeStruct((n, NSA_HEADS * NSA_DK), BF16),
                   jax.ShapeDtypeStruct((batch, NSA_GROUPS, NS_PAD, seq), F32),
                   jax.ShapeDtypeStruct((batch * NSA_GROUPS * nq, NS_PAD, 1), jnp.int32)],
        scratch_shapes=[pltpu.VMEM((NSA_HPG * t, LANES), BF16), pltpu.VMEM((1, 1, NSA_HPG * t), F32),
                        pltpu.VMEM((1, 1, NSA_HPG * t), F32), pltpu.VMEM((1, NSA_DK + NS_PAD, NSA_HPG * t), F32),
                        pltpu.VMEM((2, t, NSA_HPG * t), F32)],
        compiler_params=_params(3),
        name="nsa_cmp",
    )(q, kc, vaug, gt)


IDS_PER_WORD = 4


def _nsa_attn_kernel(counts_ref, words_ref, q_ref, selt_ref, oc_ref, gt_ref, ks_ref, kw_ref, vst_ref, vwt_ref, o_ref,
                     q_scr, m_ref, l_ref, acc_ref, s_scr, *, nq, words_per_step):
    b = pl.program_id(0)
    g = pl.program_id(1)
    i = pl.program_id(2)
    t = NSA_TILE
    hpg = NSA_HPG
    for h in range(hpg):
        q_scr[h * t:(h + 1) * t, :] = q_ref[:, h * LANES:(h + 1) * LANES]
    m_ref[...] = jnp.full_like(m_ref, NEG)
    l_ref[...] = jnp.zeros_like(l_ref)
    acc_ref[...] = jnp.zeros_like(acc_ref)

    krow = lax.broadcasted_iota(jnp.int32, (t, t), 0)
    qcol = lax.broadcasted_iota(jnp.int32, (t, t), 1)
    blocks_per_tile = t // SLC_LEN
    slc, win = 0, 1
    step = (b * NSA_GROUPS + g) * nq + i

    def run_branch(branch, count, tile_at, keepf_fn, k_ref, vt_ref):
        _skewed_flash(count, tile_at, keepf_fn, lambda j: k_ref[pl.ds(pl.multiple_of(j * t, t), t), :],
                      lambda j: vt_ref[0, j], q_scr, s_scr, m_ref, l_ref, acc_ref, branch)

    def slc_tile(n):
        word = words_ref[step * words_per_step + n // IDS_PER_WORD]
        return lax.shift_right_logical(word, 8 * (n % IDS_PER_WORD)) & 255

    def slc_keepf(j):
        dist = (i - j) * t + qcol - krow
        sel_rows = [selt_ref[0, 0, pl.ds(blocks_per_tile * j + r, 1), :] for r in range(blocks_per_tile)]
        picked = sel_rows[-1]
        for r in range(blocks_per_tile - 2, -1, -1):
            picked = jnp.where(krow < (r + 1) * SLC_LEN, sel_rows[r], picked)
        return jnp.where(dist >= 0, picked, 0.0)

    def win_keepf(j):
        dist = (i - j) * t + qcol - krow
        return jnp.where(dist >= 0, jnp.where(dist < WINDOW, 1.0, 0.0), 0.0)

    run_branch(slc, counts_ref[step], slc_tile, slc_keepf, ks_ref, vst_ref)
    first = jnp.maximum(i - WINDOW // t, 0)
    run_branch(win, i + 1 - first, lambda n: first + n, win_keepf, kw_ref, vwt_ref)

    gates = jax.nn.sigmoid(gt_ref[0])
    inv_s = 1.0 / l_ref[slc]
    inv_w = 1.0 / l_ref[win]
    for pair in range(hpg // 2):
        halves = []
        for h in (2 * pair, 2 * pair + 1):
            hs = slice(h * t, (h + 1) * t)
            halves.append(gates[3 * h + 1:3 * h + 2, :] * (acc_ref[slc, :, hs] * inv_s[:, hs])
                          + gates[3 * h + 2:3 * h + 3, :] * (acc_ref[win, :, hs] * inv_w[:, hs]))
        ps = slice(pair * LANES, (pair + 1) * LANES)
        o_ref[:, ps] = (oc_ref[:, ps].astype(F32) + jnp.concatenate(halves, axis=0).T).astype(BF16)


def _nsa_attn(counts, words, q, selt, oc, gt, ksw, vt, batch, seq):
    t = NSA_TILE
    nq = seq // t
    n = q.shape[0]
    words_per_step = words.shape[0] // (batch * NSA_GROUPS * nq)
    rows = lambda c: pl.BlockSpec((t, c), lambda b, g, i, *_: (b * nq + i, g))
    key = lambda which: pl.BlockSpec((seq, LANES), lambda b, g, i, *_: (b, which * NSA_GROUPS + g))
    vt4 = vt.reshape(batch, nq, _VT_ROWS, t)
    val = lambda which: pl.BlockSpec((1, nq, NSA_DK, t), lambda b, g, i, *_: (b, 0, which * NSA_GROUPS + g, 0))
    grid_spec = pltpu.PrefetchScalarGridSpec(
        num_scalar_prefetch=2,
        grid=(batch, NSA_GROUPS, nq),
        in_specs=[rows(NSA_HPG * LANES),
                  pl.BlockSpec((1, 1, NS_PAD, t), lambda b, g, i, *_: (b, g, 0, i)),
                  rows(NSA_HPG * NSA_DK),
                  pl.BlockSpec((1, N_GATES, t), lambda b, g, i, *_: (b * nq + i, g, 0)),
                  key(0), key(1), val(0), val(1)],
        out_specs=rows(NSA_HPG * NSA_DK),
        scratch_shapes=[pltpu.VMEM((NSA_HPG * t, LANES), BF16), pltpu.VMEM((2, 1, NSA_HPG * t), F32),
                        pltpu.VMEM((2, 1, NSA_HPG * t), F32), pltpu.VMEM((2, NSA_DK, NSA_HPG * t), F32),
                        pltpu.VMEM((2, t, NSA_HPG * t), F32)],
    )
    return pl.pallas_call(
        functools.partial(_nsa_attn_kernel, nq=nq, words_per_step=words_per_step),
        grid_spec=grid_spec,
        out_shape=jax.ShapeDtypeStruct((n, NSA_HEADS * NSA_DK), BF16),
        compiler_params=_params(3),
        name="nsa_attn",
    )(counts, words, q, selt, oc, gt, ksw, ksw, vt4, vt4)


def _outproj_kernel(x_ref, om_ref, on_ref, wm_ref, wn_ref, o_ref):
    o_ref[...] = x_ref[...] + _dot(om_ref[...], wm_ref[...]) + _dot(on_ref[...], wn_ref[...])


def _out_proj(x2, o_mla, o_nsa, wm, wn, tm=512):
    n, d = x2.shape
    row = lambda c: pl.BlockSpec((tm, c), lambda i: (i, 0))
    return pl.pallas_call(
        _outproj_kernel,
        grid=(n // tm,),
        in_specs=[row(d), row(o_mla.shape[1]), row(o_nsa.shape[1]), _resident(wm.shape), _resident(wn.shape)],
        out_specs=row(d),
        out_shape=jax.ShapeDtypeStruct((n, d), F32),
        compiler_params=_params(1),
        name="out_proj",
    )(x2, o_mla, o_nsa, wm, wn)


def _ffn_kernel(x_ref, g_ref, wg_ref, wu_ref, wd_ref, gf_ref, o_ref, h_scr, acc_scr):
    f = pl.program_id(1)

    @pl.when(f == 0)
    def _():
        h_scr[...] = _rmsnorm(x_ref[...], g_ref[...]).astype(BF16)
        acc_scr[...] = jnp.zeros_like(acc_scr)

    h = h_scr[...]
    gate = _dot(h, wg_ref[...])
    act = (gate * jax.nn.sigmoid(gate)) * _dot(h, wu_ref[...])
    acc_scr[...] += _dot(act.astype(BF16), wd_ref[...])

    @pl.when(f == pl.num_programs(1) - 1)
    def _():
        o_ref[...] = _rmsnorm(x_ref[...] + acc_scr[...], gf_ref[...])


def _ffn(x1, g, wg, wu, wd, gf, tm=512, tf=512):
    n, d = x1.shape
    dff = wg.shape[1]
    return pl.pallas_call(
        _ffn_kernel,
        grid=(n // tm, dff // tf),
        in_specs=[pl.BlockSpec((tm, d), lambda i, f: (i, 0)), _resident((1, d)),
                  pl.BlockSpec((d, tf), lambda i, f: (0, f)), pl.BlockSpec((d, tf), lambda i, f: (0, f)),
                  pl.BlockSpec((tf, d), lambda i, f: (f, 0)), _resident((1, d))],
        out_specs=pl.BlockSpec((tm, d), lambda i, f: (i, 0)),
        out_shape=jax.ShapeDtypeStruct((n, d), F32),
        scratch_shapes=[pltpu.VMEM((tm, d), BF16), pltpu.VMEM((tm, d), F32)],
        compiler_params=_params(2),
        name="ffn",
    )(x1, g, wg, wu, wd, gf)


def _pad_cols(w, width):
    return jnp.pad(w, ((0, 0), (0, width - w.shape[1])))


def _rot_cols(w):
    half = w.shape[1] // 2
    return jnp.concatenate([-w[:, half:], w[:, :half]], axis=1)


def _fused_in_weights(w_in):
    sizes = (MLA_Q_LORA, MLA_KV_LORA, MLA_ROPE, NSA_HEADS * NSA_DK) + (NSA_GROUPS * NSA_DK,) * 6 + (3 * NSA_HEADS,)
    offs = np.cumsum(sizes)[:-1].tolist()
    cq, ckv, kr, q, kc, vc, ks, vs, kw, vw, gate = jnp.split(w_in, offs, axis=1)
    d = w_in.shape[0]
    q_pad = _pad_cols((q * (NSA_DK ** -0.5 * LOG2E)).reshape(d * NSA_HEADS, NSA_DK), LANES)
    q_pad = q_pad.reshape(d, NSA_HEADS * LANES)
    per_group = lambda w, c: _pad_cols(w.reshape(d * NSA_GROUPS, c), LANES).reshape(d, NSA_GROUPS * LANES)
    kr_rot = _rot_cols(kr)
    cols = [cq, ckv, kr, kr, kr_rot, kr_rot, q_pad, kc, vc, per_group(ks, NSA_DK), per_group(kw, NSA_DK)]
    w = jnp.concatenate(cols, axis=1).astype(BF16)
    assert w.shape[1] == _IN_COLS["ksw"][1]
    rows_t = jnp.concatenate([vs, vw, gate], axis=1).T
    pad = -rows_t.shape[0] % 16
    return w, jnp.pad(rows_t, ((0, pad), (0, 0))).astype(BF16)


def _slope_features(slopes2):
    s1 = slopes2.astype(BF16).astype(F32)
    s2 = (slopes2 - s1).astype(BF16).astype(F32)
    s3 = (slopes2 - s1 - s2).astype(BF16).astype(F32)
    pieces = jnp.stack([s1, s2, s3, s1, s2, s3], axis=1)
    return jnp.pad(pieces, ((0, 0), (NSA_DK, LANES - NSA_DK - 6))).reshape(1, -1)


def _position_features(pos):
    hi = (POS_SPLIT * (pos // POS_SPLIT)).astype(F32)
    lo = (pos % POS_SPLIT).astype(F32)
    return jnp.pad(jnp.stack([hi, hi, hi, lo, lo, lo], axis=1), ((0, 0), (NSA_DK, LANES - NSA_DK - 6)))


def _fused_uq_weight(w_uq):
    d = w_uq.shape[0]
    w = w_uq.reshape(d, MLA_HEADS, MLA_NOPE + MLA_ROPE)
    nope = w[:, :, :MLA_NOPE].reshape(d, MLA_HEADS * MLA_NOPE)
    rope = w[:, :, MLA_NOPE:]
    rope_rot = jnp.concatenate([-rope[:, :, MLA_ROPE // 2:], rope[:, :, :MLA_ROPE // 2]], axis=2)
    flat = lambda r: r.reshape(d, MLA_HEADS * MLA_ROPE)
    return jnp.concatenate([nope, flat(rope), flat(rope_rot)], axis=1).astype(BF16)


def _rope_tables(seq):
    inv = ROPE_THETA ** (-jnp.arange(0, MLA_ROPE, 2, dtype=F32) / MLA_ROPE)
    ang = jnp.arange(seq, dtype=F32)[:, None] * inv[None, :]
    reps = 2 * LANES // MLA_ROPE
    return jnp.tile(jnp.cos(ang), (1, reps)), jnp.tile(jnp.sin(ang), (1, reps))


def _overlap_matrix(seq, n_cmp, ncp):
    cmp_start = CMP_STRIDE * np.arange(n_cmp)
    slc_start = SLC_LEN * np.arange(seq // SLC_LEN)
    ov = np.clip(np.minimum(cmp_start[:, None] + CMP_LEN, slc_start[None, :] + SLC_LEN)
                 - np.maximum(cmp_start[:, None], slc_start[None, :]), 0, None).astype(np.float32) / CMP_STRIDE
    out = np.zeros((NS_PAD, ncp), np.float32)
    out[:ov.shape[1], :n_cmp] = ov.T
    return jnp.asarray(out, BF16)


def _active_tiles(flags, batch, seq):
    t_blocks = NSA_TILE // SLC_LEN
    nq = seq // NSA_TILE
    steps = flags.shape[0]
    tiles = flags.reshape(steps, NS_PAD // t_blocks, t_blocks).max(axis=-1)
    tile_id = jnp.arange(tiles.shape[1], dtype=jnp.int32)[None, :]
    q_tile = (jnp.arange(steps, dtype=jnp.int32) % nq)[:, None]
    active = (tiles > 0) & (tile_id <= q_tile)
    order = jnp.argsort(jnp.where(active, tile_id, tile_id + tiles.shape[1]), axis=-1).astype(jnp.int32)
    shifts = 8 * jnp.arange(IDS_PER_WORD, dtype=jnp.int32)
    words = (order.reshape(steps, -1, IDS_PER_WORD) << shifts).sum(axis=-1).astype(jnp.int32)
    return active.sum(axis=-1).astype(jnp.int32), words.reshape(-1)


def kernel(x, attn_norm_g, w_in, mla_q_norm_g, mla_kv_norm_g, w_uq, w_uk, w_uv, cmp_pos_k, cmp_pos_v, w_cmp_k1,
           w_cmp_k2, w_cmp_v1, w_cmp_v2, w_o, ffn_norm_g, w_gate, w_up, w_down, final_norm_g):
    batch, seq, d = x.shape
    n = batch * seq
    assert w_in.shape[0] == 1, "the final RMSNorm is fused into the FFN kernel of a single layer"
    assert seq % (CMP_STRIDE * NSA_TILE) == 0 and seq // SLC_LEN <= NS_PAD and seq // POS_SPLIT <= 256
    n_cmp = (seq - CMP_LEN) // CMP_STRIDE + 1
    ncp = seq // CMP_STRIDE
    cos_t, sin_t = _rope_tables(seq)
    ov_t = _overlap_matrix(seq, n_cmp, ncp)
    slopes2 = 2.0 ** (-8.0 * jnp.arange(1, NSA_HEADS + 1, dtype=F32) / NSA_HEADS) * LOG2E
    qfeat = _slope_features(slopes2)
    kfeat = _position_features(jnp.arange(seq, dtype=jnp.int32))
    x2 = x.reshape(n, d)

    w_fused, w_fused_t = _fused_in_weights(w_in[0])
    cq, ckv, kr, q, kvc, ksw, vt, gt = _in_proj(
        x2, attn_norm_g[0][None], w_fused, w_fused_t, qfeat, cos_t, sin_t, kfeat, seq)

    qn, qr, kn, vt_mla = _mla_up(cq, ckv, mla_q_norm_g[0][None], mla_kv_norm_g[0][None], _fused_uq_weight(w_uq[0]),
                                 w_uk[0].astype(BF16), w_uv[0].T.astype(BF16), cos_t, sin_t, seq)
    o_mla = _mla_attn(qn, qr, kn, kr, vt_mla, batch, seq)

    r = kvc.reshape(batch, seq, 2, NSA_GROUPS, NSA_DK).transpose(2, 0, 3, 1, 4)
    r = r.reshape(2, batch, NSA_GROUPS, ncp, CMP_STRIDE * NSA_DK)
    pos = jnp.stack([cmp_pos_k[0].reshape(1, -1), cmp_pos_v[0].reshape(1, -1)])
    w1 = jnp.stack([w_cmp_k1[0], w_cmp_v1[0]]).astype(BF16)
    w2 = jnp.stack([_pad_cols(w_cmp_k2[0], LANES), _pad_cols(w_cmp_v2[0], LANES)]).astype(BF16)
    w2t = jnp.stack([w_cmp_k2[0].T, w_cmp_v2[0].T]).astype(BF16)
    cmp_end = CMP_STRIDE * jnp.arange(ncp, dtype=jnp.int32) + (CMP_LEN - 1)
    cmp_rows, cmp_t = _compress(r, pos, w1, w2, w2t, _position_features(cmp_end))
    vaug = jnp.concatenate([cmp_t[1], jnp.broadcast_to(ov_t, (batch, NSA_GROUPS) + ov_t.shape)], axis=2)
    vaug = vaug.reshape(batch, NSA_GROUPS, NSA_DK + NS_PAD, ncp // NSA_TILE, NSA_TILE).transpose(0, 1, 3, 2, 4)

    oc, selt, flags = _nsa_cmp(q, cmp_rows, vaug, gt, batch, seq, n_cmp)
    counts, words = _active_tiles(flags.reshape(flags.shape[0], NS_PAD), batch, seq)
    o_nsa = _nsa_attn(counts, words, q, selt, oc, gt, ksw, vt, batch, seq)

    split = MLA_HEADS * MLA_V
    x1 = _out_proj(x2, o_mla, o_nsa, w_o[0][:split].astype(BF16), w_o[0][split:].astype(BF16))
    out = _ffn(x1, ffn_norm_g[0][None], w_gate[0].astype(BF16), w_up[0].astype(BF16), w_down[0].astype(BF16),
               final_norm_g[None])
    return out.reshape(batch, seq, d)
```

```python
import functools
import math

import numpy as np
import jax
import jax.numpy as jnp
from jax import lax
from jax.experimental import pallas as pl
from jax.experimental.pallas import tpu as pltpu

F32 = jnp.float32
BF16 = jnp.bfloat16

EPS = 1e-6
NEG = -1e30
LOG2E = math.log2(math.e)
LANES = 128

MLA_HEADS = 8
MLA_Q_LORA = 512
MLA_KV_LORA = 256
MLA_NOPE = 128
MLA_ROPE = 64
MLA_V = 128
ROPE_THETA = 10000.0
MLA_TK = 512

NSA_HEADS = 16
NSA_GROUPS = 2
NSA_HPG = NSA_HEADS // NSA_GROUPS
NSA_DK = 64
CMP_LEN = 32
CMP_STRIDE = 16
CMP_HIDDEN = 128
SLC_LEN = 64
SLC_TOPK = 16
WINDOW = 512
FORCE_SCORE = 1e4
NSA_TILE = 128
NS_PAD = 128
N_GATES = 3 * NSA_HPG
POS_SPLIT = 64
ONES_ROWS = 16

VMEM_LIMIT = 56 * 1024 * 1024


def _params(n_axes):
    return pltpu.CompilerParams(dimension_semantics=("arbitrary",) * n_axes, vmem_limit_bytes=VMEM_LIMIT)


def _resident(shape):
    zeros = (0,) * len(shape)
    return pl.BlockSpec(shape, lambda *_: zeros, pipeline_mode=pl.Buffered(1))


def _rmsnorm(x, g):
    return x * lax.rsqrt(jnp.mean(x * x, axis=-1, keepdims=True) + EPS) * g


def _dot(a, b):
    return jnp.dot(a, b, preferred_element_type=F32)


def _dot_nt(a, b):
    return lax.dot_general(a, b, (((1,), (1,)), ((), ())), preferred_element_type=F32)


def _ones_rows(width):
    return jnp.where(lax.broadcasted_iota(jnp.int32, (ONES_ROWS, width), 0) == 0, 1.0, 0.0).astype(BF16)


def _flash_weights(s, m_ref, idx):
    m_prev = m_ref[idx]
    m_new = jnp.maximum(m_prev, jnp.max(s, axis=0, keepdims=True))
    m_ref[idx] = m_new
    return jnp.exp2(m_prev - m_new), jnp.exp2(s - m_new).astype(BF16)


def _flash_update(s, v_t, m_ref, acc_ref, idx):
    alpha, p = _flash_weights(s, m_ref, idx)
    acc_ref[idx] = alpha * acc_ref[idx] + _dot(v_t, p)


def _skewed_flash(count, tile_at, keepf_fn, k_tile, vt_tile, q_scr, s_scr, m_ref, acc_ref, idx):
    pairs = (count + 1) // 2

    def pair(u):
        return [tile_at(jnp.minimum(2 * u + r, count - 1)) for r in range(2)]

    def scores(u, slot):
        j0, j1 = pair(jnp.minimum(u, pairs - 1))
        s_scr[slot] = _dot_nt(jnp.concatenate([k_tile(j0), k_tile(j1)], axis=0), q_scr[...])

    def consume(u, slot):
        j0, j1 = pair(u)
        second = jnp.where(2 * u + 1 < count, 1.0, 0.0)
        bias = jnp.concatenate([jnp.where(keepf_fn(j0) > 0.5, 0.0, NEG),
                                jnp.where(keepf_fn(j1) * second > 0.5, 0.0, NEG)], axis=0)
        s = s_scr[slot]
        s = s + jnp.concatenate([bias] * (s.shape[1] // bias.shape[1]), axis=1)
        _flash_update(s, jnp.concatenate([vt_tile(j0), vt_tile(j1)], axis=1), m_ref, acc_ref, idx)

    def body(w, carry):
        u = 2 * w
        scores(u + 1, 1)
        consume(u, 0)
        scores(u + 2, 0)
        consume(u + 1, 1)
        return carry

    scores(0, 0)
    lax.fori_loop(0, pairs // 2, body, 0)

    @pl.when(pairs % 2 == 1)
    def _():
        consume(pairs - 1, 0)


_IN_COLS = dict(cq=(0, 512), ckv=(512, 768), kra=(768, 896), krb=(896, 1024), q=(1024, 3072),
                kvc=(3072, 3328), ksw=(3328, 3840))
_VT_ROWS = 2 * NSA_GROUPS * NSA_DK
_VT_OUT = _VT_ROWS // NSA_DK * (NSA_DK + ONES_ROWS)


def _inproj_kernel(x_ref, g_ref, w_ref, wt_ref, qfeat_ref, cos_ref, sin_ref, kfeat_ref, cq_ref, ckv_ref, kr_ref,
                   q_ref, kvc_ref, ksw_ref, vt_ref, gt_ref):
    h = _rmsnorm(x_ref[...], g_ref[...]).astype(BF16)

    def mm(name):
        lo, hi = _IN_COLS[name]
        return _dot(h, w_ref[:, lo:hi])

    cq_ref[...] = mm("cq")
    ckv_ref[...] = mm("ckv")
    kr_ref[...] = (mm("kra") * cos_ref[...] + mm("krb") * sin_ref[...]).astype(BF16)
    q_ref[...] = (mm("q") + qfeat_ref[...]).astype(BF16)
    kvc_ref[...] = mm("kvc")
    kfeat = kfeat_ref[...]
    ksw_ref[...] = (mm("ksw") + jnp.concatenate([kfeat] * (2 * NSA_GROUPS), axis=1)).astype(BF16)
    t = _dot_nt(wt_ref[...], h)
    ones = _ones_rows(NSA_TILE)
    for c in range(vt_ref.shape[0]):
        cs = slice(c * NSA_TILE, (c + 1) * NSA_TILE)
        pieces = []
        for k in range(_VT_ROWS // NSA_DK):
            pieces += [t[k * NSA_DK:(k + 1) * NSA_DK, cs].astype(BF16), ones]
        vt_ref[c] = jnp.concatenate(pieces, axis=0)
        gt_ref[c] = t[_VT_ROWS:_VT_ROWS + NSA_GROUPS * N_GATES, cs]


def _in_proj(x2, g, w, wt, qfeat, cos_t, sin_t, kfeat, seq, tm=256):
    n, d = x2.shape
    nt = seq // tm
    widths = {k: hi - lo for k, (lo, hi) in _IN_COLS.items()}
    row = lambda c: pl.BlockSpec((tm, c), lambda i: (i, 0))
    tab = pl.BlockSpec((tm, LANES), lambda i: (i % nt, 0))
    outs = [("cq", F32), ("ckv", F32), ("kra", BF16), ("q", BF16), ("kvc", F32), ("ksw", BF16)]
    tiles = tm // NSA_TILE
    tspec = lambda rows: pl.BlockSpec((tiles, rows, NSA_TILE), lambda i: (i, 0, 0))
    return pl.pallas_call(
        _inproj_kernel,
        grid=(n // tm,),
        in_specs=[row(d), _resident((1, d)), _resident(w.shape), _resident(wt.shape), _resident(qfeat.shape),
                  tab, tab, tab],
        out_specs=[row(widths[k]) for k, _ in outs] + [tspec(_VT_OUT), tspec(NSA_GROUPS * N_GATES)],
        out_shape=[jax.ShapeDtypeStruct((n, widths[k]), dt) for k, dt in outs] + [
            jax.ShapeDtypeStruct((n // NSA_TILE, _VT_OUT, NSA_TILE), BF16),
            jax.ShapeDtypeStruct((n // NSA_TILE, NSA_GROUPS * N_GATES, NSA_TILE), F32)],
        compiler_params=_params(1),
        name="in_proj",
    )(x2, g, w, wt, qfeat, cos_t, sin_t, kfeat)


def _mlaup_kernel(cq_ref, ckv_ref, gq_ref, gkv_ref, wq_ref, wk_ref, wvt_ref, cos_ref, sin_ref, qn_ref, qr_ref,
                  kn_ref, vt_ref, *, scale):
    cqn = _rmsnorm(cq_ref[...], gq_ref[...]).astype(BF16)
    ckvn = _rmsnorm(ckv_ref[...], gkv_ref[...]).astype(BF16)
    hn = MLA_HEADS * MLA_NOPE
    hr = MLA_HEADS * MLA_ROPE
    qn_ref[...] = (_dot(cqn, wq_ref[:, :hn]) * scale).astype(BF16)
    a = _dot(cqn, wq_ref[:, hn:hn + hr])
    b = _dot(cqn, wq_ref[:, hn + hr:])
    cos = cos_ref[...]
    sin = sin_ref[...]
    for p in range(hr // LANES):
        sl = slice(p * LANES, (p + 1) * LANES)
        qr_ref[:, sl] = ((a[:, sl] * cos + b[:, sl] * sin) * scale).astype(BF16)
    kn_ref[...] = _dot(ckvn, wk_ref[...]).astype(BF16)
    v_t = _dot_nt(wvt_ref[...], ckvn).astype(BF16)
    ones = _ones_rows(MLA_TK)
    for c in range(vt_ref.shape[0]):
        pieces = []
        for h in range(MLA_HEADS):
            pieces += [v_t[h * MLA_V:(h + 1) * MLA_V, c * MLA_TK:(c + 1) * MLA_TK], ones]
        vt_ref[c] = jnp.concatenate(pieces, axis=0)


def _mla_up(cq, ckv, gq, gkv, wq, wk, wvt, cos_t, sin_t, seq, tm=512):
    n = cq.shape[0]
    nt = seq // tm
    hn = MLA_HEADS * MLA_NOPE
    hr = MLA_HEADS * MLA_ROPE
    hv = MLA_HEADS * (MLA_V + ONES_ROWS)
    row = lambda c: pl.BlockSpec((tm, c), lambda i: (i, 0))
    tab = pl.BlockSpec((tm, LANES), lambda i: (i % nt, 0))
    scale = (MLA_NOPE + MLA_ROPE) ** -0.5 * LOG2E
    return pl.pallas_call(
        functools.partial(_mlaup_kernel, scale=scale),
        grid=(n // tm,),
        in_specs=[row(MLA_Q_LORA), row(MLA_KV_LORA), _resident(gq.shape), _resident(gkv.shape),
                  _resident(wq.shape), _resident(wk.shape), _resident(wvt.shape), tab, tab],
        out_specs=[row(hn), row(hr), row(hn), pl.BlockSpec((tm // MLA_TK, hv, MLA_TK), lambda i: (i, 0, 0))],
        out_shape=[jax.ShapeDtypeStruct((n, hn), BF16), jax.ShapeDtypeStruct((n, hr), BF16),
                   jax.ShapeDtypeStruct((n, hn), BF16), jax.ShapeDtypeStruct((n // MLA_TK, hv, MLA_TK), BF16)],
        compiler_params=_params(1),
        name="mla_up",
    )(cq, ckv, gq, gkv, wq, wk, wvt, cos_t, sin_t)


def _mla_attn_kernel(qn_ref, qr_ref, kn_ref, kr_ref, vt_ref, o_ref, m_ref, acc_ref, s_scr, *, tq):
    i = pl.program_id(2)
    tk = MLA_TK
    lane = lax.broadcasted_iota(jnp.int32, (tq, LANES), 1)
    krow = lax.broadcasted_iota(jnp.int32, (tk, tq), 0)
    qcol = lax.broadcasted_iota(jnp.int32, (tk, tq), 1)
    qr = qr_ref[...]
    zero = jnp.zeros_like(qr)
    q_cat = [jnp.concatenate([qn_ref[:, hh * LANES:(hh + 1) * LANES],
                              jnp.where((lane < MLA_ROPE) == (hh == 0), qr, zero)], axis=1) for hh in range(2)]
    m_ref[...] = jnp.full_like(m_ref, NEG)
    acc_ref[...] = jnp.zeros_like(acc_ref)
    vrows = MLA_V + ONES_ROWS

    def scores(j, slot):
        ks = pl.ds(pl.multiple_of(j * tk, tk), tk)
        k_rope = kr_ref[ks, :]
        for hh in range(2):
            hs = slice(hh * LANES, (hh + 1) * LANES)
            s_scr[slot, hh] = _dot_nt(jnp.concatenate([kn_ref[ks, hs], k_rope], axis=1), q_cat[hh])

    def consume(j, slot, diagonal):
        weights = []
        for hh in range(2):
            s = s_scr[slot, hh]
            if diagonal:
                s = jnp.where(j * tk + krow <= i * tq + qcol, s, NEG)
            weights.append(_flash_weights(s, m_ref, hh))
        for hh in range(2):
            alpha, p = weights[hh]
            acc_ref[hh] = alpha * acc_ref[hh] + _dot(vt_ref[0, j, hh * vrows:(hh + 1) * vrows, :], p)

    def body(u, carry):
        j = 2 * u
        scores(j + 1, 1)
        consume(j, 0, False)
        scores(j + 2, 0)
        consume(j + 1, 1, False)
        return carry

    assert tq == tk, "the last key tile of a query tile must be its only diagonal tile"
    scores(0, 0)
    lax.fori_loop(0, i // 2, body, 0)

    @pl.when(i % 2 == 1)
    def _():
        scores(i, 1)
        consume(i - 1, 0, False)
        consume(i, 1, True)

    @pl.when(i % 2 == 0)
    def _():
        consume(i, 0, True)

    for hh in range(2):
        o_t = acc_ref[hh, :MLA_V] / acc_ref[hh, MLA_V:MLA_V + 1]
        for c in range(tq // LANES):
            o_ref[c * LANES:(c + 1) * LANES, hh * LANES:(hh + 1) * LANES] = (
                o_t[:, c * LANES:(c + 1) * LANES].T.astype(BF16))


def _mla_attn(qn, qr, kn, kr, vt, batch, seq, tq=512):
    n = qn.shape[0]
    nq = seq // tq
    nk = seq // MLA_TK
    pairs = MLA_HEADS // 2
    pw = 2 * LANES
    qspec = lambda c: pl.BlockSpec((tq, c), lambda b, p, i: (b * nq + i, p))
    vrows = MLA_V + ONES_ROWS
    vt4 = vt.reshape(batch, nk, MLA_HEADS * vrows, MLA_TK)
    return pl.pallas_call(
        functools.partial(_mla_attn_kernel, tq=tq),
        grid=(batch, pairs, nq),
        in_specs=[qspec(pw), qspec(LANES), pl.BlockSpec((seq, pw), lambda b, p, i: (b, p)),
                  pl.BlockSpec((seq, LANES), lambda b, p, i: (b, 0)),
                  pl.BlockSpec((1, nk, 2 * vrows, MLA_TK), lambda b, p, i: (b, 0, p, 0))],
        out_specs=qspec(pw),
        out_shape=jax.ShapeDtypeStruct((n, MLA_HEADS * MLA_V), BF16),
        scratch_shapes=[pltpu.VMEM((2, 1, tq), F32), pltpu.VMEM((2, vrows, tq), F32),
                        pltpu.VMEM((2, 2, MLA_TK, tq), F32)],
        compiler_params=_params(3),
        name="mla_attn",
    )(qn, qr, kn, kr, vt4)


def _compress_kernel(r_ref, pos_ref, w1_ref, w2_ref, w2t_ref, feat_ref, o_ref, ot_ref):
    half = CMP_STRIDE * NSA_DK
    r = r_ref[0, 0, 0]
    pos = pos_ref[0]
    a = _dot((r + pos[:, :half]).astype(BF16), w1_ref[0, :half, :])
    b = _dot((r + pos[:, half:]).astype(BF16), w1_ref[0, half:, :])
    hid = a + pltpu.roll(b, b.shape[0] - 1, 0)
    act = (hid * jax.nn.sigmoid(hid)).astype(BF16)
    o_ref[0, 0, 0] = (_dot(act, w2_ref[0]) + feat_ref[...]).astype(BF16)
    ot_ref[0, 0, 0] = _dot_nt(w2t_ref[0], act).astype(BF16)


def _compress(r, pos, w1, w2, w2t, feat):
    _, batch, groups, nc, width = r.shape
    return pl.pallas_call(
        _compress_kernel,
        grid=(2, batch, groups),
        in_specs=[pl.BlockSpec((1, 1, 1, nc, width), lambda t, b, g: (t, b, g, 0, 0)),
                  pl.BlockSpec((1, 1, 2 * width), lambda t, b, g: (t, 0, 0)),
                  pl.BlockSpec((1, 2 * width, CMP_HIDDEN), lambda t, b, g: (t, 0, 0)),
                  pl.BlockSpec((1, CMP_HIDDEN, LANES), lambda t, b, g: (t, 0, 0)),
                  pl.BlockSpec((1, NSA_DK, CMP_HIDDEN), lambda t, b, g: (t, 0, 0)),
                  pl.BlockSpec((nc, LANES), lambda t, b, g: (0, 0))],
        out_specs=[pl.BlockSpec((1, 1, 1, nc, LANES), lambda t, b, g: (t, b, g, 0, 0)),
                   pl.BlockSpec((1, 1, 1, NSA_DK, nc), lambda t, b, g: (t, b, g, 0, 0))],
        out_shape=[jax.ShapeDtypeStruct((2, batch, groups, nc, LANES), BF16),
                   jax.ShapeDtypeStruct((2, batch, groups, NSA_DK, nc), BF16)],
        compiler_params=_params(3),
        name="compress",
    )(r, pos, w1, w2, w2t, feat)


def _nsa_cmp_kernel(q_ref, kc_ref, vaug_ref, gt_ref, oc_ref, selt_ref, flag_ref, q_scr, m_ref, acc_ref, s_scr,
                    *, n_cmp):
    i = pl.program_id(2)
    t = NSA_TILE
    hpg = NSA_HPG
    for h in range(hpg):
        q_scr[h * t:(h + 1) * t, :] = q_ref[:, h * LANES:(h + 1) * LANES]
    m_ref[...] = jnp.full_like(m_ref, NEG)
    acc_ref[...] = jnp.zeros_like(acc_ref)

    crow = lax.broadcasted_iota(jnp.int32, (t, t), 0)
    qcol = lax.broadcasted_iota(jnp.int32, (t, t), 1)
    tq = i * t + qcol

    def keepf(j):
        c = j * t + crow
        return jnp.where(CMP_STRIDE * c + (CMP_LEN - 1) <= tq, jnp.where(c < n_cmp, 1.0, 0.0), 0.0)

    count = jnp.minimum((i * t + t - CMP_LEN) // (CMP_STRIDE * t) + 1, kc_ref.shape[3] // t)
    _skewed_flash(count, lambda n: n, keepf, lambda j: kc_ref[0, 0, 0, pl.ds(pl.multiple_of(j * t, t), t), :],
                  lambda j: vaug_ref[0, 0, j], q_scr, s_scr, m_ref, acc_ref, 0)

    some = jnp.where(tq[:1] >= CMP_LEN - 1, 1.0, 0.0)
    inv = jnp.concatenate([some] * hpg, axis=1) / acc_ref[0, NSA_DK + NS_PAD:NSA_DK + NS_PAD + 1]
    gates = jax.nn.sigmoid(gt_ref[0])
    imp = jnp.zeros((NS_PAD, t), F32)
    for pair in range(hpg // 2):
        halves = []
        for h in (2 * pair, 2 * pair + 1):
            hs = slice(h * t, (h + 1) * t)
            halves.append(gates[3 * h:3 * h + 1, :] * (acc_ref[0, :NSA_DK, hs] * inv[:, hs]))
            imp = imp + acc_ref[0, NSA_DK:NSA_DK + NS_PAD, hs] * inv[:, hs]
        oc_ref[:, pair * LANES:(pair + 1) * LANES] = jnp.concatenate(halves, axis=0).T.astype(BF16)

    blk = lax.broadcasted_iota(jnp.int32, (NS_PAD, t), 0)
    blk_t = (i * t + lax.broadcasted_iota(jnp.int32, (NS_PAD, t), 1)) // SLC_LEN
    forced = jnp.where(blk == 0, 1.0, 0.0) + jnp.where(blk == blk_t, 1.0, 0.0) + jnp.where(blk == blk_t - 1, 1.0, 0.0)
    imp = jnp.where(forced > 0.5, FORCE_SCORE, jnp.where(blk <= blk_t, imp, -1.0))
    blkf = blk.astype(F32)
    sel = jnp.zeros((NS_PAD, t), F32)
    for _ in range(SLC_TOPK):
        best = jnp.max(imp, axis=0, keepdims=True)
        pick = jnp.min(jnp.where(imp == best, blkf, float(NS_PAD)), axis=0, keepdims=True)
        hit = blkf == pick
        sel = jnp.where(hit, 1.0, sel)
        imp = jnp.where(hit, -3e38, imp)
    selt_ref[0, 0] = sel
    flag_ref[0] = jnp.max(sel, axis=1, keepdims=True).astype(jnp.int32)


def _nsa_cmp(q, kc, vaug, gt, batch, seq, n_cmp):
    t = NSA_TILE
    nq = seq // t
    n = q.shape[0]
    ncp = kc.shape[3]
    rows = lambda c: pl.BlockSpec((t, c), lambda b, g, i: (b * nq + i, g))
    return pl.pallas_call(
        functools.partial(_nsa_cmp_kernel, n_cmp=n_cmp),
        grid=(batch, NSA_GROUPS, nq),
        in_specs=[rows(NSA_HPG * LANES),
                  pl.BlockSpec((1, 1, 1, ncp, LANES), lambda b, g, i: (0, b, g, 0, 0)),
                  pl.BlockSpec((1, 1) + vaug.shape[2:], lambda b, g, i: (b, g, 0, 0, 0)),
                  pl.BlockSpec((1, N_GATES, t), lambda b, g, i: (b * nq + i, g, 0))],
        out_specs=[rows(NSA_HPG * NSA_DK),
                   pl.BlockSpec((1, 1, NS_PAD, t), lambda b, g, i: (b, g, 0, i)),
                   pl.BlockSpec((1, NS_PAD, 1), lambda b, g, i: ((b * NSA_GROUPS + g) * nq + i, 0, 0))],
        out_shape=[jax.ShapeDtypeStruct((n, NSA_HEADS * NSA_DK), BF16),
                   jax.ShapeDtypeStruct((batch, NSA_GROUPS, NS_PAD, seq), F32),
                   jax.ShapeDtypeStruct((batch * NSA_GROUPS * nq, NS_PAD, 1), jnp.int32)],
        scratch_shapes=[pltpu.VMEM((NSA_HPG * t, LANES), BF16), pltpu.VMEM((1, 1, NSA_HPG * t), F32),
                        pltpu.VMEM((1, NSA_DK + NS_PAD + ONES_ROWS, NSA_HPG * t), F32),
                        pltpu.VMEM((2, 2 * t, NSA_HPG * t), F32)],
        compiler_params=_params(3),
        name="nsa_cmp",
    )(q, kc, vaug, gt)


IDS_PER_WORD = 4


def _nsa_attn_kernel(counts_ref, words_ref, q_ref, selt_ref, oc_ref, gt_ref, ks_ref, kw_ref, vst_ref, vwt_ref, o_ref,
                     q_scr, m_ref, acc_ref, s_scr, *, nq, words_per_step):
    b = pl.program_id(0)
    g = pl.program_id(1)
    i = pl.program_id(2)
    t = NSA_TILE
    hpg = NSA_HPG
    for h in range(hpg):
        q_scr[h * t:(h + 1) * t, :] = q_ref[:, h * LANES:(h + 1) * LANES]
    m_ref[...] = jnp.full_like(m_ref, NEG)
    acc_ref[...] = jnp.zeros_like(acc_ref)

    krow = lax.broadcasted_iota(jnp.int32, (t, t), 0)
    qcol = lax.broadcasted_iota(jnp.int32, (t, t), 1)
    blocks_per_tile = t // SLC_LEN
    slc, win = 0, 1
    step = (b * NSA_GROUPS + g) * nq + i

    def run_branch(branch, count, tile_at, keepf_fn, k_ref, vt_ref):
        _skewed_flash(count, tile_at, keepf_fn, lambda j: k_ref[pl.ds(pl.multiple_of(j * t, t), t), :],
                      lambda j: vt_ref[0, j], q_scr, s_scr, m_ref, acc_ref, branch)

    def slc_tile(n):
        word = words_ref[step * words_per_step + n // IDS_PER_WORD]
        return lax.shift_right_logical(word, 8 * (n % IDS_PER_WORD)) & 255

    def slc_keepf(j):
        dist = (i - j) * t + qcol - krow
        sel_rows = [selt_ref[0, 0, pl.ds(blocks_per_tile * j + r, 1), :] for r in range(blocks_per_tile)]
        picked = sel_rows[-1]
        for r in range(blocks_per_tile - 2, -1, -1):
            picked = jnp.where(krow < (r + 1) * SLC_LEN, sel_rows[r], picked)
        return jnp.where(dist >= 0, picked, 0.0)

    def win_keepf(j):
        dist = (i - j) * t + qcol - krow
        return jnp.where(dist >= 0, jnp.where(dist < WINDOW, 1.0, 0.0), 0.0)

    run_branch(slc, counts_ref[step], slc_tile, slc_keepf, ks_ref, vst_ref)
    first = jnp.maximum(i - WINDOW // t, 0)
    run_branch(win, i + 1 - first, lambda n: first + n, win_keepf, kw_ref, vwt_ref)

    gates = jax.nn.sigmoid(gt_ref[0])
    inv_s = 1.0 / acc_ref[slc, NSA_DK:NSA_DK + 1]
    inv_w = 1.0 / acc_ref[win, NSA_DK:NSA_DK + 1]
    for pair in range(hpg // 2):
        halves = []
        for h in (2 * pair, 2 * pair + 1):
            hs = slice(h * t, (h + 1) * t)
            halves.append(gates[3 * h + 1:3 * h + 2, :] * (acc_ref[slc, :NSA_DK, hs] * inv_s[:, hs])
                          + gates[3 * h + 2:3 * h + 3, :] * (acc_ref[win, :NSA_DK, hs] * inv_w[:, hs]))
        ps = slice(pair * LANES, (pair + 1) * LANES)
        o_ref[:, ps] = (oc_ref[:, ps].astype(F32) + jnp.concatenate(halves, axis=0).T).astype(BF16)


def _nsa_attn(counts, words, q, selt, oc, gt, ksw, vt, batch, seq):
    t = NSA_TILE
    nq = seq // t
    n = q.shape[0]
    words_per_step = words.shape[0] // (batch * NSA_GROUPS * nq)
    rows = lambda c: pl.BlockSpec((t, c), lambda b, g, i, *_: (b * nq + i, g))
    key = lambda which: pl.BlockSpec((seq, LANES), lambda b, g, i, *_: (b, which * NSA_GROUPS + g))
    vt4 = vt.reshape(batch, nq, _VT_OUT, t)
    vrows = NSA_DK + ONES_ROWS
    val = lambda which: pl.BlockSpec((1, nq, vrows, t), lambda b, g, i, *_: (b, 0, which * NSA_GROUPS + g, 0))
    grid_spec = pltpu.PrefetchScalarGridSpec(
        num_scalar_prefetch=2,
        grid=(batch, NSA_GROUPS, nq),
        in_specs=[rows(NSA_HPG * LANES),
                  pl.BlockSpec((1, 1, NS_PAD, t), lambda b, g, i, *_: (b, g, 0, i)),
                  rows(NSA_HPG * NSA_DK),
                  pl.BlockSpec((1, N_GATES, t), lambda b, g, i, *_: (b * nq + i, g, 0)),
                  key(0), key(1), val(0), val(1)],
        out_specs=rows(NSA_HPG * NSA_DK),
        scratch_shapes=[pltpu.VMEM((NSA_HPG * t, LANES), BF16), pltpu.VMEM((2, 1, NSA_HPG * t), F32),
                        pltpu.VMEM((2, vrows, NSA_HPG * t), F32),
                        pltpu.VMEM((2, 2 * t, NSA_HPG * t), F32)],
    )
    return pl.pallas_call(
        functools.partial(_nsa_attn_kernel, nq=nq, words_per_step=words_per_step),
        grid_spec=grid_spec,
        out_shape=jax.ShapeDtypeStruct((n, NSA_HEADS * NSA_DK), BF16),
        compiler_params=_params(3),
        name="nsa_attn",
    )(counts, words, q, selt, oc, gt, ksw, ksw, vt4, vt4)


def _outproj_kernel(x_ref, om_ref, on_ref, wm_ref, wn_ref, o_ref):
    o_ref[...] = x_ref[...] + _dot(om_ref[...], wm_ref[...]) + _dot(on_ref[...], wn_ref[...])


def _out_proj(x2, o_mla, o_nsa, wm, wn, tm=512):
    n, d = x2.shape
    row = lambda c: pl.BlockSpec((tm, c), lambda i: (i, 0))
    return pl.pallas_call(
        _outproj_kernel,
        grid=(n // tm,),
        in_specs=[row(d), row(o_mla.shape[1]), row(o_nsa.shape[1]), _resident(wm.shape), _resident(wn.shape)],
        out_specs=row(d),
        out_shape=jax.ShapeDtypeStruct((n, d), F32),
        compiler_params=_params(1),
        name="out_proj",
    )(x2, o_mla, o_nsa, wm, wn)


def _ffn_kernel(x_ref, g_ref, wg_ref, wu_ref, wd_ref, gf_ref, o_ref, h_scr, acc_scr):
    f = pl.program_id(1)

    @pl.when(f == 0)
    def _():
        h_scr[...] = _rmsnorm(x_ref[...], g_ref[...]).astype(BF16)
        acc_scr[...] = jnp.zeros_like(acc_scr)

    h = h_scr[...]
    gate = _dot(h, wg_ref[...])
    act = (gate * jax.nn.sigmoid(gate)) * _dot(h, wu_ref[...])
    acc_scr[...] += _dot(act.astype(BF16), wd_ref[...])

    @pl.when(f == pl.num_programs(1) - 1)
    def _():
        o_ref[...] = _rmsnorm(x_ref[...] + acc_scr[...], gf_ref[...])


def _ffn(x1, g, wg, wu, wd, gf, tm=512, tf=512):
    n, d = x1.shape
    dff = wg.shape[1]
    return pl.pallas_call(
        _ffn_kernel,
        grid=(n // tm, dff // tf),
        in_specs=[pl.BlockSpec((tm, d), lambda i, f: (i, 0)), _resident((1, d)),
                  pl.BlockSpec((d, tf), lambda i, f: (0, f)), pl.BlockSpec((d, tf), lambda i, f: (0, f)),
                  pl.BlockSpec((tf, d), lambda i, f: (f, 0)), _resident((1, d))],
        out_specs=pl.BlockSpec((tm, d), lambda i, f: (i, 0)),
        out_shape=jax.ShapeDtypeStruct((n, d), F32),
        scratch_shapes=[pltpu.VMEM((tm, d), BF16), pltpu.VMEM((tm, d), F32)],
        compiler_params=_params(2),
        name="ffn",
    )(x1, g, wg, wu, wd, gf)


def _pad_cols(w, width):
    return jnp.pad(w, ((0, 0), (0, width - w.shape[1])))


def _rot_cols(w):
    half = w.shape[1] // 2
    return jnp.concatenate([-w[:, half:], w[:, :half]], axis=1)


def _fused_in_weights(w_in):
    sizes = (MLA_Q_LORA, MLA_KV_LORA, MLA_ROPE, NSA_HEADS * NSA_DK) + (NSA_GROUPS * NSA_DK,) * 6 + (3 * NSA_HEADS,)
    offs = np.cumsum(sizes)[:-1].tolist()
    cq, ckv, kr, q, kc, vc, ks, vs, kw, vw, gate = jnp.split(w_in, offs, axis=1)
    d = w_in.shape[0]
    q_pad = _pad_cols((q * (NSA_DK ** -0.5 * LOG2E)).reshape(d * NSA_HEADS, NSA_DK), LANES)
    q_pad = q_pad.reshape(d, NSA_HEADS * LANES)
    per_group = lambda w, c: _pad_cols(w.reshape(d * NSA_GROUPS, c), LANES).reshape(d, NSA_GROUPS * LANES)
    kr_rot = _rot_cols(kr)
    cols = [cq, ckv, kr, kr, kr_rot, kr_rot, q_pad, kc, vc, per_group(ks, NSA_DK), per_group(kw, NSA_DK)]
    w = jnp.concatenate(cols, axis=1).astype(BF16)
    assert w.shape[1] == _IN_COLS["ksw"][1]
    rows_t = jnp.concatenate([vs, vw, gate], axis=1).T
    pad = -rows_t.shape[0] % 16
    return w, jnp.pad(rows_t, ((0, pad), (0, 0))).astype(BF16)


def _slope_features(slopes2):
    s1 = slopes2.astype(BF16).astype(F32)
    s2 = (slopes2 - s1).astype(BF16).astype(F32)
    s3 = (slopes2 - s1 - s2).astype(BF16).astype(F32)
    pieces = jnp.stack([s1, s2, s3, s1, s2, s3], axis=1)
    return jnp.pad(pieces, ((0, 0), (NSA_DK, LANES - NSA_DK - 6))).reshape(1, -1)


def _position_features(pos):
    hi = (POS_SPLIT * (pos // POS_SPLIT)).astype(F32)
    lo = (pos % POS_SPLIT).astype(F32)
    return jnp.pad(jnp.stack([hi, hi, hi, lo, lo, lo], axis=1), ((0, 0), (NSA_DK, LANES - NSA_DK - 6)))


def _fused_uq_weight(w_uq):
    d = w_uq.shape[0]
    w = w_uq.reshape(d, MLA_HEADS, MLA_NOPE + MLA_ROPE)
    nope = w[:, :, :MLA_NOPE].reshape(d, MLA_HEADS * MLA_NOPE)
    rope = w[:, :, MLA_NOPE:]
    rope_rot = jnp.concatenate([-rope[:, :, MLA_ROPE // 2:], rope[:, :, :MLA_ROPE // 2]], axis=2)
    flat = lambda r: r.reshape(d, MLA_HEADS * MLA_ROPE)
    return jnp.concatenate([nope, flat(rope), flat(rope_rot)], axis=1).astype(BF16)


def _rope_tables(seq):
    inv = ROPE_THETA ** (-jnp.arange(0, MLA_ROPE, 2, dtype=F32) / MLA_ROPE)
    ang = jnp.arange(seq, dtype=F32)[:, None] * inv[None, :]
    reps = 2 * LANES // MLA_ROPE
    return jnp.tile(jnp.cos(ang), (1, reps)), jnp.tile(jnp.sin(ang), (1, reps))


def _overlap_matrix(seq, n_cmp, ncp):
    cmp_start = CMP_STRIDE * np.arange(n_cmp)
    slc_start = SLC_LEN * np.arange(seq // SLC_LEN)
    ov = np.clip(np.minimum(cmp_start[:, None] + CMP_LEN, slc_start[None, :] + SLC_LEN)
                 - np.maximum(cmp_start[:, None], slc_start[None, :]), 0, None).astype(np.float32) / CMP_STRIDE
    out = np.zeros((NS_PAD, ncp), np.float32)
    out[:ov.shape[1], :n_cmp] = ov.T
    return jnp.asarray(out, BF16)


def _active_tiles(flags, batch, seq):
    t_blocks = NSA_TILE // SLC_LEN
    nq = seq // NSA_TILE
    steps = flags.shape[0]
    tiles = flags.reshape(steps, NS_PAD // t_blocks, t_blocks).max(axis=-1)
    tile_id = jnp.arange(tiles.shape[1], dtype=jnp.int32)[None, :]
    q_tile = (jnp.arange(steps, dtype=jnp.int32) % nq)[:, None]
    active = (tiles > 0) & (tile_id <= q_tile)
    order = jnp.argsort(jnp.where(active, tile_id, tile_id + tiles.shape[1]), axis=-1).astype(jnp.int32)
    shifts = 8 * jnp.arange(IDS_PER_WORD, dtype=jnp.int32)
    words = (order.reshape(steps, -1, IDS_PER_WORD) << shifts).sum(axis=-1).astype(jnp.int32)
    return active.sum(axis=-1).astype(jnp.int32), words.reshape(-1)


def kernel(x, attn_norm_g, w_in, mla_q_norm_g, mla_kv_norm_g, w_uq, w_uk, w_uv, cmp_pos_k, cmp_pos_v, w_cmp_k1,
           w_cmp_k2, w_cmp_v1, w_cmp_v2, w_o, ffn_norm_g, w_gate, w_up, w_down, final_norm_g):
    batch, seq, d = x.shape
    n = batch * seq
    assert w_in.shape[0] == 1, "the final RMSNorm is fused into the FFN kernel of a single layer"
    assert seq % (CMP_STRIDE * NSA_TILE) == 0 and seq // SLC_LEN <= NS_PAD and seq // POS_SPLIT <= 256
    n_cmp = (seq - CMP_LEN) // CMP_STRIDE + 1
    ncp = seq // CMP_STRIDE
    cos_t, sin_t = _rope_tables(seq)
    ov_t = _overlap_matrix(seq, n_cmp, ncp)
    slopes2 = 2.0 ** (-8.0 * jnp.arange(1, NSA_HEADS + 1, dtype=F32) / NSA_HEADS) * LOG2E
    qfeat = _slope_features(slopes2)
    kfeat = _position_features(jnp.arange(seq, dtype=jnp.int32))
    x2 = x.reshape(n, d)

    w_fused, w_fused_t = _fused_in_weights(w_in[0])
    cq, ckv, kr, q, kvc, ksw, vt, gt = _in_proj(
        x2, attn_norm_g[0][None], w_fused, w_fused_t, qfeat, cos_t, sin_t, kfeat, seq)

    qn, qr, kn, vt_mla = _mla_up(cq, ckv, mla_q_norm_g[0][None], mla_kv_norm_g[0][None], _fused_uq_weight(w_uq[0]),
                                 w_uk[0].astype(BF16), w_uv[0].T.astype(BF16), cos_t, sin_t, seq)
    o_mla = _mla_attn(qn, qr, kn, kr, vt_mla, batch, seq)

    r = kvc.reshape(batch, seq, 2, NSA_GROUPS, NSA_DK).transpose(2, 0, 3, 1, 4)
    r = r.reshape(2, batch, NSA_GROUPS, ncp, CMP_STRIDE * NSA_DK)
    pos = jnp.stack([cmp_pos_k[0].reshape(1, -1), cmp_pos_v[0].reshape(1, -1)])
    w1 = jnp.stack([w_cmp_k1[0], w_cmp_v1[0]]).astype(BF16)
    w2 = jnp.stack([_pad_cols(w_cmp_k2[0], LANES), _pad_cols(w_cmp_v2[0], LANES)]).astype(BF16)
    w2t = jnp.stack([w_cmp_k2[0].T, w_cmp_v2[0].T]).astype(BF16)
    cmp_end = CMP_STRIDE * jnp.arange(ncp, dtype=jnp.int32) + (CMP_LEN - 1)
    cmp_rows, cmp_t = _compress(r, pos, w1, w2, w2t, _position_features(cmp_end))
    ones = jnp.zeros((ONES_ROWS, ncp), BF16).at[0].set(1.0)
    const_rows = jnp.broadcast_to(jnp.concatenate([ov_t, ones]), (batch, NSA_GROUPS, NS_PAD + ONES_ROWS, ncp))
    vaug = jnp.concatenate([cmp_t[1], const_rows], axis=2)
    vaug = vaug.reshape(batch, NSA_GROUPS, vaug.shape[2], ncp // NSA_TILE, NSA_TILE).transpose(0, 1, 3, 2, 4)

    oc, selt, flags = _nsa_cmp(q, cmp_rows, vaug, gt, batch, seq, n_cmp)
    counts, words = _active_tiles(flags.reshape(flags.shape[0], NS_PAD), batch, seq)
    o_nsa = _nsa_attn(counts, words, q, selt, oc, gt, ksw, vt, batch, seq)

    split = MLA_HEADS * MLA_V
    x1 = _out_proj(x2, o_mla, o_nsa, w_o[0][:split].astype(BF16), w_o[0][split:].astype(BF16))
    out = _ffn(x1, ffn_norm_g[0][None], w_gate[0].astype(BF16), w_up[0].astype(BF16), w_down[0].astype(BF16),
               final_norm_g[None])
    return out.reshape(batch, seq, d)
```

```python
import functools
import math

import numpy as np
import jax
import jax.numpy as jnp
from jax import lax
from jax.experimental import pallas as pl
from jax.experimental.pallas import tpu as pltpu

F32 = jnp.float32
BF16 = jnp.bfloat16

EPS = 1e-6
NEG = -1e30
LOG2E = math.log2(math.e)
LANES = 128

MLA_HEADS = 8
MLA_Q_LORA = 512
MLA_KV_LORA = 256
MLA_NOPE = 128
MLA_ROPE = 64
MLA_V = 128
ROPE_THETA = 10000.0
MLA_TK = 512

NSA_HEADS = 16
NSA_GROUPS = 2
NSA_HPG = NSA_HEADS // NSA_GROUPS
NSA_DK = 64
CMP_LEN = 32
CMP_STRIDE = 16
CMP_HIDDEN = 128
SLC_LEN = 64
SLC_TOPK = 16
WINDOW = 512
FORCE_SCORE = 1e4
NSA_TILE = 128
NS_PAD = 128
N_GATES = 3 * NSA_HPG
POS_SPLIT = 64
ONES_ROWS = 16

VMEM_LIMIT = 56 * 1024 * 1024


def _params(n_axes):
    return pltpu.CompilerParams(dimension_semantics=("arbitrary",) * n_axes, vmem_limit_bytes=VMEM_LIMIT)


def _resident(shape):
    zeros = (0,) * len(shape)
    return pl.BlockSpec(shape, lambda *_: zeros, pipeline_mode=pl.Buffered(1))


def _rmsnorm(x, g):
    return x * lax.rsqrt(jnp.mean(x * x, axis=-1, keepdims=True) + EPS) * g


def _dot(a, b):
    return jnp.dot(a, b, preferred_element_type=F32)


def _dot_nt(a, b):
    return lax.dot_general(a, b, (((1,), (1,)), ((), ())), preferred_element_type=F32)


def _ones_rows(width):
    return jnp.where(lax.broadcasted_iota(jnp.int32, (ONES_ROWS, width), 0) == 0, 1.0, 0.0).astype(BF16)


def _flash_weights(s, m_ref, idx):
    m_prev = m_ref[idx]
    m_new = jnp.maximum(m_prev, jnp.max(s, axis=0, keepdims=True))
    m_ref[idx] = m_new
    return jnp.exp2(m_prev - m_new), jnp.exp2(s - m_new).astype(BF16)


def _stack_heads_transposed(q_ref, qt_scr):
    t = q_ref.shape[0]
    for h in range(q_ref.shape[1] // LANES):
        qt_scr[:, h * t:(h + 1) * t] = q_ref[:, h * LANES:(h + 1) * LANES].astype(F32).T.astype(BF16)


def _flash_update(s, v_t, m_ref, acc_ref, idx):
    alpha, p = _flash_weights(s, m_ref, idx)
    acc_ref[idx] = alpha * acc_ref[idx] + _dot(v_t, p)


def _skewed_flash(count, tile_at, keepf_fn, k_tile, vt_tile, qt_scr, s_scr, m_ref, acc_ref, idx):
    pairs = (count + 1) // 2

    def pair(u):
        return [tile_at(jnp.minimum(2 * u + r, count - 1)) for r in range(2)]

    def scores(u, slot):
        j0, j1 = pair(jnp.minimum(u, pairs - 1))
        s_scr[slot] = _dot(jnp.concatenate([k_tile(j0), k_tile(j1)], axis=0), qt_scr[...])

    def consume(u, slot):
        j0, j1 = pair(u)
        second = jnp.where(2 * u + 1 < count, 1.0, 0.0)
        bias = jnp.concatenate([jnp.where(keepf_fn(j0) > 0.5, 0.0, NEG),
                                jnp.where(keepf_fn(j1) * second > 0.5, 0.0, NEG)], axis=0)
        s = s_scr[slot]
        s = s + jnp.concatenate([bias] * (s.shape[1] // bias.shape[1]), axis=1)
        _flash_update(s, jnp.concatenate([vt_tile(j0), vt_tile(j1)], axis=1), m_ref, acc_ref, idx)

    def body(w, carry):
        u = 2 * w
        scores(u + 1, 1)
        consume(u, 0)
        scores(u + 2, 0)
        consume(u + 1, 1)
        return carry

    scores(0, 0)
    lax.fori_loop(0, pairs // 2, body, 0)

    @pl.when(pairs % 2 == 1)
    def _():
        consume(pairs - 1, 0)


_IN_COLS = dict(cq=(0, 512), ckv=(512, 768), kra=(768, 896), krb=(896, 1024), q=(1024, 3072),
                kvc=(3072, 3328), ksw=(3328, 3840))
_VT_ROWS = 2 * NSA_GROUPS * NSA_DK
_VT_OUT = _VT_ROWS // NSA_DK * (NSA_DK + ONES_ROWS)


def _inproj_kernel(x_ref, g_ref, w_ref, wt_ref, qfeat_ref, cos_ref, sin_ref, kfeat_ref, cq_ref, ckv_ref, kr_ref,
                   q_ref, kvc_ref, ksw_ref, vt_ref, gt_ref):
    h = _rmsnorm(x_ref[...], g_ref[...]).astype(BF16)

    def mm(name):
        lo, hi = _IN_COLS[name]
        return _dot(h, w_ref[:, lo:hi])

    cq_ref[...] = mm("cq")
    ckv_ref[...] = mm("ckv")
    kr_ref[...] = (mm("kra") * cos_ref[...] + mm("krb") * sin_ref[...]).astype(BF16)
    q_ref[...] = (mm("q") + qfeat_ref[...]).astype(BF16)
    kvc_ref[...] = mm("kvc")
    kfeat = kfeat_ref[...]
    ksw_ref[...] = (mm("ksw") + jnp.concatenate([kfeat] * (2 * NSA_GROUPS), axis=1)).astype(BF16)
    t = _dot_nt(wt_ref[...], h)
    ones = _ones_rows(NSA_TILE)
    for c in range(vt_ref.shape[0]):
        cs = slice(c * NSA_TILE, (c + 1) * NSA_TILE)
        pieces = []
        for k in range(_VT_ROWS // NSA_DK):
            pieces += [t[k * NSA_DK:(k + 1) * NSA_DK, cs].astype(BF16), ones]
        vt_ref[c] = jnp.concatenate(pieces, axis=0)
        gt_ref[c] = t[_VT_ROWS:_VT_ROWS + NSA_GROUPS * N_GATES, cs]


def _in_proj(x2, g, w, wt, qfeat, cos_t, sin_t, kfeat, seq, tm=256):
    n, d = x2.shape
    nt = seq // tm
    widths = {k: hi - lo for k, (lo, hi) in _IN_COLS.items()}
    row = lambda c: pl.BlockSpec((tm, c), lambda i: (i, 0))
    tab = pl.BlockSpec((tm, LANES), lambda i: (i % nt, 0))
    outs = [("cq", F32), ("ckv", F32), ("kra", BF16), ("q", BF16), ("kvc", F32), ("ksw", BF16)]
    tiles = tm // NSA_TILE
    tspec = lambda rows: pl.BlockSpec((tiles, rows, NSA_TILE), lambda i: (i, 0, 0))
    return pl.pallas_call(
        _inproj_kernel,
        grid=(n // tm,),
        in_specs=[row(d), _resident((1, d)), _resident(w.shape), _resident(wt.shape), _resident(qfeat.shape),
                  tab, tab, tab],
        out_specs=[row(widths[k]) for k, _ in outs] + [tspec(_VT_OUT), tspec(NSA_GROUPS * N_GATES)],
        out_shape=[jax.ShapeDtypeStruct((n, widths[k]), dt) for k, dt in outs] + [
            jax.ShapeDtypeStruct((n // NSA_TILE, _VT_OUT, NSA_TILE), BF16),
            jax.ShapeDtypeStruct((n // NSA_TILE, NSA_GROUPS * N_GATES, NSA_TILE), F32)],
        compiler_params=_params(1),
        name="in_proj",
    )(x2, g, w, wt, qfeat, cos_t, sin_t, kfeat)


def _mlaup_kernel(cq_ref, ckv_ref, gq_ref, gkv_ref, wq_ref, wk_ref, wvt_ref, cos_ref, sin_ref, qn_ref, qr_ref,
                  kn_ref, vt_ref, *, scale):
    cqn = _rmsnorm(cq_ref[...], gq_ref[...]).astype(BF16)
    ckvn = _rmsnorm(ckv_ref[...], gkv_ref[...]).astype(BF16)
    hn = MLA_HEADS * MLA_NOPE
    hr = MLA_HEADS * MLA_ROPE
    qn_ref[...] = (_dot(cqn, wq_ref[:, :hn]) * scale).astype(BF16)
    a = _dot(cqn, wq_ref[:, hn:hn + hr])
    b = _dot(cqn, wq_ref[:, hn + hr:])
    cos = cos_ref[...]
    sin = sin_ref[...]
    for p in range(hr // LANES):
        sl = slice(p * LANES, (p + 1) * LANES)
        qr_ref[:, sl] = ((a[:, sl] * cos + b[:, sl] * sin) * scale).astype(BF16)
    kn_ref[...] = _dot(ckvn, wk_ref[...]).astype(BF16)
    v_t = _dot_nt(wvt_ref[...], ckvn).astype(BF16)
    ones = _ones_rows(MLA_TK)
    for c in range(vt_ref.shape[0]):
        pieces = []
        for h in range(MLA_HEADS):
            pieces += [v_t[h * MLA_V:(h + 1) * MLA_V, c * MLA_TK:(c + 1) * MLA_TK], ones]
        vt_ref[c] = jnp.concatenate(pieces, axis=0)


def _mla_up(cq, ckv, gq, gkv, wq, wk, wvt, cos_t, sin_t, seq, tm=512):
    n = cq.shape[0]
    nt = seq // tm
    hn = MLA_HEADS * MLA_NOPE
    hr = MLA_HEADS * MLA_ROPE
    hv = MLA_HEADS * (MLA_V + ONES_ROWS)
    row = lambda c: pl.BlockSpec((tm, c), lambda i: (i, 0))
    tab = pl.BlockSpec((tm, LANES), lambda i: (i % nt, 0))
    scale = (MLA_NOPE + MLA_ROPE) ** -0.5 * LOG2E
    return pl.pallas_call(
        functools.partial(_mlaup_kernel, scale=scale),
        grid=(n // tm,),
        in_specs=[row(MLA_Q_LORA), row(MLA_KV_LORA), _resident(gq.shape), _resident(gkv.shape),
                  _resident(wq.shape), _resident(wk.shape), _resident(wvt.shape), tab, tab],
        out_specs=[row(hn), row(hr), row(hn), pl.BlockSpec((tm // MLA_TK, hv, MLA_TK), lambda i: (i, 0, 0))],
        out_shape=[jax.ShapeDtypeStruct((n, hn), BF16), jax.ShapeDtypeStruct((n, hr), BF16),
                   jax.ShapeDtypeStruct((n, hn), BF16), jax.ShapeDtypeStruct((n // MLA_TK, hv, MLA_TK), BF16)],
        compiler_params=_params(1),
        name="mla_up",
    )(cq, ckv, gq, gkv, wq, wk, wvt, cos_t, sin_t)


def _mla_attn_kernel(qn_ref, qr_ref, kn_ref, kr_ref, vt_ref, o_ref, m_ref, acc_ref, s_scr, qt_scr, *, tq):
    i = pl.program_id(2)
    tk = MLA_TK
    lane = lax.broadcasted_iota(jnp.int32, (tq, LANES), 1)
    krow = lax.broadcasted_iota(jnp.int32, (tk, tq), 0)
    qcol = lax.broadcasted_iota(jnp.int32, (tk, tq), 1)
    qr = qr_ref[...]
    zero = jnp.zeros_like(qr)
    for hh in range(2):
        q_cat = jnp.concatenate([qn_ref[:, hh * LANES:(hh + 1) * LANES],
                                 jnp.where((lane < MLA_ROPE) == (hh == 0), qr, zero)], axis=1)
        qt_scr[hh] = q_cat.astype(F32).T.astype(BF16)
    m_ref[...] = jnp.full_like(m_ref, NEG)
    acc_ref[...] = jnp.zeros_like(acc_ref)
    vrows = MLA_V + ONES_ROWS

    def scores(j, slot):
        ks = pl.ds(pl.multiple_of(j * tk, tk), tk)
        k_rope = kr_ref[ks, :]
        for hh in range(2):
            hs = slice(hh * LANES, (hh + 1) * LANES)
            s_scr[slot, hh] = _dot(jnp.concatenate([kn_ref[ks, hs], k_rope], axis=1), qt_scr[hh])

    def consume(j, slot, diagonal):
        weights = []
        for hh in range(2):
            s = s_scr[slot, hh]
            if diagonal:
                s = jnp.where(j * tk + krow <= i * tq + qcol, s, NEG)
            weights.append(_flash_weights(s, m_ref, hh))
        for hh in range(2):
            alpha, p = weights[hh]
            acc_ref[hh] = alpha * acc_ref[hh] + _dot(vt_ref[0, j, hh * vrows:(hh + 1) * vrows, :], p)

    def body(u, carry):
        j = 2 * u
        scores(j + 1, 1)
        consume(j, 0, False)
        scores(j + 2, 0)
        consume(j + 1, 1, False)
        return carry

    assert tq == tk, "the last key tile of a query tile must be its only diagonal tile"
    scores(0, 0)
    lax.fori_loop(0, i // 2, body, 0)

    @pl.when(i % 2 == 1)
    def _():
        scores(i, 1)
        consume(i - 1, 0, False)
        consume(i, 1, True)

    @pl.when(i % 2 == 0)
    def _():
        consume(i, 0, True)

    for hh in range(2):
        o_t = acc_ref[hh, :MLA_V] / acc_ref[hh, MLA_V:MLA_V + 1]
        for c in range(tq // LANES):
            o_ref[c * LANES:(c + 1) * LANES, hh * LANES:(hh + 1) * LANES] = (
                o_t[:, c * LANES:(c + 1) * LANES].T.astype(BF16))


def _mla_attn(qn, qr, kn, kr, vt, batch, seq, tq=512):
    n = qn.shape[0]
    nq = seq // tq
    nk = seq // MLA_TK
    pairs = MLA_HEADS // 2
    pw = 2 * LANES
    qspec = lambda c: pl.BlockSpec((tq, c), lambda b, p, i: (b * nq + i, p))
    vrows = MLA_V + ONES_ROWS
    vt4 = vt.reshape(batch, nk, MLA_HEADS * vrows, MLA_TK)
    return pl.pallas_call(
        functools.partial(_mla_attn_kernel, tq=tq),
        grid=(batch, pairs, nq),
        in_specs=[qspec(pw), qspec(LANES), pl.BlockSpec((seq, pw), lambda b, p, i: (b, p)),
                  pl.BlockSpec((seq, LANES), lambda b, p, i: (b, 0)),
                  pl.BlockSpec((1, nk, 2 * vrows, MLA_TK), lambda b, p, i: (b, 0, p, 0))],
        out_specs=qspec(pw),
        out_shape=jax.ShapeDtypeStruct((n, MLA_HEADS * MLA_V), BF16),
        scratch_shapes=[pltpu.VMEM((2, 1, tq), F32), pltpu.VMEM((2, vrows, tq), F32),
                        pltpu.VMEM((2, 2, MLA_TK, tq), F32), pltpu.VMEM((2, 2 * LANES, tq), BF16)],
        compiler_params=_params(3),
        name="mla_attn",
    )(qn, qr, kn, kr, vt4)


def _compress_kernel(r_ref, pos_ref, w1_ref, w2_ref, w2t_ref, feat_ref, o_ref, ot_ref):
    half = CMP_STRIDE * NSA_DK
    r = r_ref[0, 0, 0]
    pos = pos_ref[0]
    a = _dot((r + pos[:, :half]).astype(BF16), w1_ref[0, :half, :])
    b = _dot((r + pos[:, half:]).astype(BF16), w1_ref[0, half:, :])
    hid = a + pltpu.roll(b, b.shape[0] - 1, 0)
    act = (hid * jax.nn.sigmoid(hid)).astype(BF16)
    o_ref[0, 0, 0] = (_dot(act, w2_ref[0]) + feat_ref[...]).astype(BF16)
    ot_ref[0, 0, 0] = _dot_nt(w2t_ref[0], act).astype(BF16)


def _compress(r, pos, w1, w2, w2t, feat):
    _, batch, groups, nc, width = r.shape
    return pl.pallas_call(
        _compress_kernel,
        grid=(2, batch, groups),
        in_specs=[pl.BlockSpec((1, 1, 1, nc, width), lambda t, b, g: (t, b, g, 0, 0)),
                  pl.BlockSpec((1, 1, 2 * width), lambda t, b, g: (t, 0, 0)),
                  pl.BlockSpec((1, 2 * width, CMP_HIDDEN), lambda t, b, g: (t, 0, 0)),
                  pl.BlockSpec((1, CMP_HIDDEN, LANES), lambda t, b, g: (t, 0, 0)),
                  pl.BlockSpec((1, NSA_DK, CMP_HIDDEN), lambda t, b, g: (t, 0, 0)),
                  pl.BlockSpec((nc, LANES), lambda t, b, g: (0, 0))],
        out_specs=[pl.BlockSpec((1, 1, 1, nc, LANES), lambda t, b, g: (t, b, g, 0, 0)),
                   pl.BlockSpec((1, 1, 1, NSA_DK, nc), lambda t, b, g: (t, b, g, 0, 0))],
        out_shape=[jax.ShapeDtypeStruct((2, batch, groups, nc, LANES), BF16),
                   jax.ShapeDtypeStruct((2, batch, groups, NSA_DK, nc), BF16)],
        compiler_params=_params(3),
        name="compress",
    )(r, pos, w1, w2, w2t, feat)


def _nsa_cmp_kernel(q_ref, kc_ref, vaug_ref, gt_ref, oc_ref, selt_ref, flag_ref, q_scr, m_ref, acc_ref, s_scr,
                    *, n_cmp):
    i = pl.program_id(2)
    t = NSA_TILE
    hpg = NSA_HPG
    _stack_heads_transposed(q_ref, q_scr)
    m_ref[...] = jnp.full_like(m_ref, NEG)
    acc_ref[...] = jnp.zeros_like(acc_ref)

    crow = lax.broadcasted_iota(jnp.int32, (t, t), 0)
    qcol = lax.broadcasted_iota(jnp.int32, (t, t), 1)
    tq = i * t + qcol

    def keepf(j):
        c = j * t + crow
        return jnp.where(CMP_STRIDE * c + (CMP_LEN - 1) <= tq, jnp.where(c < n_cmp, 1.0, 0.0), 0.0)

    count = jnp.minimum((i * t + t - CMP_LEN) // (CMP_STRIDE * t) + 1, kc_ref.shape[3] // t)
    _skewed_flash(count, lambda n: n, keepf, lambda j: kc_ref[0, 0, 0, pl.ds(pl.multiple_of(j * t, t), t), :],
                  lambda j: vaug_ref[0, 0, j], q_scr, s_scr, m_ref, acc_ref, 0)

    some = jnp.where(tq[:1] >= CMP_LEN - 1, 1.0, 0.0)
    inv = jnp.concatenate([some] * hpg, axis=1) / acc_ref[0, NSA_DK + NS_PAD:NSA_DK + NS_PAD + 1]
    gates = jax.nn.sigmoid(gt_ref[0])
    imp = jnp.zeros((NS_PAD, t), F32)
    for pair in range(hpg // 2):
        halves = []
        for h in (2 * pair, 2 * pair + 1):
            hs = slice(h * t, (h + 1) * t)
            halves.append(gates[3 * h:3 * h + 1, :] * (acc_ref[0, :NSA_DK, hs] * inv[:, hs]))
            imp = imp + acc_ref[0, NSA_DK:NSA_DK + NS_PAD, hs] * inv[:, hs]
        oc_ref[:, pair * LANES:(pair + 1) * LANES] = jnp.concatenate(halves, axis=0).T.astype(BF16)

    blk = lax.broadcasted_iota(jnp.int32, (NS_PAD, t), 0)
    blk_t = (i * t + lax.broadcasted_iota(jnp.int32, (NS_PAD, t), 1)) // SLC_LEN
    forced = jnp.where(blk == 0, 1.0, 0.0) + jnp.where(blk == blk_t, 1.0, 0.0) + jnp.where(blk == blk_t - 1, 1.0, 0.0)
    imp = jnp.where(forced > 0.5, FORCE_SCORE, jnp.where(blk <= blk_t, imp, -1.0))
    blkf = blk.astype(F32)
    sel = jnp.zeros((NS_PAD, t), F32)
    for _ in range(SLC_TOPK):
        best = jnp.max(imp, axis=0, keepdims=True)
        pick = jnp.min(jnp.where(imp == best, blkf, float(NS_PAD)), axis=0, keepdims=True)
        hit = blkf == pick
        sel = jnp.where(hit, 1.0, sel)
        imp = jnp.where(hit, -3e38, imp)
    selt_ref[0, 0] = sel
    flag_ref[0] = jnp.max(sel, axis=1, keepdims=True).astype(jnp.int32)


def _nsa_cmp(q, kc, vaug, gt, batch, seq, n_cmp):
    t = NSA_TILE
    nq = seq // t
    n = q.shape[0]
    ncp = kc.shape[3]
    rows = lambda c: pl.BlockSpec((t, c), lambda b, g, i: (b * nq + i, g))
    return pl.pallas_call(
        functools.partial(_nsa_cmp_kernel, n_cmp=n_cmp),
        grid=(batch, NSA_GROUPS, nq),
        in_specs=[rows(NSA_HPG * LANES),
                  pl.BlockSpec((1, 1, 1, ncp, LANES), lambda b, g, i: (0, b, g, 0, 0)),
                  pl.BlockSpec((1, 1) + vaug.shape[2:], lambda b, g, i: (b, g, 0, 0, 0)),
                  pl.BlockSpec((1, N_GATES, t), lambda b, g, i: (b * nq + i, g, 0))],
        out_specs=[rows(NSA_HPG * NSA_DK),
                   pl.BlockSpec((1, 1, NS_PAD, t), lambda b, g, i: (b, g, 0, i)),
                   pl.BlockSpec((1, NS_PAD, 1), lambda b, g, i: ((b * NSA_GROUPS + g) * nq + i, 0, 0))],
        out_shape=[jax.ShapeDtypeStruct((n, NSA_HEADS * NSA_DK), BF16),
                   jax.ShapeDtypeStruct((batch, NSA_GROUPS, NS_PAD, seq), F32),
                   jax.ShapeDtypeStruct((batch * NSA_GROUPS * nq, NS_PAD, 1), jnp.int32)],
        scratch_shapes=[pltpu.VMEM((LANES, NSA_HPG * t), BF16), pltpu.VMEM((1, 1, NSA_HPG * t), F32),
                        pltpu.VMEM((1, NSA_DK + NS_PAD + ONES_ROWS, NSA_HPG * t), F32),
                        pltpu.VMEM((2, 2 * t, NSA_HPG * t), F32)],
        compiler_params=_params(3),
        name="nsa_cmp",
    )(q, kc, vaug, gt)


IDS_PER_WORD = 4


def _nsa_attn_kernel(counts_ref, words_ref, q_ref, selt_ref, oc_ref, gt_ref, ks_ref, kw_ref, vst_ref, vwt_ref, o_ref,
                     q_scr, m_ref, acc_ref, s_scr, *, nq, words_per_step):
    b = pl.program_id(0)
    g = pl.program_id(1)
    i = pl.program_id(2)
    t = NSA_TILE
    hpg = NSA_HPG
    _stack_heads_transposed(q_ref, q_scr)
    m_ref[...] = jnp.full_like(m_ref, NEG)
    acc_ref[...] = jnp.zeros_like(acc_ref)

    krow = lax.broadcasted_iota(jnp.int32, (t, t), 0)
    qcol = lax.broadcasted_iota(jnp.int32, (t, t), 1)
    blocks_per_tile = t // SLC_LEN
    slc, win = 0, 1
    step = (b * NSA_GROUPS + g) * nq + i

    def run_branch(branch, count, tile_at, keepf_fn, k_ref, vt_ref):
        _skewed_flash(count, tile_at, keepf_fn, lambda j: k_ref[pl.ds(pl.multiple_of(j * t, t), t), :],
                      lambda j: vt_ref[0, j], q_scr, s_scr, m_ref, acc_ref, branch)

    def slc_tile(n):
        word = words_ref[step * words_per_step + n // IDS_PER_WORD]
        return lax.shift_right_logical(word, 8 * (n % IDS_PER_WORD)) & 255

    def slc_keepf(j):
        dist = (i - j) * t + qcol - krow
        sel_rows = [selt_ref[0, 0, pl.ds(blocks_per_tile * j + r, 1), :] for r in range(blocks_per_tile)]
        picked = sel_rows[-1]
        for r in range(blocks_per_tile - 2, -1, -1):
            picked = jnp.where(krow < (r + 1) * SLC_LEN, sel_rows[r], picked)
        return jnp.where(dist >= 0, picked, 0.0)

    def win_keepf(j):
        dist = (i - j) * t + qcol - krow
        return jnp.where(dist >= 0, jnp.where(dist < WINDOW, 1.0, 0.0), 0.0)

    run_branch(slc, counts_ref[step], slc_tile, slc_keepf, ks_ref, vst_ref)
    first = jnp.maximum(i - WINDOW // t, 0)
    run_branch(win, i + 1 - first, lambda n: first + n, win_keepf, kw_ref, vwt_ref)

    gates = jax.nn.sigmoid(gt_ref[0])
    inv_s = 1.0 / acc_ref[slc, NSA_DK:NSA_DK + 1]
    inv_w = 1.0 / acc_ref[win, NSA_DK:NSA_DK + 1]
    for pair in range(hpg // 2):
        halves = []
        for h in (2 * pair, 2 * pair + 1):
            hs = slice(h * t, (h + 1) * t)
            halves.append(gates[3 * h + 1:3 * h + 2, :] * (acc_ref[slc, :NSA_DK, hs] * inv_s[:, hs])
                          + gates[3 * h + 2:3 * h + 3, :] * (acc_ref[win, :NSA_DK, hs] * inv_w[:, hs]))
        ps = slice(pair * LANES, (pair + 1) * LANES)
        o_ref[:, ps] = (oc_ref[:, ps].astype(F32) + jnp.concatenate(halves, axis=0).T).astype(BF16)


def _nsa_attn(counts, words, q, selt, oc, gt, ksw, vt, batch, seq):
    t = NSA_TILE
    nq = seq // t
    n = q.shape[0]
    words_per_step = words.shape[0] // (batch * NSA_GROUPS * nq)
    rows = lambda c: pl.BlockSpec((t, c), lambda b, g, i, *_: (b * nq + i, g))
    key = lambda which: pl.BlockSpec((seq, LANES), lambda b, g, i, *_: (b, which * NSA_GROUPS + g))
    vt4 = vt.reshape(batch, nq, _VT_OUT, t)
    vrows = NSA_DK + ONES_ROWS
    val = lambda which: pl.BlockSpec((1, nq, vrows, t), lambda b, g, i, *_: (b, 0, which * NSA_GROUPS + g, 0))
    grid_spec = pltpu.PrefetchScalarGridSpec(
        num_scalar_prefetch=2,
        grid=(batch, NSA_GROUPS, nq),
        in_specs=[rows(NSA_HPG * LANES),
                  pl.BlockSpec((1, 1, NS_PAD, t), lambda b, g, i, *_: (b, g, 0, i)),
                  rows(NSA_HPG * NSA_DK),
                  pl.BlockSpec((1, N_GATES, t), lambda b, g, i, *_: (b * nq + i, g, 0)),
                  key(0), key(1), val(0), val(1)],
        out_specs=rows(NSA_HPG * NSA_DK),
        scratch_shapes=[pltpu.VMEM((LANES, NSA_HPG * t), BF16), pltpu.VMEM((2, 1, NSA_HPG * t), F32),
                        pltpu.VMEM((2, vrows, NSA_HPG * t), F32),
                        pltpu.VMEM((2, 2 * t, NSA_HPG * t), F32)],
    )
    return pl.pallas_call(
        functools.partial(_nsa_attn_kernel, nq=nq, words_per_step=words_per_step),
        grid_spec=grid_spec,
        out_shape=jax.ShapeDtypeStruct((n, NSA_HEADS * NSA_DK), BF16),
        compiler_params=_params(3),
        name="nsa_attn",
    )(counts, words, q, selt, oc, gt, ksw, ksw, vt4, vt4)


def _outproj_kernel(x_ref, om_ref, on_ref, wm_ref, wn_ref, o_ref):
    o_ref[...] = x_ref[...] + _dot(om_ref[...], wm_ref[...]) + _dot(on_ref[...], wn_ref[...])


def _out_proj(x2, o_mla, o_nsa, wm, wn, tm=512):
    n, d = x2.shape
    row = lambda c: pl.BlockSpec((tm, c), lambda i: (i, 0))
    return pl.pallas_call(
        _outproj_kernel,
        grid=(n // tm,),
        in_specs=[row(d), row(o_mla.shape[1]), row(o_nsa.shape[1]), _resident(wm.shape), _resident(wn.shape)],
        out_specs=row(d),
        out_shape=jax.ShapeDtypeStruct((n, d), F32),
        compiler_params=_params(1),
        name="out_proj",
    )(x2, o_mla, o_nsa, wm, wn)


def _ffn_kernel(x_ref, g_ref, wg_ref, wu_ref, wd_ref, gf_ref, o_ref, h_scr, acc_scr):
    f = pl.program_id(1)

    @pl.when(f == 0)
    def _():
        h_scr[...] = _rmsnorm(x_ref[...], g_ref[...]).astype(BF16)
        acc_scr[...] = jnp.zeros_like(acc_scr)

    h = h_scr[...]
    gate = _dot(h, wg_ref[...])
    act = (gate * jax.nn.sigmoid(gate)) * _dot(h, wu_ref[...])
    acc_scr[...] += _dot(act.astype(BF16), wd_ref[...])

    @pl.when(f == pl.num_programs(1) - 1)
    def _():
        o_ref[...] = _rmsnorm(x_ref[...] + acc_scr[...], gf_ref[...])


def _ffn(x1, g, wg, wu, wd, gf, tm=512, tf=512):
    n, d = x1.shape
    dff = wg.shape[1]
    return pl.pallas_call(
        _ffn_kernel,
        grid=(n // tm, dff // tf),
        in_specs=[pl.BlockSpec((tm, d), lambda i, f: (i, 0)), _resident((1, d)),
                  pl.BlockSpec((d, tf), lambda i, f: (0, f)), pl.BlockSpec((d, tf), lambda i, f: (0, f)),
                  pl.BlockSpec((tf, d), lambda i, f: (f, 0)), _resident((1, d))],
        out_specs=pl.BlockSpec((tm, d), lambda i, f: (i, 0)),
        out_shape=jax.ShapeDtypeStruct((n, d), F32),
        scratch_shapes=[pltpu.VMEM((tm, d), BF16), pltpu.VMEM((tm, d), F32)],
        compiler_params=_params(2),
        name="ffn",
    )(x1, g, wg, wu, wd, gf)


def _pad_cols(w, width):
    return jnp.pad(w, ((0, 0), (0, width - w.shape[1])))


def _rot_cols(w):
    half = w.shape[1] // 2
    return jnp.concatenate([-w[:, half:], w[:, :half]], axis=1)


def _fused_in_weights(w_in):
    sizes = (MLA_Q_LORA, MLA_KV_LORA, MLA_ROPE, NSA_HEADS * NSA_DK) + (NSA_GROUPS * NSA_DK,) * 6 + (3 * NSA_HEADS,)
    offs = np.cumsum(sizes)[:-1].tolist()
    cq, ckv, kr, q, kc, vc, ks, vs, kw, vw, gate = jnp.split(w_in, offs, axis=1)
    d = w_in.shape[0]
    q_pad = _pad_cols((q * (NSA_DK ** -0.5 * LOG2E)).reshape(d * NSA_HEADS, NSA_DK), LANES)
    q_pad = q_pad.reshape(d, NSA_HEADS * LANES)
    per_group = lambda w, c: _pad_cols(w.reshape(d * NSA_GROUPS, c), LANES).reshape(d, NSA_GROUPS * LANES)
    kr_rot = _rot_cols(kr)
    cols = [cq, ckv, kr, kr, kr_rot, kr_rot, q_pad, kc, vc, per_group(ks, NSA_DK), per_group(kw, NSA_DK)]
    w = jnp.concatenate(cols, axis=1).astype(BF16)
    assert w.shape[1] == _IN_COLS["ksw"][1]
    rows_t = jnp.concatenate([vs, vw, gate], axis=1).T
    pad = -rows_t.shape[0] % 16
    return w, jnp.pad(rows_t, ((0, pad), (0, 0))).astype(BF16)


def _slope_features(slopes2):
    s1 = slopes2.astype(BF16).astype(F32)
    s2 = (slopes2 - s1).astype(BF16).astype(F32)
    s3 = (slopes2 - s1 - s2).astype(BF16).astype(F32)
    pieces = jnp.stack([s1, s2, s3, s1, s2, s3], axis=1)
    return jnp.pad(pieces, ((0, 0), (NSA_DK, LANES - NSA_DK - 6))).reshape(1, -1)


def _position_features(pos):
    hi = (POS_SPLIT * (pos // POS_SPLIT)).astype(F32)
    lo = (pos % POS_SPLIT).astype(F32)
    return jnp.pad(jnp.stack([hi, hi, hi, lo, lo, lo], axis=1), ((0, 0), (NSA_DK, LANES - NSA_DK - 6)))


def _fused_uq_weight(w_uq):
    d = w_uq.shape[0]
    w = w_uq.reshape(d, MLA_HEADS, MLA_NOPE + MLA_ROPE)
    nope = w[:, :, :MLA_NOPE].reshape(d, MLA_HEADS * MLA_NOPE)
    rope = w[:, :, MLA_NOPE:]
    rope_rot = jnp.concatenate([-rope[:, :, MLA_ROPE // 2:], rope[:, :, :MLA_ROPE // 2]], axis=2)
    flat = lambda r: r.reshape(d, MLA_HEADS * MLA_ROPE)
    return jnp.concatenate([nope, flat(rope), flat(rope_rot)], axis=1).astype(BF16)


def _rope_tables(seq):
    inv = ROPE_THETA ** (-jnp.arange(0, MLA_ROPE, 2, dtype=F32) / MLA_ROPE)
    ang = jnp.arange(seq, dtype=F32)[:, None] * inv[None, :]
    reps = 2 * LANES // MLA_ROPE
    return jnp.tile(jnp.cos(ang), (1, reps)), jnp.tile(jnp.sin(ang), (1, reps))


def _overlap_matrix(seq, n_cmp, ncp):
    cmp_start = CMP_STRIDE * np.arange(n_cmp)
    slc_start = SLC_LEN * np.arange(seq // SLC_LEN)
    ov = np.clip(np.minimum(cmp_start[:, None] + CMP_LEN, slc_start[None, :] + SLC_LEN)
                 - np.maximum(cmp_start[:, None], slc_start[None, :]), 0, None).astype(np.float32) / CMP_STRIDE
    out = np.zeros((NS_PAD, ncp), np.float32)
    out[:ov.shape[1], :n_cmp] = ov.T
    return jnp.asarray(out, BF16)


def _active_tiles(flags, batch, seq):
    t_blocks = NSA_TILE // SLC_LEN
    nq = seq // NSA_TILE
    steps = flags.shape[0]
    tiles = flags.reshape(steps, NS_PAD // t_blocks, t_blocks).max(axis=-1)
    tile_id = jnp.arange(tiles.shape[1], dtype=jnp.int32)[None, :]
    q_tile = (jnp.arange(steps, dtype=jnp.int32) % nq)[:, None]
    active = (tiles > 0) & (tile_id <= q_tile)
    order = jnp.argsort(jnp.where(active, tile_id, tile_id + tiles.shape[1]), axis=-1).astype(jnp.int32)
    shifts = 8 * jnp.arange(IDS_PER_WORD, dtype=jnp.int32)
    words = (order.reshape(steps, -1, IDS_PER_WORD) << shifts).sum(axis=-1).astype(jnp.int32)
    return active.sum(axis=-1).astype(jnp.int32), words.reshape(-1)


def kernel(x, attn_norm_g, w_in, mla_q_norm_g, mla_kv_norm_g, w_uq, w_uk, w_uv, cmp_pos_k, cmp_pos_v, w_cmp_k1,
           w_cmp_k2, w_cmp_v1, w_cmp_v2, w_o, ffn_norm_g, w_gate, w_up, w_down, final_norm_g):
    batch, seq, d = x.shape
    n = batch * seq
    assert w_in.shape[0] == 1, "the final RMSNorm is fused into the FFN kernel of a single layer"
    assert seq % (CMP_STRIDE * NSA_TILE) == 0 and seq // SLC_LEN <= NS_PAD and seq // POS_SPLIT <= 256
    n_cmp = (seq - CMP_LEN) // CMP_STRIDE + 1
    ncp = seq // CMP_STRIDE
    cos_t, sin_t = _rope_tables(seq)
    ov_t = _overlap_matrix(seq, n_cmp, ncp)
    slopes2 = 2.0 ** (-8.0 * jnp.arange(1, NSA_HEADS + 1, dtype=F32) / NSA_HEADS) * LOG2E
    qfeat = _slope_features(slopes2)
    kfeat = _position_features(jnp.arange(seq, dtype=jnp.int32))
    x2 = x.reshape(n, d)

    w_fused, w_fused_t = _fused_in_weights(w_in[0])
    cq, ckv, kr, q, kvc, ksw, vt, gt = _in_proj(
        x2, attn_norm_g[0][None], w_fused, w_fused_t, qfeat, cos_t, sin_t, kfeat, seq)

    qn, qr, kn, vt_mla = _mla_up(cq, ckv, mla_q_norm_g[0][None], mla_kv_norm_g[0][None], _fused_uq_weight(w_uq[0]),
                                 w_uk[0].astype(BF16), w_uv[0].T.astype(BF16), cos_t, sin_t, seq)
    o_mla = _mla_attn(qn, qr, kn, kr, vt_mla, batch, seq)

    r = kvc.reshape(batch, seq, 2, NSA_GROUPS, NSA_DK).transpose(2, 0, 3, 1, 4)
    r = r.reshape(2, batch, NSA_GROUPS, ncp, CMP_STRIDE * NSA_DK)
    pos = jnp.stack([cmp_pos_k[0].reshape(1, -1), cmp_pos_v[0].reshape(1, -1)])
    w1 = jnp.stack([w_cmp_k1[0], w_cmp_v1[0]]).astype(BF16)
    w2 = jnp.stack([_pad_cols(w_cmp_k2[0], LANES), _pad_cols(w_cmp_v2[0], LANES)]).astype(BF16)
    w2t = jnp.stack([w_cmp_k2[0].T, w_cmp_v2[0].T]).astype(BF16)
    cmp_end = CMP_STRIDE * jnp.arange(ncp, dtype=jnp.int32) + (CMP_LEN - 1)
    cmp_rows, cmp_t = _compress(r, pos, w1, w2, w2t, _position_features(cmp_end))
    ones = jnp.zeros((ONES_ROWS, ncp), BF16).at[0].set(1.0)
    const_rows = jnp.broadcast_to(jnp.concatenate([ov_t, ones]), (batch, NSA_GROUPS, NS_PAD + ONES_ROWS, ncp))
    vaug = jnp.concatenate([cmp_t[1], const_rows], axis=2)
    vaug = vaug.reshape(batch, NSA_GROUPS, vaug.shape[2], ncp // NSA_TILE, NSA_TILE).transpose(0, 1, 3, 2, 4)

    oc, selt, flags = _nsa_cmp(q, cmp_rows, vaug, gt, batch, seq, n_cmp)
    counts, words = _active_tiles(flags.reshape(flags.shape[0], NS_PAD), batch, seq)
    o_nsa = _nsa_attn(counts, words, q, selt, oc, gt, ksw, vt, batch, seq)

    split = MLA_HEADS * MLA_V
    x1 = _out_proj(x2, o_mla, o_nsa, w_o[0][:split].astype(BF16), w_o[0][split:].astype(BF16))
    out = _ffn(x1, ffn_norm_g[0][None], w_gate[0].astype(BF16), w_up[0].astype(BF16), w_down[0].astype(BF16),
               final_norm_g[None])
    return out.reshape(batch, seq, d)
```

```python
import functools
import math

import numpy as np
import jax
import jax.numpy as jnp
from jax import lax
from jax.experimental import pallas as pl
from jax.experimental.pallas import tpu as pltpu

F32 = jnp.float32
BF16 = jnp.bfloat16

EPS = 1e-6
NEG = -1e30
LOG2E = math.log2(math.e)
LANES = 128

MLA_HEADS = 8
MLA_Q_LORA = 512
MLA_KV_LORA = 256
MLA_NOPE = 128
MLA_ROPE = 64
MLA_V = 128
ROPE_THETA = 10000.0
MLA_TK = 512

NSA_HEADS = 16
NSA_GROUPS = 2
NSA_HPG = NSA_HEADS // NSA_GROUPS
NSA_DK = 64
CMP_LEN = 32
CMP_STRIDE = 16
CMP_HIDDEN = 128
SLC_LEN = 64
SLC_TOPK = 16
WINDOW = 512
FORCE_SCORE = 1e4
NSA_TILE = 128
NS_PAD = 128
CMP_TQ = 256
N_GATES = 3 * NSA_HPG
POS_SPLIT = 64
ONES_ROWS = 16

VMEM_LIMIT = 56 * 1024 * 1024


def _params(n_axes):
    return pltpu.CompilerParams(dimension_semantics=("arbitrary",) * n_axes, vmem_limit_bytes=VMEM_LIMIT)


def _resident(shape):
    zeros = (0,) * len(shape)
    return pl.BlockSpec(shape, lambda *_: zeros, pipeline_mode=pl.Buffered(1))


def _rmsnorm(x, g):
    return x * lax.rsqrt(jnp.mean(x * x, axis=-1, keepdims=True) + EPS) * g


def _dot(a, b):
    return jnp.dot(a, b, preferred_element_type=F32)


def _dot_nt(a, b):
    return lax.dot_general(a, b, (((1,), (1,)), ((), ())), preferred_element_type=F32)


def _ones_rows(width):
    return jnp.where(lax.broadcasted_iota(jnp.int32, (ONES_ROWS, width), 0) == 0, 1.0, 0.0).astype(BF16)


def _flash_weights(s, m_ref, idx):
    m_prev = m_ref[idx]
    m_new = jnp.maximum(m_prev, jnp.max(s, axis=0, keepdims=True))
    m_ref[idx] = m_new
    return jnp.exp2(m_prev - m_new), jnp.exp2(s - m_new).astype(BF16)


def _stack_heads_transposed(q_ref, qt_scr):
    t = q_ref.shape[0]
    for h in range(q_ref.shape[1] // LANES):
        qt_scr[:, h * t:(h + 1) * t] = q_ref[:, h * LANES:(h + 1) * LANES].astype(F32).T.astype(BF16)


def _flash_update(s, v_t, m_ref, acc_ref, idx):
    alpha, p = _flash_weights(s, m_ref, idx)
    acc_ref[idx] = alpha * acc_ref[idx] + _dot(v_t, p)


def _skewed_flash(count, tile_at, keepf_fn, k_tile, vt_tile, qt_scr, s_scr, m_ref, acc_ref, idx):
    pairs = (count + 1) // 2

    def pair(u):
        return [tile_at(jnp.minimum(2 * u + r, count - 1)) for r in range(2)]

    def scores(u, slot):
        j0, j1 = pair(jnp.minimum(u, pairs - 1))
        s_scr[slot] = _dot(jnp.concatenate([k_tile(j0), k_tile(j1)], axis=0), qt_scr[...])

    def consume(u, slot):
        j0, j1 = pair(u)
        second = jnp.where(2 * u + 1 < count, 1.0, 0.0)
        bias = jnp.concatenate([jnp.where(keepf_fn(j0) > 0.5, 0.0, NEG),
                                jnp.where(keepf_fn(j1) * second > 0.5, 0.0, NEG)], axis=0)
        s = s_scr[slot]
        s = s + jnp.concatenate([bias] * (s.shape[1] // bias.shape[1]), axis=1)
        _flash_update(s, jnp.concatenate([vt_tile(j0), vt_tile(j1)], axis=1), m_ref, acc_ref, idx)

    def body(w, carry):
        u = 2 * w
        scores(u + 1, 1)
        consume(u, 0)
        scores(u + 2, 0)
        consume(u + 1, 1)
        return carry

    scores(0, 0)
    lax.fori_loop(0, pairs // 2, body, 0)

    @pl.when(pairs % 2 == 1)
    def _():
        consume(pairs - 1, 0)


_IN_COLS = dict(cq=(0, 512), ckv=(512, 768), kra=(768, 896), krb=(896, 1024), q=(1024, 3072),
                kvc=(3072, 3328), ksw=(3328, 3840))
_VT_ROWS = 2 * NSA_GROUPS * NSA_DK
_VT_OUT = _VT_ROWS // NSA_DK * (NSA_DK + ONES_ROWS)


def _inproj_kernel(x_ref, g_ref, w_ref, wt_ref, qfeat_ref, cos_ref, sin_ref, kfeat_ref, cq_ref, ckv_ref, kr_ref,
                   q_ref, kvc_ref, ksw_ref, vt_ref, gt_ref):
    h = _rmsnorm(x_ref[...], g_ref[...]).astype(BF16)

    def mm(name):
        lo, hi = _IN_COLS[name]
        return _dot(h, w_ref[:, lo:hi])

    cq_ref[...] = mm("cq")
    ckv_ref[...] = mm("ckv")
    kr_ref[...] = (mm("kra") * cos_ref[...] + mm("krb") * sin_ref[...]).astype(BF16)
    q_ref[...] = (mm("q") + qfeat_ref[...]).astype(BF16)
    kvc_ref[...] = mm("kvc")
    kfeat = kfeat_ref[...]
    ksw_ref[...] = (mm("ksw") + jnp.concatenate([kfeat] * (2 * NSA_GROUPS), axis=1)).astype(BF16)
    t = _dot_nt(wt_ref[...], h)
    ones = _ones_rows(NSA_TILE)
    for c in range(vt_ref.shape[0]):
        cs = slice(c * NSA_TILE, (c + 1) * NSA_TILE)
        pieces = []
        for k in range(_VT_ROWS // NSA_DK):
            pieces += [t[k * NSA_DK:(k + 1) * NSA_DK, cs].astype(BF16), ones]
        vt_ref[c] = jnp.concatenate(pieces, axis=0)
        gt_ref[c] = t[_VT_ROWS:_VT_ROWS + NSA_GROUPS * N_GATES, cs]


def _in_proj(x2, g, w, wt, qfeat, cos_t, sin_t, kfeat, seq, tm=256):
    n, d = x2.shape
    nt = seq // tm
    widths = {k: hi - lo for k, (lo, hi) in _IN_COLS.items()}
    row = lambda c: pl.BlockSpec((tm, c), lambda i: (i, 0))
    tab = pl.BlockSpec((tm, LANES), lambda i: (i % nt, 0))
    outs = [("cq", F32), ("ckv", F32), ("kra", BF16), ("q", BF16), ("kvc", F32), ("ksw", BF16)]
    tiles = tm // NSA_TILE
    tspec = lambda rows: pl.BlockSpec((tiles, rows, NSA_TILE), lambda i: (i, 0, 0))
    return pl.pallas_call(
        _inproj_kernel,
        grid=(n // tm,),
        in_specs=[row(d), _resident((1, d)), _resident(w.shape), _resident(wt.shape), _resident(qfeat.shape),
                  tab, tab, tab],
        out_specs=[row(widths[k]) for k, _ in outs] + [tspec(_VT_OUT), tspec(NSA_GROUPS * N_GATES)],
        out_shape=[jax.ShapeDtypeStruct((n, widths[k]), dt) for k, dt in outs] + [
            jax.ShapeDtypeStruct((n // NSA_TILE, _VT_OUT, NSA_TILE), BF16),
            jax.ShapeDtypeStruct((n // NSA_TILE, NSA_GROUPS * N_GATES, NSA_TILE), F32)],
        compiler_params=_params(1),
        name="in_proj",
    )(x2, g, w, wt, qfeat, cos_t, sin_t, kfeat)


def _mlaup_kernel(cq_ref, ckv_ref, gq_ref, gkv_ref, wq_ref, wk_ref, wvt_ref, cos_ref, sin_ref, qn_ref, qr_ref,
                  kn_ref, vt_ref, *, scale):
    cqn = _rmsnorm(cq_ref[...], gq_ref[...]).astype(BF16)
    ckvn = _rmsnorm(ckv_ref[...], gkv_ref[...]).astype(BF16)
    hn = MLA_HEADS * MLA_NOPE
    hr = MLA_HEADS * MLA_ROPE
    qn_ref[...] = (_dot(cqn, wq_ref[:, :hn]) * scale).astype(BF16)
    a = _dot(cqn, wq_ref[:, hn:hn + hr])
    b = _dot(cqn, wq_ref[:, hn + hr:])
    cos = cos_ref[...]
    sin = sin_ref[...]
    for p in range(hr // LANES):
        sl = slice(p * LANES, (p + 1) * LANES)
        qr_ref[:, sl] = ((a[:, sl] * cos + b[:, sl] * sin) * scale).astype(BF16)
    kn_ref[...] = _dot(ckvn, wk_ref[...]).astype(BF16)
    v_t = _dot_nt(wvt_ref[...], ckvn).astype(BF16)
    ones = _ones_rows(MLA_TK)
    for c in range(vt_ref.shape[0]):
        pieces = []
        for h in range(MLA_HEADS):
            pieces += [v_t[h * MLA_V:(h + 1) * MLA_V, c * MLA_TK:(c + 1) * MLA_TK], ones]
        vt_ref[c] = jnp.concatenate(pieces, axis=0)


def _mla_up(cq, ckv, gq, gkv, wq, wk, wvt, cos_t, sin_t, seq, tm=512):
    n = cq.shape[0]
    nt = seq // tm
    hn = MLA_HEADS * MLA_NOPE
    hr = MLA_HEADS * MLA_ROPE
    hv = MLA_HEADS * (MLA_V + ONES_ROWS)
    row = lambda c: pl.BlockSpec((tm, c), lambda i: (i, 0))
    tab = pl.BlockSpec((tm, LANES), lambda i: (i % nt, 0))
    scale = (MLA_NOPE + MLA_ROPE) ** -0.5 * LOG2E
    return pl.pallas_call(
        functools.partial(_mlaup_kernel, scale=scale),
        grid=(n // tm,),
        in_specs=[row(MLA_Q_LORA), row(MLA_KV_LORA), _resident(gq.shape), _resident(gkv.shape),
                  _resident(wq.shape), _resident(wk.shape), _resident(wvt.shape), tab, tab],
        out_specs=[row(hn), row(hr), row(hn), pl.BlockSpec((tm // MLA_TK, hv, MLA_TK), lambda i: (i, 0, 0))],
        out_shape=[jax.ShapeDtypeStruct((n, hn), BF16), jax.ShapeDtypeStruct((n, hr), BF16),
                   jax.ShapeDtypeStruct((n, hn), BF16), jax.ShapeDtypeStruct((n // MLA_TK, hv, MLA_TK), BF16)],
        compiler_params=_params(1),
        name="mla_up",
    )(cq, ckv, gq, gkv, wq, wk, wvt, cos_t, sin_t)


def _mla_attn_kernel(qn_ref, qr_ref, kn_ref, kr_ref, vt_ref, o_ref, m_ref, acc_ref, s_scr, qt_scr, *, tq):
    i = pl.program_id(2)
    tk = MLA_TK
    lane = lax.broadcasted_iota(jnp.int32, (tq, LANES), 1)
    krow = lax.broadcasted_iota(jnp.int32, (tk, tq), 0)
    qcol = lax.broadcasted_iota(jnp.int32, (tk, tq), 1)
    qr = qr_ref[...]
    zero = jnp.zeros_like(qr)
    for hh in range(2):
        q_cat = jnp.concatenate([qn_ref[:, hh * LANES:(hh + 1) * LANES],
                                 jnp.where((lane < MLA_ROPE) == (hh == 0), qr, zero)], axis=1)
        qt_scr[hh] = q_cat.astype(F32).T.astype(BF16)
    m_ref[...] = jnp.full_like(m_ref, NEG)
    acc_ref[...] = jnp.zeros_like(acc_ref)
    vrows = MLA_V + ONES_ROWS

    def scores(j, slot):
        ks = pl.ds(pl.multiple_of(j * tk, tk), tk)
        k_rope = kr_ref[ks, :]
        for hh in range(2):
            hs = slice(hh * LANES, (hh + 1) * LANES)
            s_scr[slot, hh] = _dot(jnp.concatenate([kn_ref[ks, hs], k_rope], axis=1), qt_scr[hh])

    def consume(j, slot, diagonal):
        weights = []
        for hh in range(2):
            s = s_scr[slot, hh]
            if diagonal:
                s = jnp.where(j * tk + krow <= i * tq + qcol, s, NEG)
            weights.append(_flash_weights(s, m_ref, hh))
        for hh in range(2):
            alpha, p = weights[hh]
            acc_ref[hh] = alpha * acc_ref[hh] + _dot(vt_ref[0, j, hh * vrows:(hh + 1) * vrows, :], p)

    def body(u, carry):
        j = 2 * u
        scores(j + 1, 1)
        consume(j, 0, False)
        scores(j + 2, 0)
        consume(j + 1, 1, False)
        return carry

    assert tq == tk, "the last key tile of a query tile must be its only diagonal tile"
    scores(0, 0)
    lax.fori_loop(0, i // 2, body, 0)

    @pl.when(i % 2 == 1)
    def _():
        scores(i, 1)
        consume(i - 1, 0, False)
        consume(i, 1, True)

    @pl.when(i % 2 == 0)
    def _():
        consume(i, 0, True)

    for hh in range(2):
        o_t = acc_ref[hh, :MLA_V] / acc_ref[hh, MLA_V:MLA_V + 1]
        for c in range(tq // LANES):
            o_ref[c * LANES:(c + 1) * LANES, hh * LANES:(hh + 1) * LANES] = (
                o_t[:, c * LANES:(c + 1) * LANES].T.astype(BF16))


def _mla_attn(qn, qr, kn, kr, vt, batch, seq, tq=512):
    n = qn.shape[0]
    nq = seq // tq
    nk = seq // MLA_TK
    pairs = MLA_HEADS // 2
    pw = 2 * LANES
    qspec = lambda c: pl.BlockSpec((tq, c), lambda b, p, i: (b * nq + i, p))
    vrows = MLA_V + ONES_ROWS
    vt4 = vt.reshape(batch, nk, MLA_HEADS * vrows, MLA_TK)
    return pl.pallas_call(
        functools.partial(_mla_attn_kernel, tq=tq),
        grid=(batch, pairs, nq),
        in_specs=[qspec(pw), qspec(LANES), pl.BlockSpec((seq, pw), lambda b, p, i: (b, p)),
                  pl.BlockSpec((seq, LANES), lambda b, p, i: (b, 0)),
                  pl.BlockSpec((1, nk, 2 * vrows, MLA_TK), lambda b, p, i: (b, 0, p, 0))],
        out_specs=qspec(pw),
        out_shape=jax.ShapeDtypeStruct((n, MLA_HEADS * MLA_V), BF16),
        scratch_shapes=[pltpu.VMEM((2, 1, tq), F32), pltpu.VMEM((2, vrows, tq), F32),
                        pltpu.VMEM((2, 2, MLA_TK, tq), F32), pltpu.VMEM((2, 2 * LANES, tq), BF16)],
        compiler_params=_params(3),
        name="mla_attn",
    )(qn, qr, kn, kr, vt4)


def _compress_kernel(r_ref, pos_ref, w1_ref, w2_ref, w2t_ref, feat_ref, o_ref, ot_ref):
    half = CMP_STRIDE * NSA_DK
    r = r_ref[0, 0, 0]
    pos = pos_ref[0]
    a = _dot((r + pos[:, :half]).astype(BF16), w1_ref[0, :half, :])
    b = _dot((r + pos[:, half:]).astype(BF16), w1_ref[0, half:, :])
    hid = a + pltpu.roll(b, b.shape[0] - 1, 0)
    act = (hid * jax.nn.sigmoid(hid)).astype(BF16)
    o_ref[0, 0, 0] = (_dot(act, w2_ref[0]) + feat_ref[...]).astype(BF16)
    ot_ref[0, 0, 0] = _dot_nt(w2t_ref[0], act).astype(BF16)


def _compress(r, pos, w1, w2, w2t, feat):
    _, batch, groups, nc, width = r.shape
    return pl.pallas_call(
        _compress_kernel,
        grid=(2, batch, groups),
        in_specs=[pl.BlockSpec((1, 1, 1, nc, width), lambda t, b, g: (t, b, g, 0, 0)),
                  pl.BlockSpec((1, 1, 2 * width), lambda t, b, g: (t, 0, 0)),
                  pl.BlockSpec((1, 2 * width, CMP_HIDDEN), lambda t, b, g: (t, 0, 0)),
                  pl.BlockSpec((1, CMP_HIDDEN, LANES), lambda t, b, g: (t, 0, 0)),
                  pl.BlockSpec((1, NSA_DK, CMP_HIDDEN), lambda t, b, g: (t, 0, 0)),
                  pl.BlockSpec((nc, LANES), lambda t, b, g: (0, 0))],
        out_specs=[pl.BlockSpec((1, 1, 1, nc, LANES), lambda t, b, g: (t, b, g, 0, 0)),
                   pl.BlockSpec((1, 1, 1, NSA_DK, nc), lambda t, b, g: (t, b, g, 0, 0))],
        out_shape=[jax.ShapeDtypeStruct((2, batch, groups, nc, LANES), BF16),
                   jax.ShapeDtypeStruct((2, batch, groups, NSA_DK, nc), BF16)],
        compiler_params=_params(3),
        name="compress",
    )(r, pos, w1, w2, w2t, feat)


def _nsa_cmp_kernel(q_ref, kc_ref, vaug_ref, gt_ref, oc_ref, selt_ref, flag_ref, q_scr, m_ref, acc_ref, s_scr,
                    *, n_cmp):
    i = pl.program_id(2)
    t = NSA_TILE
    tq = CMP_TQ
    hpg = NSA_HPG
    _stack_heads_transposed(q_ref, q_scr)
    m_ref[...] = jnp.full_like(m_ref, NEG)
    acc_ref[...] = jnp.zeros_like(acc_ref)

    crow = lax.broadcasted_iota(jnp.int32, (t, tq), 0)
    qpos = i * tq + lax.broadcasted_iota(jnp.int32, (t, tq), 1)

    def keepf(j):
        c = j * t + crow
        return jnp.where(CMP_STRIDE * c + (CMP_LEN - 1) <= qpos, jnp.where(c < n_cmp, 1.0, 0.0), 0.0)

    count = jnp.minimum((i * tq + tq - CMP_LEN) // (CMP_STRIDE * t) + 1, kc_ref.shape[3] // t)
    _skewed_flash(count, lambda n: n, keepf, lambda j: kc_ref[0, 0, 0, pl.ds(pl.multiple_of(j * t, t), t), :],
                  lambda j: vaug_ref[0, 0, j], q_scr, s_scr, m_ref, acc_ref, 0)

    some = jnp.where(qpos[:1] >= CMP_LEN - 1, 1.0, 0.0)
    inv = jnp.concatenate([some] * hpg, axis=1) / acc_ref[0, NSA_DK + NS_PAD:NSA_DK + NS_PAD + 1]
    gates = jax.nn.sigmoid(jnp.concatenate([gt_ref[c] for c in range(tq // t)], axis=1))
    imp = jnp.zeros((NS_PAD, tq), F32)
    for pair in range(hpg // 2):
        halves = []
        for h in (2 * pair, 2 * pair + 1):
            hs = slice(h * tq, (h + 1) * tq)
            halves.append(gates[3 * h:3 * h + 1, :] * (acc_ref[0, :NSA_DK, hs] * inv[:, hs]))
            imp = imp + acc_ref[0, NSA_DK:NSA_DK + NS_PAD, hs] * inv[:, hs]
        oc_ref[:, pair * LANES:(pair + 1) * LANES] = jnp.concatenate(halves, axis=0).T.astype(BF16)

    blk = lax.broadcasted_iota(jnp.int32, (NS_PAD, tq), 0)
    blk_t = (i * tq + lax.broadcasted_iota(jnp.int32, (NS_PAD, tq), 1)) // SLC_LEN
    forced = jnp.where(blk == 0, 1.0, 0.0) + jnp.where(blk == blk_t, 1.0, 0.0) + jnp.where(blk == blk_t - 1, 1.0, 0.0)
    imp = jnp.where(forced > 0.5, FORCE_SCORE, jnp.where(blk <= blk_t, imp, -1.0))
    blkf = blk.astype(F32)
    sel = jnp.zeros((NS_PAD, tq), F32)
    for _ in range(SLC_TOPK):
        best = jnp.max(imp, axis=0, keepdims=True)
        pick = jnp.min(jnp.where(imp == best, blkf, float(NS_PAD)), axis=0, keepdims=True)
        hit = blkf == pick
        sel = jnp.where(hit, 1.0, sel)
        imp = jnp.where(hit, -3e38, imp)
    selt_ref[0, 0] = sel
    for c in range(tq // t):
        flag_ref[c] = jnp.max(sel[:, c * t:(c + 1) * t].T, axis=0, keepdims=True).astype(jnp.int32)


def _nsa_cmp(q, kc, vaug, gt, batch, seq, n_cmp):
    t = NSA_TILE
    tq = CMP_TQ
    nq = seq // tq
    n = q.shape[0]
    ncp = kc.shape[3]
    sub = tq // t
    rows = lambda c: pl.BlockSpec((tq, c), lambda b, g, i: (b * nq + i, g))
    return pl.pallas_call(
        functools.partial(_nsa_cmp_kernel, n_cmp=n_cmp),
        grid=(batch, NSA_GROUPS, nq),
        in_specs=[rows(NSA_HPG * LANES),
                  pl.BlockSpec((1, 1, 1, ncp, LANES), lambda b, g, i: (0, b, g, 0, 0)),
                  pl.BlockSpec((1, 1) + vaug.shape[2:], lambda b, g, i: (b, g, 0, 0, 0)),
                  pl.BlockSpec((sub, N_GATES, t), lambda b, g, i: (b * nq + i, g, 0))],
        out_specs=[rows(NSA_HPG * NSA_DK),
                   pl.BlockSpec((1, 1, NS_PAD, tq), lambda b, g, i: (b, g, 0, i)),
                   pl.BlockSpec((sub, 1, NS_PAD), lambda b, g, i: ((b * NSA_GROUPS + g) * nq + i, 0, 0))],
        out_shape=[jax.ShapeDtypeStruct((n, NSA_HEADS * NSA_DK), BF16),
                   jax.ShapeDtypeStruct((batch, NSA_GROUPS, NS_PAD, seq), F32),
                   jax.ShapeDtypeStruct((batch * NSA_GROUPS * seq // t, 1, NS_PAD), jnp.int32)],
        scratch_shapes=[pltpu.VMEM((LANES, NSA_HPG * tq), BF16), pltpu.VMEM((1, 1, NSA_HPG * tq), F32),
                        pltpu.VMEM((1, NSA_DK + NS_PAD + ONES_ROWS, NSA_HPG * tq), F32),
                        pltpu.VMEM((2, 2 * t, NSA_HPG * tq), F32)],
        compiler_params=_params(3),
        name="nsa_cmp",
    )(q, kc, vaug, gt)


IDS_PER_WORD = 4


def _nsa_attn_kernel(counts_ref, words_ref, q_ref, selt_ref, oc_ref, gt_ref, ks_ref, kw_ref, vst_ref, vwt_ref, o_ref,
                     q_scr, m_ref, acc_ref, s_scr, *, nq, words_per_step):
    b = pl.program_id(0)
    g = pl.program_id(1)
    i = pl.program_id(2)
    t = NSA_TILE
    hpg = NSA_HPG
    _stack_heads_transposed(q_ref, q_scr)
    m_ref[...] = jnp.full_like(m_ref, NEG)
    acc_ref[...] = jnp.zeros_like(acc_ref)

    krow = lax.broadcasted_iota(jnp.int32, (t, t), 0)
    qcol = lax.broadcasted_iota(jnp.int32, (t, t), 1)
    blocks_per_tile = t // SLC_LEN
    slc, win = 0, 1
    step = (b * NSA_GROUPS + g) * nq + i

    def run_branch(branch, count, tile_at, keepf_fn, k_ref, vt_ref):
        _skewed_flash(count, tile_at, keepf_fn, lambda j: k_ref[pl.ds(pl.multiple_of(j * t, t), t), :],
                      lambda j: vt_ref[0, j], q_scr, s_scr, m_ref, acc_ref, branch)

    def slc_tile(n):
        word = words_ref[step * words_per_step + n // IDS_PER_WORD]
        return lax.shift_right_logical(word, 8 * (n % IDS_PER_WORD)) & 255

    def slc_keepf(j):
        dist = (i - j) * t + qcol - krow
        sel_rows = [selt_ref[0, 0, pl.ds(blocks_per_tile * j + r, 1), :] for r in range(blocks_per_tile)]
        picked = sel_rows[-1]
        for r in range(blocks_per_tile - 2, -1, -1):
            picked = jnp.where(krow < (r + 1) * SLC_LEN, sel_rows[r], picked)
        return jnp.where(dist >= 0, picked, 0.0)

    def win_keepf(j):
        dist = (i - j) * t + qcol - krow
        return jnp.where(dist >= 0, jnp.where(dist < WINDOW, 1.0, 0.0), 0.0)

    run_branch(slc, counts_ref[step], slc_tile, slc_keepf, ks_ref, vst_ref)
    first = jnp.maximum(i - WINDOW // t, 0)
    run_branch(win, i + 1 - first, lambda n: first + n, win_keepf, kw_ref, vwt_ref)

    gates = jax.nn.sigmoid(gt_ref[0])
    inv_s = 1.0 / acc_ref[slc, NSA_DK:NSA_DK + 1]
    inv_w = 1.0 / acc_ref[win, NSA_DK:NSA_DK + 1]
    for pair in range(hpg // 2):
        halves = []
        for h in (2 * pair, 2 * pair + 1):
            hs = slice(h * t, (h + 1) * t)
            halves.append(gates[3 * h + 1:3 * h + 2, :] * (acc_ref[slc, :NSA_DK, hs] * inv_s[:, hs])
                          + gates[3 * h + 2:3 * h + 3, :] * (acc_ref[win, :NSA_DK, hs] * inv_w[:, hs]))
        ps = slice(pair * LANES, (pair + 1) * LANES)
        o_ref[:, ps] = (oc_ref[:, ps].astype(F32) + jnp.concatenate(halves, axis=0).T).astype(BF16)


def _nsa_attn(counts, words, q, selt, oc, gt, ksw, vt, batch, seq):
    t = NSA_TILE
    nq = seq // t
    n = q.shape[0]
    words_per_step = words.shape[0] // (batch * NSA_GROUPS * nq)
    rows = lambda c: pl.BlockSpec((t, c), lambda b, g, i, *_: (b * nq + i, g))
    key = lambda which: pl.BlockSpec((seq, LANES), lambda b, g, i, *_: (b, which * NSA_GROUPS + g))
    vt4 = vt.reshape(batch, nq, _VT_OUT, t)
    vrows = NSA_DK + ONES_ROWS
    val = lambda which: pl.BlockSpec((1, nq, vrows, t), lambda b, g, i, *_: (b, 0, which * NSA_GROUPS + g, 0))
    grid_spec = pltpu.PrefetchScalarGridSpec(
        num_scalar_prefetch=2,
        grid=(batch, NSA_GROUPS, nq),
        in_specs=[rows(NSA_HPG * LANES),
                  pl.BlockSpec((1, 1, NS_PAD, t), lambda b, g, i, *_: (b, g, 0, i)),
                  rows(NSA_HPG * NSA_DK),
                  pl.BlockSpec((1, N_GATES, t), lambda b, g, i, *_: (b * nq + i, g, 0)),
                  key(0), key(1), val(0), val(1)],
        out_specs=rows(NSA_HPG * NSA_DK),
        scratch_shapes=[pltpu.VMEM((LANES, NSA_HPG * t), BF16), pltpu.VMEM((2, 1, NSA_HPG * t), F32),
                        pltpu.VMEM((2, vrows, NSA_HPG * t), F32),
                        pltpu.VMEM((2, 2 * t, NSA_HPG * t), F32)],
    )
    return pl.pallas_call(
        functools.partial(_nsa_attn_kernel, nq=nq, words_per_step=words_per_step),
        grid_spec=grid_spec,
        out_shape=jax.ShapeDtypeStruct((n, NSA_HEADS * NSA_DK), BF16),
        compiler_params=_params(3),
        name="nsa_attn",
    )(counts, words, q, selt, oc, gt, ksw, ksw, vt4, vt4)


def _outproj_kernel(x_ref, om_ref, on_ref, wm_ref, wn_ref, o_ref):
    o_ref[...] = x_ref[...] + _dot(om_ref[...], wm_ref[...]) + _dot(on_ref[...], wn_ref[...])


def _out_proj(x2, o_mla, o_nsa, wm, wn, tm=512):
    n, d = x2.shape
    row = lambda c: pl.BlockSpec((tm, c), lambda i: (i, 0))
    return pl.pallas_call(
        _outproj_kernel,
        grid=(n // tm,),
        in_specs=[row(d), row(o_mla.shape[1]), row(o_nsa.shape[1]), _resident(wm.shape), _resident(wn.shape)],
        out_specs=row(d),
        out_shape=jax.ShapeDtypeStruct((n, d), F32),
        compiler_params=_params(1),
        name="out_proj",
    )(x2, o_mla, o_nsa, wm, wn)


def _ffn_kernel(x_ref, g_ref, wg_ref, wu_ref, wd_ref, gf_ref, o_ref, h_scr, acc_scr):
    f = pl.program_id(1)

    @pl.when(f == 0)
    def _():
        h_scr[...] = _rmsnorm(x_ref[...], g_ref[...]).astype(BF16)
        acc_scr[...] = jnp.zeros_like(acc_scr)

    h = h_scr[...]
    gate = _dot(h, wg_ref[...])
    act = (gate * jax.nn.sigmoid(gate)) * _dot(h, wu_ref[...])
    acc_scr[...] += _dot(act.astype(BF16), wd_ref[...])

    @pl.when(f == pl.num_programs(1) - 1)
    def _():
        o_ref[...] = _rmsnorm(x_ref[...] + acc_scr[...], gf_ref[...])


def _ffn(x1, g, wg, wu, wd, gf, tm=512, tf=512):
    n, d = x1.shape
    dff = wg.shape[1]
    return pl.pallas_call(
        _ffn_kernel,
        grid=(n // tm, dff // tf),
        in_specs=[pl.BlockSpec((tm, d), lambda i, f: (i, 0)), _resident((1, d)),
                  pl.BlockSpec((d, tf), lambda i, f: (0, f)), pl.BlockSpec((d, tf), lambda i, f: (0, f)),
                  pl.BlockSpec((tf, d), lambda i, f: (f, 0)), _resident((1, d))],
        out_specs=pl.BlockSpec((tm, d), lambda i, f: (i, 0)),
        out_shape=jax.ShapeDtypeStruct((n, d), F32),
        scratch_shapes=[pltpu.VMEM((tm, d), BF16), pltpu.VMEM((tm, d), F32)],
        compiler_params=_params(2),
        name="ffn",
    )(x1, g, wg, wu, wd, gf)


def _pad_cols(w, width):
    return jnp.pad(w, ((0, 0), (0, width - w.shape[1])))


def _rot_cols(w):
    half = w.shape[1] // 2
    return jnp.concatenate([-w[:, half:], w[:, :half]], axis=1)


def _fused_in_weights(w_in):
    sizes = (MLA_Q_LORA, MLA_KV_LORA, MLA_ROPE, NSA_HEADS * NSA_DK) + (NSA_GROUPS * NSA_DK,) * 6 + (3 * NSA_HEADS,)
    offs = np.cumsum(sizes)[:-1].tolist()
    cq, ckv, kr, q, kc, vc, ks, vs, kw, vw, gate = jnp.split(w_in, offs, axis=1)
    d = w_in.shape[0]
    q_pad = _pad_cols((q * (NSA_DK ** -0.5 * LOG2E)).reshape(d * NSA_HEADS, NSA_DK), LANES)
    q_pad = q_pad.reshape(d, NSA_HEADS * LANES)
    per_group = lambda w, c: _pad_cols(w.reshape(d * NSA_GROUPS, c), LANES).reshape(d, NSA_GROUPS * LANES)
    kr_rot = _rot_cols(kr)
    cols = [cq, ckv, kr, kr, kr_rot, kr_rot, q_pad, kc, vc, per_group(ks, NSA_DK), per_group(kw, NSA_DK)]
    w = jnp.concatenate(cols, axis=1).astype(BF16)
    assert w.shape[1] == _IN_COLS["ksw"][1]
    rows_t = jnp.concatenate([vs, vw, gate], axis=1).T
    pad = -rows_t.shape[0] % 16
    return w, jnp.pad(rows_t, ((0, pad), (0, 0))).astype(BF16)


def _slope_features(slopes2):
    s1 = slopes2.astype(BF16).astype(F32)
    s2 = (slopes2 - s1).astype(BF16).astype(F32)
    s3 = (slopes2 - s1 - s2).astype(BF16).astype(F32)
    pieces = jnp.stack([s1, s2, s3, s1, s2, s3], axis=1)
    return jnp.pad(pieces, ((0, 0), (NSA_DK, LANES - NSA_DK - 6))).reshape(1, -1)


def _position_features(pos):
    hi = (POS_SPLIT * (pos // POS_SPLIT)).astype(F32)
    lo = (pos % POS_SPLIT).astype(F32)
    return jnp.pad(jnp.stack([hi, hi, hi, lo, lo, lo], axis=1), ((0, 0), (NSA_DK, LANES - NSA_DK - 6)))


def _fused_uq_weight(w_uq):
    d = w_uq.shape[0]
    w = w_uq.reshape(d, MLA_HEADS, MLA_NOPE + MLA_ROPE)
    nope = w[:, :, :MLA_NOPE].reshape(d, MLA_HEADS * MLA_NOPE)
    rope = w[:, :, MLA_NOPE:]
    rope_rot = jnp.concatenate([-rope[:, :, MLA_ROPE // 2:], rope[:, :, :MLA_ROPE // 2]], axis=2)
    flat = lambda r: r.reshape(d, MLA_HEADS * MLA_ROPE)
    return jnp.concatenate([nope, flat(rope), flat(rope_rot)], axis=1).astype(BF16)


def _rope_tables(seq):
    inv = ROPE_THETA ** (-jnp.arange(0, MLA_ROPE, 2, dtype=F32) / MLA_ROPE)
    ang = jnp.arange(seq, dtype=F32)[:, None] * inv[None, :]
    reps = 2 * LANES // MLA_ROPE
    return jnp.tile(jnp.cos(ang), (1, reps)), jnp.tile(jnp.sin(ang), (1, reps))


def _overlap_matrix(seq, n_cmp, ncp):
    cmp_start = CMP_STRIDE * np.arange(n_cmp)
    slc_start = SLC_LEN * np.arange(seq // SLC_LEN)
    ov = np.clip(np.minimum(cmp_start[:, None] + CMP_LEN, slc_start[None, :] + SLC_LEN)
                 - np.maximum(cmp_start[:, None], slc_start[None, :]), 0, None).astype(np.float32) / CMP_STRIDE
    out = np.zeros((NS_PAD, ncp), np.float32)
    out[:ov.shape[1], :n_cmp] = ov.T
    return jnp.asarray(out, BF16)


def _active_tiles(flags, batch, seq):
    t_blocks = NSA_TILE // SLC_LEN
    nq = seq // NSA_TILE
    steps = flags.shape[0]
    tiles = flags.reshape(steps, NS_PAD // t_blocks, t_blocks).max(axis=-1)
    tile_id = jnp.arange(tiles.shape[1], dtype=jnp.int32)[None, :]
    q_tile = (jnp.arange(steps, dtype=jnp.int32) % nq)[:, None]
    active = (tiles > 0) & (tile_id <= q_tile)
    order = jnp.argsort(jnp.where(active, tile_id, tile_id + tiles.shape[1]), axis=-1).astype(jnp.int32)
    shifts = 8 * jnp.arange(IDS_PER_WORD, dtype=jnp.int32)
    words = (order.reshape(steps, -1, IDS_PER_WORD) << shifts).sum(axis=-1).astype(jnp.int32)
    return active.sum(axis=-1).astype(jnp.int32), words.reshape(-1)


def kernel(x, attn_norm_g, w_in, mla_q_norm_g, mla_kv_norm_g, w_uq, w_uk, w_uv, cmp_pos_k, cmp_pos_v, w_cmp_k1,
           w_cmp_k2, w_cmp_v1, w_cmp_v2, w_o, ffn_norm_g, w_gate, w_up, w_down, final_norm_g):
    batch, seq, d = x.shape
    n = batch * seq
    assert w_in.shape[0] == 1, "the final RMSNorm is fused into the FFN kernel of a single layer"
    assert seq % (CMP_STRIDE * NSA_TILE) == 0 and seq // SLC_LEN <= NS_PAD and seq // POS_SPLIT <= 256
    n_cmp = (seq - CMP_LEN) // CMP_STRIDE + 1
    ncp = seq // CMP_STRIDE
    cos_t, sin_t = _rope_tables(seq)
    ov_t = _overlap_matrix(seq, n_cmp, ncp)
    slopes2 = 2.0 ** (-8.0 * jnp.arange(1, NSA_HEADS + 1, dtype=F32) / NSA_HEADS) * LOG2E
    qfeat = _slope_features(slopes2)
    kfeat = _position_features(jnp.arange(seq, dtype=jnp.int32))
    x2 = x.reshape(n, d)

    w_fused, w_fused_t = _fused_in_weights(w_in[0])
    cq, ckv, kr, q, kvc, ksw, vt, gt = _in_proj(
        x2, attn_norm_g[0][None], w_fused, w_fused_t, qfeat, cos_t, sin_t, kfeat, seq)

    qn, qr, kn, vt_mla = _mla_up(cq, ckv, mla_q_norm_g[0][None], mla_kv_norm_g[0][None], _fused_uq_weight(w_uq[0]),
                                 w_uk[0].astype(BF16), w_uv[0].T.astype(BF16), cos_t, sin_t, seq)
    o_mla = _mla_attn(qn, qr, kn, kr, vt_mla, batch, seq)

    r = kvc.reshape(batch, seq, 2, NSA_GROUPS, NSA_DK).transpose(2, 0, 3, 1, 4)
    r = r.reshape(2, batch, NSA_GROUPS, ncp, CMP_STRIDE * NSA_DK)
    pos = jnp.stack([cmp_pos_k[0].reshape(1, -1), cmp_pos_v[0].reshape(1, -1)])
    w1 = jnp.stack([w_cmp_k1[0], w_cmp_v1[0]]).astype(BF16)
    w2 = jnp.stack([_pad_cols(w_cmp_k2[0], LANES), _pad_cols(w_cmp_v2[0], LANES)]).astype(BF16)
    w2t = jnp.stack([w_cmp_k2[0].T, w_cmp_v2[0].T]).astype(BF16)
    cmp_end = CMP_STRIDE * jnp.arange(ncp, dtype=jnp.int32) + (CMP_LEN - 1)
    cmp_rows, cmp_t = _compress(r, pos, w1, w2, w2t, _position_features(cmp_end))
    ones = jnp.zeros((ONES_ROWS, ncp), BF16).at[0].set(1.0)
    const_rows = jnp.broadcast_to(jnp.concatenate([ov_t, ones]), (batch, NSA_GROUPS, NS_PAD + ONES_ROWS, ncp))
    vaug = jnp.concatenate([cmp_t[1], const_rows], axis=2)
    vaug = vaug.reshape(batch, NSA_GROUPS, vaug.shape[2], ncp // NSA_TILE, NSA_TILE).transpose(0, 1, 3, 2, 4)

    oc, selt, flags = _nsa_cmp(q, cmp_rows, vaug, gt, batch, seq, n_cmp)
    counts, words = _active_tiles(flags.reshape(flags.shape[0], NS_PAD), batch, seq)
    o_nsa = _nsa_attn(counts, words, q, selt, oc, gt, ksw, vt, batch, seq)

    split = MLA_HEADS * MLA_V
    x1 = _out_proj(x2, o_mla, o_nsa, w_o[0][:split].astype(BF16), w_o[0][split:].astype(BF16))
    out = _ffn(x1, ffn_norm_g[0][None], w_gate[0].astype(BF16), w_up[0].astype(BF16), w_down[0].astype(BF16),
               final_norm_g[None])
    return out.reshape(batch, seq, d)
```

```python
import functools
import math

import numpy as np
import jax
import jax.numpy as jnp
from jax import lax
from jax.experimental import pallas as pl
from jax.experimental.pallas import tpu as pltpu

F32 = jnp.float32
BF16 = jnp.bfloat16

EPS = 1e-6
NEG = -1e30
LOG2E = math.log2(math.e)
LANES = 128

MLA_HEADS = 8
MLA_Q_LORA = 512
MLA_KV_LORA = 256
MLA_NOPE = 128
MLA_ROPE = 64
MLA_V = 128
ROPE_THETA = 10000.0
MLA_TK = 512

NSA_HEADS = 16
NSA_GROUPS = 2
NSA_HPG = NSA_HEADS // NSA_GROUPS
NSA_DK = 64
CMP_LEN = 32
CMP_STRIDE = 16
CMP_HIDDEN = 128
SLC_LEN = 64
SLC_TOPK = 16
WINDOW = 512
FORCE_SCORE = 1e4
NSA_TILE = 128
NS_PAD = 128
CMP_TQ = 256
ATT_TQ = 256
N_GATES = 3 * NSA_HPG
POS_SPLIT = 64
ONES_ROWS = 16

VMEM_LIMIT = 56 * 1024 * 1024


def _params(n_axes):
    return pltpu.CompilerParams(dimension_semantics=("arbitrary",) * n_axes, vmem_limit_bytes=VMEM_LIMIT)


def _resident(shape):
    zeros = (0,) * len(shape)
    return pl.BlockSpec(shape, lambda *_: zeros, pipeline_mode=pl.Buffered(1))


def _rmsnorm(x, g):
    return x * lax.rsqrt(jnp.mean(x * x, axis=-1, keepdims=True) + EPS) * g


def _dot(a, b):
    return jnp.dot(a, b, preferred_element_type=F32)


def _dot_nt(a, b):
    return lax.dot_general(a, b, (((1,), (1,)), ((), ())), preferred_element_type=F32)


def _ones_rows(width):
    return jnp.where(lax.broadcasted_iota(jnp.int32, (ONES_ROWS, width), 0) == 0, 1.0, 0.0).astype(BF16)


def _flash_weights(s, m_ref, idx):
    m_prev = m_ref[idx]
    m_new = jnp.maximum(m_prev, jnp.max(s, axis=0, keepdims=True))
    m_ref[idx] = m_new
    return jnp.exp2(m_prev - m_new), jnp.exp2(s - m_new).astype(BF16)


def _stack_heads(qt_ref, qt_scr):
    tiles, rows, t = qt_ref.shape
    for h in range(rows // LANES):
        for c in range(tiles):
            qt_scr[:, (h * tiles + c) * t:(h * tiles + c + 1) * t] = qt_ref[c, h * LANES:(h + 1) * LANES, :]


def _flash_update(s, v_t, m_ref, acc_ref, idx):
    alpha, p = _flash_weights(s, m_ref, idx)
    acc_ref[idx] = alpha * acc_ref[idx] + _dot(v_t, p)


def _skewed_flash(count, tile_at, keepf_fn, k_tile, vt_tile, qt_scr, s_scr, m_ref, acc_ref, idx):
    pairs = (count + 1) // 2

    def pair(u):
        return [tile_at(jnp.minimum(2 * u + r, count - 1)) for r in range(2)]

    def scores(u, slot):
        j0, j1 = pair(jnp.minimum(u, pairs - 1))
        s_scr[slot] = _dot(jnp.concatenate([k_tile(j0), k_tile(j1)], axis=0), qt_scr[...])

    def consume(u, slot):
        j0, j1 = pair(u)
        second = jnp.where(2 * u + 1 < count, 1.0, 0.0)
        bias = jnp.concatenate([jnp.where(keepf_fn(j0) > 0.5, 0.0, NEG),
                                jnp.where(keepf_fn(j1) * second > 0.5, 0.0, NEG)], axis=0)
        s = s_scr[slot]
        s = s + jnp.concatenate([bias] * (s.shape[1] // bias.shape[1]), axis=1)
        _flash_update(s, jnp.concatenate([vt_tile(j0), vt_tile(j1)], axis=1), m_ref, acc_ref, idx)

    def body(w, carry):
        u = 2 * w
        scores(u + 1, 1)
        consume(u, 0)
        scores(u + 2, 0)
        consume(u + 1, 1)
        return carry

    scores(0, 0)
    lax.fori_loop(0, pairs // 2, body, 0)

    @pl.when(pairs % 2 == 1)
    def _():
        consume(pairs - 1, 0)


_IN_COLS = dict(cq=(0, 512), ckv=(512, 768), kra=(768, 896), krb=(896, 1024), kvc=(1024, 1280), ksw=(1280, 1792))
_QT_ROWS = NSA_HEADS * LANES
_VT_ROWS = 2 * NSA_GROUPS * NSA_DK
_VT_OUT = _VT_ROWS // NSA_DK * (NSA_DK + ONES_ROWS)


def _inproj_kernel(x_ref, g_ref, w_ref, wt_ref, qfeat_ref, cos_ref, sin_ref, kfeat_ref, cq_ref, ckv_ref, kr_ref,
                   kvc_ref, ksw_ref, qt_ref, vt_ref, gt_ref):
    h = _rmsnorm(x_ref[...], g_ref[...]).astype(BF16)

    def mm(name):
        lo, hi = _IN_COLS[name]
        return _dot(h, w_ref[:, lo:hi])

    cq_ref[...] = mm("cq")
    ckv_ref[...] = mm("ckv")
    kr_ref[...] = (mm("kra") * cos_ref[...] + mm("krb") * sin_ref[...]).astype(BF16)
    kvc_ref[...] = mm("kvc")
    kfeat = kfeat_ref[...]
    ksw_ref[...] = (mm("ksw") + jnp.concatenate([kfeat] * (2 * NSA_GROUPS), axis=1)).astype(BF16)
    t = _dot_nt(wt_ref[...], h)
    ones = _ones_rows(NSA_TILE)
    qfeat = qfeat_ref[...]
    for c in range(vt_ref.shape[0]):
        cs = slice(c * NSA_TILE, (c + 1) * NSA_TILE)
        qt_ref[c] = (t[:_QT_ROWS, cs] + qfeat).astype(BF16)
        pieces = []
        for k in range(_VT_ROWS // NSA_DK):
            rows = slice(_QT_ROWS + k * NSA_DK, _QT_ROWS + (k + 1) * NSA_DK)
            pieces += [t[rows, cs].astype(BF16), ones]
        vt_ref[c] = jnp.concatenate(pieces, axis=0)
        gt_ref[c] = t[_QT_ROWS + _VT_ROWS:_QT_ROWS + _VT_ROWS + NSA_GROUPS * N_GATES, cs]


def _in_proj(x2, g, w, wt, qfeat, cos_t, sin_t, kfeat, seq, tm=256):
    n, d = x2.shape
    nt = seq // tm
    widths = {k: hi - lo for k, (lo, hi) in _IN_COLS.items()}
    row = lambda c: pl.BlockSpec((tm, c), lambda i: (i, 0))
    tab = pl.BlockSpec((tm, LANES), lambda i: (i % nt, 0))
    outs = [("cq", F32), ("ckv", F32), ("kra", BF16), ("kvc", F32), ("ksw", BF16)]
    tiles = tm // NSA_TILE
    tspec = lambda rows: pl.BlockSpec((tiles, rows, NSA_TILE), lambda i: (i, 0, 0))
    return pl.pallas_call(
        _inproj_kernel,
        grid=(n // tm,),
        in_specs=[row(d), _resident((1, d)), _resident(w.shape), _resident(wt.shape), _resident(qfeat.shape),
                  tab, tab, tab],
        out_specs=[row(widths[k]) for k, _ in outs] + [tspec(_QT_ROWS), tspec(_VT_OUT), tspec(NSA_GROUPS * N_GATES)],
        out_shape=[jax.ShapeDtypeStruct((n, widths[k]), dt) for k, dt in outs] + [
            jax.ShapeDtypeStruct((n // NSA_TILE, _QT_ROWS, NSA_TILE), BF16),
            jax.ShapeDtypeStruct((n // NSA_TILE, _VT_OUT, NSA_TILE), BF16),
            jax.ShapeDtypeStruct((n // NSA_TILE, NSA_GROUPS * N_GATES, NSA_TILE), F32)],
        compiler_params=_params(1),
        name="in_proj",
    )(x2, g, w, wt, qfeat, cos_t, sin_t, kfeat)


def _mlaup_kernel(cq_ref, ckv_ref, gq_ref, gkv_ref, wq_ref, wk_ref, wvt_ref, cos_ref, sin_ref, qn_ref, qr_ref,
                  kn_ref, vt_ref, *, scale):
    cqn = _rmsnorm(cq_ref[...], gq_ref[...]).astype(BF16)
    ckvn = _rmsnorm(ckv_ref[...], gkv_ref[...]).astype(BF16)
    hn = MLA_HEADS * MLA_NOPE
    hr = MLA_HEADS * MLA_ROPE
    qn_ref[...] = (_dot(cqn, wq_ref[:, :hn]) * scale).astype(BF16)
    a = _dot(cqn, wq_ref[:, hn:hn + hr])
    b = _dot(cqn, wq_ref[:, hn + hr:])
    cos = cos_ref[...]
    sin = sin_ref[...]
    for p in range(hr // LANES):
        sl = slice(p * LANES, (p + 1) * LANES)
        qr_ref[:, sl] = ((a[:, sl] * cos + b[:, sl] * sin) * scale).astype(BF16)
    kn_ref[...] = _dot(ckvn, wk_ref[...]).astype(BF16)
    v_t = _dot_nt(wvt_ref[...], ckvn).astype(BF16)
    ones = _ones_rows(MLA_TK)
    for c in range(vt_ref.shape[0]):
        pieces = []
        for h in range(MLA_HEADS):
            pieces += [v_t[h * MLA_V:(h + 1) * MLA_V, c * MLA_TK:(c + 1) * MLA_TK], ones]
        vt_ref[c] = jnp.concatenate(pieces, axis=0)


def _mla_up(cq, ckv, gq, gkv, wq, wk, wvt, cos_t, sin_t, seq, tm=512):
    n = cq.shape[0]
    nt = seq // tm
    hn = MLA_HEADS * MLA_NOPE
    hr = MLA_HEADS * MLA_ROPE
    hv = MLA_HEADS * (MLA_V + ONES_ROWS)
    row = lambda c: pl.BlockSpec((tm, c), lambda i: (i, 0))
    tab = pl.BlockSpec((tm, LANES), lambda i: (i % nt, 0))
    scale = (MLA_NOPE + MLA_ROPE) ** -0.5 * LOG2E
    return pl.pallas_call(
        functools.partial(_mlaup_kernel, scale=scale),
        grid=(n // tm,),
        in_specs=[row(MLA_Q_LORA), row(MLA_KV_LORA), _resident(gq.shape), _resident(gkv.shape),
                  _resident(wq.shape), _resident(wk.shape), _resident(wvt.shape), tab, tab],
        out_specs=[row(hn), row(hr), row(hn), pl.BlockSpec((tm // MLA_TK, hv, MLA_TK), lambda i: (i, 0, 0))],
        out_shape=[jax.ShapeDtypeStruct((n, hn), BF16), jax.ShapeDtypeStruct((n, hr), BF16),
                   jax.ShapeDtypeStruct((n, hn), BF16), jax.ShapeDtypeStruct((n // MLA_TK, hv, MLA_TK), BF16)],
        compiler_params=_params(1),
        name="mla_up",
    )(cq, ckv, gq, gkv, wq, wk, wvt, cos_t, sin_t)


def _mla_attn_kernel(qn_ref, qr_ref, kn_ref, kr_ref, vt_ref, o_ref, m_ref, acc_ref, s_scr, qt_scr, *, tq):
    i = pl.program_id(2)
    tk = MLA_TK
    lane = lax.broadcasted_iota(jnp.int32, (tq, LANES), 1)
    krow = lax.broadcasted_iota(jnp.int32, (tk, tq), 0)
    qcol = lax.broadcasted_iota(jnp.int32, (tk, tq), 1)
    qr = qr_ref[...]
    zero = jnp.zeros_like(qr)
    for hh in range(2):
        q_cat = jnp.concatenate([qn_ref[:, hh * LANES:(hh + 1) * LANES],
                                 jnp.where((lane < MLA_ROPE) == (hh == 0), qr, zero)], axis=1)
        qt_scr[hh] = q_cat.astype(F32).T.astype(BF16)
    m_ref[...] = jnp.full_like(m_ref, NEG)
    acc_ref[...] = jnp.zeros_like(acc_ref)
    vrows = MLA_V + ONES_ROWS

    def scores(j, slot):
        ks = pl.ds(pl.multiple_of(j * tk, tk), tk)
        k_rope = kr_ref[ks, :]
        for hh in range(2):
            hs = slice(hh * LANES, (hh + 1) * LANES)
            s_scr[slot, hh] = _dot(jnp.concatenate([kn_ref[ks, hs], k_rope], axis=1), qt_scr[hh])

    def consume(j, slot, diagonal):
        weights = []
        for hh in range(2):
            s = s_scr[slot, hh]
            if diagonal:
                s = jnp.where(j * tk + krow <= i * tq + qcol, s, NEG)
            weights.append(_flash_weights(s, m_ref, hh))
        for hh in range(2):
            alpha, p = weights[hh]
            acc_ref[hh] = alpha * acc_ref[hh] + _dot(vt_ref[0, j, hh * vrows:(hh + 1) * vrows, :], p)

    def body(u, carry):
        j = 2 * u
        scores(j + 1, 1)
        consume(j, 0, False)
        scores(j + 2, 0)
        consume(j + 1, 1, False)
        return carry

    assert tq == tk, "the last key tile of a query tile must be its only diagonal tile"
    scores(0, 0)
    lax.fori_loop(0, i // 2, body, 0)

    @pl.when(i % 2 == 1)
    def _():
        scores(i, 1)
        consume(i - 1, 0, False)
        consume(i, 1, True)

    @pl.when(i % 2 == 0)
    def _():
        consume(i, 0, True)

    for hh in range(2):
        o_t = acc_ref[hh, :MLA_V] / acc_ref[hh, MLA_V:MLA_V + 1]
        for c in range(tq // LANES):
            o_ref[c * LANES:(c + 1) * LANES, hh * LANES:(hh + 1) * LANES] = (
                o_t[:, c * LANES:(c + 1) * LANES].T.astype(BF16))


def _mla_attn(qn, qr, kn, kr, vt, batch, seq, tq=512):
    n = qn.shape[0]
    nq = seq // tq
    nk = seq // MLA_TK
    pairs = MLA_HEADS // 2
    pw = 2 * LANES
    qspec = lambda c: pl.BlockSpec((tq, c), lambda b, p, i: (b * nq + i, p))
    vrows = MLA_V + ONES_ROWS
    vt4 = vt.reshape(batch, nk, MLA_HEADS * vrows, MLA_TK)
    return pl.pallas_call(
        functools.partial(_mla_attn_kernel, tq=tq),
        grid=(batch, pairs, nq),
        in_specs=[qspec(pw), qspec(LANES), pl.BlockSpec((seq, pw), lambda b, p, i: (b, p)),
                  pl.BlockSpec((seq, LANES), lambda b, p, i: (b, 0)),
                  pl.BlockSpec((1, nk, 2 * vrows, MLA_TK), lambda b, p, i: (b, 0, p, 0))],
        out_specs=qspec(pw),
        out_shape=jax.ShapeDtypeStruct((n, MLA_HEADS * MLA_V), BF16),
        scratch_shapes=[pltpu.VMEM((2, 1, tq), F32), pltpu.VMEM((2, vrows, tq), F32),
                        pltpu.VMEM((2, 2, MLA_TK, tq), F32), pltpu.VMEM((2, 2 * LANES, tq), BF16)],
        compiler_params=_params(3),
        name="mla_attn",
    )(qn, qr, kn, kr, vt4)


def _compress_kernel(r_ref, pos_ref, w1_ref, w2_ref, w2t_ref, feat_ref, o_ref, ot_ref):
    half = CMP_STRIDE * NSA_DK
    r = r_ref[0, 0, 0]
    pos = pos_ref[0]
    a = _dot((r + pos[:, :half]).astype(BF16), w1_ref[0, :half, :])
    b = _dot((r + pos[:, half:]).astype(BF16), w1_ref[0, half:, :])
    hid = a + pltpu.roll(b, b.shape[0] - 1, 0)
    act = (hid * jax.nn.sigmoid(hid)).astype(BF16)
    o_ref[0, 0, 0] = (_dot(act, w2_ref[0]) + feat_ref[...]).astype(BF16)
    ot_ref[0, 0, 0] = _dot_nt(w2t_ref[0], act).astype(BF16)


def _compress(r, pos, w1, w2, w2t, feat):
    _, batch, groups, nc, width = r.shape
    return pl.pallas_call(
        _compress_kernel,
        grid=(2, batch, groups),
        in_specs=[pl.BlockSpec((1, 1, 1, nc, width), lambda t, b, g: (t, b, g, 0, 0)),
                  pl.BlockSpec((1, 1, 2 * width), lambda t, b, g: (t, 0, 0)),
                  pl.BlockSpec((1, 2 * width, CMP_HIDDEN), lambda t, b, g: (t, 0, 0)),
                  pl.BlockSpec((1, CMP_HIDDEN, LANES), lambda t, b, g: (t, 0, 0)),
                  pl.BlockSpec((1, NSA_DK, CMP_HIDDEN), lambda t, b, g: (t, 0, 0)),
                  pl.BlockSpec((nc, LANES), lambda t, b, g: (0, 0))],
        out_specs=[pl.BlockSpec((1, 1, 1, nc, LANES), lambda t, b, g: (t, b, g, 0, 0)),
                   pl.BlockSpec((1, 1, 1, NSA_DK, nc), lambda t, b, g: (t, b, g, 0, 0))],
        out_shape=[jax.ShapeDtypeStruct((2, batch, groups, nc, LANES), BF16),
                   jax.ShapeDtypeStruct((2, batch, groups, NSA_DK, nc), BF16)],
        compiler_params=_params(3),
        name="compress",
    )(r, pos, w1, w2, w2t, feat)


def _nsa_cmp_kernel(q_ref, kc_ref, vaug_ref, gt_ref, oc_ref, selt_ref, flag_ref, q_scr, m_ref, acc_ref, s_scr,
                    *, n_cmp):
    i = pl.program_id(2)
    t = NSA_TILE
    tq = CMP_TQ
    hpg = NSA_HPG
    _stack_heads(q_ref, q_scr)
    m_ref[...] = jnp.full_like(m_ref, NEG)
    acc_ref[...] = jnp.zeros_like(acc_ref)

    crow = lax.broadcasted_iota(jnp.int32, (t, tq), 0)
    qpos = i * tq + lax.broadcasted_iota(jnp.int32, (t, tq), 1)

    def keepf(j):
        c = j * t + crow
        return jnp.where(CMP_STRIDE * c + (CMP_LEN - 1) <= qpos, jnp.where(c < n_cmp, 1.0, 0.0), 0.0)

    count = jnp.minimum((i * tq + tq - CMP_LEN) // (CMP_STRIDE * t) + 1, kc_ref.shape[3] // t)
    _skewed_flash(count, lambda n: n, keepf, lambda j: kc_ref[0, 0, 0, pl.ds(pl.multiple_of(j * t, t), t), :],
                  lambda j: vaug_ref[0, 0, j], q_scr, s_scr, m_ref, acc_ref, 0)

    some = jnp.where(qpos[:1] >= CMP_LEN - 1, 1.0, 0.0)
    inv = jnp.concatenate([some] * hpg, axis=1) / acc_ref[0, NSA_DK + NS_PAD:NSA_DK + NS_PAD + 1]
    gates = jax.nn.sigmoid(jnp.concatenate([gt_ref[c] for c in range(tq // t)], axis=1))
    imp = jnp.zeros((NS_PAD, tq), F32)
    for pair in range(hpg // 2):
        halves = []
        for h in (2 * pair, 2 * pair + 1):
            hs = slice(h * tq, (h + 1) * tq)
            halves.append(gates[3 * h:3 * h + 1, :] * (acc_ref[0, :NSA_DK, hs] * inv[:, hs]))
            imp = imp + acc_ref[0, NSA_DK:NSA_DK + NS_PAD, hs] * inv[:, hs]
        oc_ref[:, pair * LANES:(pair + 1) * LANES] = jnp.concatenate(halves, axis=0).T.astype(BF16)

    blk = lax.broadcasted_iota(jnp.int32, (NS_PAD, tq), 0)
    blk_t = (i * tq + lax.broadcasted_iota(jnp.int32, (NS_PAD, tq), 1)) // SLC_LEN
    forced = jnp.where(blk == 0, 1.0, 0.0) + jnp.where(blk == blk_t, 1.0, 0.0) + jnp.where(blk == blk_t - 1, 1.0, 0.0)
    imp = jnp.where(forced > 0.5, FORCE_SCORE, jnp.where(blk <= blk_t, imp, -1.0))
    blkf = blk.astype(F32)
    sel = jnp.zeros((NS_PAD, tq), F32)
    for _ in range(SLC_TOPK):
        best = jnp.max(imp, axis=0, keepdims=True)
        pick = jnp.min(jnp.where(imp == best, blkf, float(NS_PAD)), axis=0, keepdims=True)
        hit = blkf == pick
        sel = jnp.where(hit, 1.0, sel)
        imp = jnp.where(hit, -3e38, imp)
    selt_ref[0, 0] = sel
    for c in range(tq // t):
        flag_ref[c] = jnp.max(sel[:, c * t:(c + 1) * t].T, axis=0, keepdims=True).astype(jnp.int32)


def _nsa_cmp(q, kc, vaug, gt, batch, seq, n_cmp):
    t = NSA_TILE
    tq = CMP_TQ
    nq = seq // tq
    n = batch * seq
    ncp = kc.shape[3]
    sub = tq // t
    rows = lambda c: pl.BlockSpec((tq, c), lambda b, g, i: (b * nq + i, g))
    return pl.pallas_call(
        functools.partial(_nsa_cmp_kernel, n_cmp=n_cmp),
        grid=(batch, NSA_GROUPS, nq),
        in_specs=[pl.BlockSpec((sub, NSA_HPG * LANES, t), lambda b, g, i: (b * nq + i, g, 0)),
                  pl.BlockSpec((1, 1, 1, ncp, LANES), lambda b, g, i: (0, b, g, 0, 0)),
                  pl.BlockSpec((1, 1) + vaug.shape[2:], lambda b, g, i: (b, g, 0, 0, 0)),
                  pl.BlockSpec((sub, N_GATES, t), lambda b, g, i: (b * nq + i, g, 0))],
        out_specs=[rows(NSA_HPG * NSA_DK),
                   pl.BlockSpec((1, 1, NS_PAD, tq), lambda b, g, i: (b, g, 0, i)),
                   pl.BlockSpec((sub, 1, NS_PAD), lambda b, g, i: ((b * NSA_GROUPS + g) * nq + i, 0, 0))],
        out_shape=[jax.ShapeDtypeStruct((n, NSA_HEADS * NSA_DK), BF16),
                   jax.ShapeDtypeStruct((batch, NSA_GROUPS, NS_PAD, seq), F32),
                   jax.ShapeDtypeStruct((batch * NSA_GROUPS * seq // t, 1, NS_PAD), jnp.int32)],
        scratch_shapes=[pltpu.VMEM((LANES, NSA_HPG * tq), BF16), pltpu.VMEM((1, 1, NSA_HPG * tq), F32),
                        pltpu.VMEM((1, NSA_DK + NS_PAD + ONES_ROWS, NSA_HPG * tq), F32),
                        pltpu.VMEM((2, 2 * t, NSA_HPG * tq), F32)],
        compiler_params=_params(3),
        name="nsa_cmp",
    )(q, kc, vaug, gt)


IDS_PER_WORD = 4


def _nsa_attn_kernel(counts_ref, words_ref, q_ref, selt_ref, oc_ref, gt_ref, ks_ref, kw_ref, vst_ref, vwt_ref, o_ref,
                     q_scr, m_ref, acc_ref, s_scr, *, nq, words_per_step):
    b = pl.program_id(0)
    g = pl.program_id(1)
    i = pl.program_id(2)
    t = NSA_TILE
    tq = ATT_TQ
    sub = tq // t
    hpg = NSA_HPG
    _stack_heads(q_ref, q_scr)
    m_ref[...] = jnp.full_like(m_ref, NEG)
    acc_ref[...] = jnp.zeros_like(acc_ref)

    krow = lax.broadcasted_iota(jnp.int32, (t, tq), 0)
    qpos = i * tq + lax.broadcasted_iota(jnp.int32, (t, tq), 1)
    blocks_per_tile = t // SLC_LEN
    slc, win = 0, 1
    step = (b * NSA_GROUPS + g) * nq + i

    def run_branch(branch, count, tile_at, keepf_fn, k_ref, vt_ref):
        _skewed_flash(count, tile_at, keepf_fn, lambda j: k_ref[pl.ds(pl.multiple_of(j * t, t), t), :],
                      lambda j: vt_ref[0, j], q_scr, s_scr, m_ref, acc_ref, branch)

    def slc_tile(n):
        word = words_ref[step * words_per_step + n // IDS_PER_WORD]
        return lax.shift_right_logical(word, 8 * (n % IDS_PER_WORD)) & 255

    def slc_keepf(j):
        dist = qpos - (j * t + krow)
        sel_rows = [selt_ref[0, 0, pl.ds(blocks_per_tile * j + r, 1), :] for r in range(blocks_per_tile)]
        picked = sel_rows[-1]
        for r in range(blocks_per_tile - 2, -1, -1):
            picked = jnp.where(krow < (r + 1) * SLC_LEN, sel_rows[r], picked)
        return jnp.where(dist >= 0, picked, 0.0)

    def win_keepf(j):
        dist = qpos - (j * t + krow)
        return jnp.where(dist >= 0, jnp.where(dist < WINDOW, 1.0, 0.0), 0.0)

    run_branch(slc, counts_ref[step], slc_tile, slc_keepf, ks_ref, vst_ref)
    first = jnp.maximum(i * sub - WINDOW // t, 0)
    run_branch(win, (i + 1) * sub - first, lambda n: first + n, win_keepf, kw_ref, vwt_ref)

    gates = jax.nn.sigmoid(jnp.concatenate([gt_ref[c] for c in range(sub)], axis=1))
    inv_s = 1.0 / acc_ref[slc, NSA_DK:NSA_DK + 1]
    inv_w = 1.0 / acc_ref[win, NSA_DK:NSA_DK + 1]
    for pair in range(hpg // 2):
        halves = []
        for h in (2 * pair, 2 * pair + 1):
            hs = slice(h * tq, (h + 1) * tq)
            halves.append(gates[3 * h + 1:3 * h + 2, :] * (acc_ref[slc, :NSA_DK, hs] * inv_s[:, hs])
                          + gates[3 * h + 2:3 * h + 3, :] * (acc_ref[win, :NSA_DK, hs] * inv_w[:, hs]))
        ps = slice(pair * LANES, (pair + 1) * LANES)
        o_ref[:, ps] = (oc_ref[:, ps].astype(F32) + jnp.concatenate(halves, axis=0).T).astype(BF16)


def _nsa_attn(counts, words, q, selt, oc, gt, ksw, vt, batch, seq):
    t = NSA_TILE
    tq = ATT_TQ
    sub = tq // t
    nq = seq // tq
    nk = seq // t
    n = batch * seq
    words_per_step = words.shape[0] // (batch * NSA_GROUPS * nq)
    rows = lambda c: pl.BlockSpec((tq, c), lambda b, g, i, *_: (b * nq + i, g))
    tiles = lambda r: pl.BlockSpec((sub, r, t), lambda b, g, i, *_: (b * nq + i, g, 0))
    key = lambda which: pl.BlockSpec((seq, LANES), lambda b, g, i, *_: (b, which * NSA_GROUPS + g))
    vt4 = vt.reshape(batch, nk, _VT_OUT, t)
    vrows = NSA_DK + ONES_ROWS
    val = lambda which: pl.BlockSpec((1, nk, vrows, t), lambda b, g, i, *_: (b, 0, which * NSA_GROUPS + g, 0))
    cols = NSA_HPG * tq
    grid_spec = pltpu.PrefetchScalarGridSpec(
        num_scalar_prefetch=2,
        grid=(batch, NSA_GROUPS, nq),
        in_specs=[tiles(NSA_HPG * LANES),
                  pl.BlockSpec((1, 1, NS_PAD, tq), lambda b, g, i, *_: (b, g, 0, i)),
                  rows(NSA_HPG * NSA_DK), tiles(N_GATES), key(0), key(1), val(0), val(1)],
        out_specs=rows(NSA_HPG * NSA_DK),
        scratch_shapes=[pltpu.VMEM((LANES, cols), BF16), pltpu.VMEM((2, 1, cols), F32),
                        pltpu.VMEM((2, vrows, cols), F32), pltpu.VMEM((2, 2 * t, cols), F32)],
    )
    return pl.pallas_call(
        functools.partial(_nsa_attn_kernel, nq=nq, words_per_step=words_per_step),
        grid_spec=grid_spec,
        out_shape=jax.ShapeDtypeStruct((n, NSA_HEADS * NSA_DK), BF16),
        compiler_params=_params(3),
        name="nsa_attn",
    )(counts, words, q, selt, oc, gt, ksw, ksw, vt4, vt4)


def _outproj_kernel(x_ref, om_ref, on_ref, wm_ref, wn_ref, o_ref):
    o_ref[...] = x_ref[...] + _dot(om_ref[...], wm_ref[...]) + _dot(on_ref[...], wn_ref[...])


def _out_proj(x2, o_mla, o_nsa, wm, wn, tm=512):
    n, d = x2.shape
    row = lambda c: pl.BlockSpec((tm, c), lambda i: (i, 0))
    return pl.pallas_call(
        _outproj_kernel,
        grid=(n // tm,),
        in_specs=[row(d), row(o_mla.shape[1]), row(o_nsa.shape[1]), _resident(wm.shape), _resident(wn.shape)],
        out_specs=row(d),
        out_shape=jax.ShapeDtypeStruct((n, d), F32),
        compiler_params=_params(1),
        name="out_proj",
    )(x2, o_mla, o_nsa, wm, wn)


def _ffn_kernel(x_ref, g_ref, wg_ref, wu_ref, wd_ref, gf_ref, o_ref, h_scr, acc_scr):
    f = pl.program_id(1)

    @pl.when(f == 0)
    def _():
        h_scr[...] = _rmsnorm(x_ref[...], g_ref[...]).astype(BF16)
        acc_scr[...] = jnp.zeros_like(acc_scr)

    h = h_scr[...]
    gate = _dot(h, wg_ref[...])
    act = (gate * jax.nn.sigmoid(gate)) * _dot(h, wu_ref[...])
    acc_scr[...] += _dot(act.astype(BF16), wd_ref[...])

    @pl.when(f == pl.num_programs(1) - 1)
    def _():
        o_ref[...] = _rmsnorm(x_ref[...] + acc_scr[...], gf_ref[...])


def _ffn(x1, g, wg, wu, wd, gf, tm=512, tf=512):
    n, d = x1.shape
    dff = wg.shape[1]
    return pl.pallas_call(
        _ffn_kernel,
        grid=(n // tm, dff // tf),
        in_specs=[pl.BlockSpec((tm, d), lambda i, f: (i, 0)), _resident((1, d)),
                  pl.BlockSpec((d, tf), lambda i, f: (0, f)), pl.BlockSpec((d, tf), lambda i, f: (0, f)),
                  pl.BlockSpec((tf, d), lambda i, f: (f, 0)), _resident((1, d))],
        out_specs=pl.BlockSpec((tm, d), lambda i, f: (i, 0)),
        out_shape=jax.ShapeDtypeStruct((n, d), F32),
        scratch_shapes=[pltpu.VMEM((tm, d), BF16), pltpu.VMEM((tm, d), F32)],
        compiler_params=_params(2),
        name="ffn",
    )(x1, g, wg, wu, wd, gf)


def _pad_cols(w, width):
    return jnp.pad(w, ((0, 0), (0, width - w.shape[1])))


def _rot_cols(w):
    half = w.shape[1] // 2
    return jnp.concatenate([-w[:, half:], w[:, :half]], axis=1)


def _fused_in_weights(w_in):
    sizes = (MLA_Q_LORA, MLA_KV_LORA, MLA_ROPE, NSA_HEADS * NSA_DK) + (NSA_GROUPS * NSA_DK,) * 6 + (3 * NSA_HEADS,)
    offs = np.cumsum(sizes)[:-1].tolist()
    cq, ckv, kr, q, kc, vc, ks, vs, kw, vw, gate = jnp.split(w_in, offs, axis=1)
    d = w_in.shape[0]
    q_pad = _pad_cols((q * (NSA_DK ** -0.5 * LOG2E)).reshape(d * NSA_HEADS, NSA_DK), LANES)
    q_pad = q_pad.reshape(d, NSA_HEADS * LANES)
    per_group = lambda w, c: _pad_cols(w.reshape(d * NSA_GROUPS, c), LANES).reshape(d, NSA_GROUPS * LANES)
    kr_rot = _rot_cols(kr)
    cols = [cq, ckv, kr, kr, kr_rot, kr_rot, kc, vc, per_group(ks, NSA_DK), per_group(kw, NSA_DK)]
    w = jnp.concatenate(cols, axis=1).astype(BF16)
    assert w.shape[1] == _IN_COLS["ksw"][1]
    rows_t = jnp.concatenate([q_pad, vs, vw, gate], axis=1).T
    pad = -rows_t.shape[0] % 16
    return w, jnp.pad(rows_t, ((0, pad), (0, 0))).astype(BF16)


def _slope_features(slopes2):
    s1 = slopes2.astype(BF16).astype(F32)
    s2 = (slopes2 - s1).astype(BF16).astype(F32)
    s3 = (slopes2 - s1 - s2).astype(BF16).astype(F32)
    pieces = jnp.stack([s1, s2, s3, s1, s2, s3], axis=1)
    column = jnp.pad(pieces, ((0, 0), (NSA_DK, LANES - NSA_DK - 6))).reshape(-1, 1)
    return jnp.broadcast_to(column, (column.shape[0], NSA_TILE))


def _position_features(pos):
    hi = (POS_SPLIT * (pos // POS_SPLIT)).astype(F32)
    lo = (pos % POS_SPLIT).astype(F32)
    return jnp.pad(jnp.stack([hi, hi, hi, lo, lo, lo], axis=1), ((0, 0), (NSA_DK, LANES - NSA_DK - 6)))


def _fused_uq_weight(w_uq):
    d = w_uq.shape[0]
    w = w_uq.reshape(d, MLA_HEADS, MLA_NOPE + MLA_ROPE)
    nope = w[:, :, :MLA_NOPE].reshape(d, MLA_HEADS * MLA_NOPE)
    rope = w[:, :, MLA_NOPE:]
    rope_rot = jnp.concatenate([-rope[:, :, MLA_ROPE // 2:], rope[:, :, :MLA_ROPE // 2]], axis=2)
    flat = lambda r: r.reshape(d, MLA_HEADS * MLA_ROPE)
    return jnp.concatenate([nope, flat(rope), flat(rope_rot)], axis=1).astype(BF16)


def _rope_tables(seq):
    inv = ROPE_THETA ** (-jnp.arange(0, MLA_ROPE, 2, dtype=F32) / MLA_ROPE)
    ang = jnp.arange(seq, dtype=F32)[:, None] * inv[None, :]
    reps = 2 * LANES // MLA_ROPE
    return jnp.tile(jnp.cos(ang), (1, reps)), jnp.tile(jnp.sin(ang), (1, reps))


def _overlap_matrix(seq, n_cmp, ncp):
    cmp_start = CMP_STRIDE * np.arange(n_cmp)
    slc_start = SLC_LEN * np.arange(seq // SLC_LEN)
    ov = np.clip(np.minimum(cmp_start[:, None] + CMP_LEN, slc_start[None, :] + SLC_LEN)
                 - np.maximum(cmp_start[:, None], slc_start[None, :]), 0, None).astype(np.float32) / CMP_STRIDE
    out = np.zeros((NS_PAD, ncp), np.float32)
    out[:ov.shape[1], :n_cmp] = ov.T
    return jnp.asarray(out, BF16)


def _active_tiles(flags, seq):
    t_blocks = NSA_TILE // SLC_LEN
    sub = ATT_TQ // NSA_TILE
    nq = seq // ATT_TQ
    steps = flags.shape[0] // sub
    tiles = flags.reshape(steps, sub, NS_PAD // t_blocks, t_blocks).max(axis=(1, 3))
    tile_id = jnp.arange(tiles.shape[1], dtype=jnp.int32)[None, :]
    last_tile = ((jnp.arange(steps, dtype=jnp.int32) % nq) * sub + (sub - 1))[:, None]
    active = (tiles > 0) & (tile_id <= last_tile)
    order = jnp.argsort(jnp.where(active, tile_id, tile_id + tiles.shape[1]), axis=-1).astype(jnp.int32)
    shifts = 8 * jnp.arange(IDS_PER_WORD, dtype=jnp.int32)
    words = (order.reshape(steps, -1, IDS_PER_WORD) << shifts).sum(axis=-1).astype(jnp.int32)
    return active.sum(axis=-1).astype(jnp.int32), words.reshape(-1)


def kernel(x, attn_norm_g, w_in, mla_q_norm_g, mla_kv_norm_g, w_uq, w_uk, w_uv, cmp_pos_k, cmp_pos_v, w_cmp_k1,
           w_cmp_k2, w_cmp_v1, w_cmp_v2, w_o, ffn_norm_g, w_gate, w_up, w_down, final_norm_g):
    batch, seq, d = x.shape
    n = batch * seq
    assert w_in.shape[0] == 1, "the final RMSNorm is fused into the FFN kernel of a single layer"
    assert seq % (CMP_STRIDE * NSA_TILE) == 0 and seq // SLC_LEN <= NS_PAD and seq // POS_SPLIT <= 256
    n_cmp = (seq - CMP_LEN) // CMP_STRIDE + 1
    ncp = seq // CMP_STRIDE
    cos_t, sin_t = _rope_tables(seq)
    ov_t = _overlap_matrix(seq, n_cmp, ncp)
    slopes2 = 2.0 ** (-8.0 * jnp.arange(1, NSA_HEADS + 1, dtype=F32) / NSA_HEADS) * LOG2E
    qfeat = _slope_features(slopes2)
    kfeat = _position_features(jnp.arange(seq, dtype=jnp.int32))
    x2 = x.reshape(n, d)

    w_fused, w_fused_t = _fused_in_weights(w_in[0])
    cq, ckv, kr, kvc, ksw, q, vt, gt = _in_proj(
        x2, attn_norm_g[0][None], w_fused, w_fused_t, qfeat, cos_t, sin_t, kfeat, seq)

    qn, qr, kn, vt_mla = _mla_up(cq, ckv, mla_q_norm_g[0][None], mla_kv_norm_g[0][None], _fused_uq_weight(w_uq[0]),
                                 w_uk[0].astype(BF16), w_uv[0].T.astype(BF16), cos_t, sin_t, seq)
    o_mla = _mla_attn(qn, qr, kn, kr, vt_mla, batch, seq)

    r = kvc.reshape(batch, seq, 2, NSA_GROUPS, NSA_DK).transpose(2, 0, 3, 1, 4)
    r = r.reshape(2, batch, NSA_GROUPS, ncp, CMP_STRIDE * NSA_DK)
    pos = jnp.stack([cmp_pos_k[0].reshape(1, -1), cmp_pos_v[0].reshape(1, -1)])
    w1 = jnp.stack([w_cmp_k1[0], w_cmp_v1[0]]).astype(BF16)
    w2 = jnp.stack([_pad_cols(w_cmp_k2[0], LANES), _pad_cols(w_cmp_v2[0], LANES)]).astype(BF16)
    w2t = jnp.stack([w_cmp_k2[0].T, w_cmp_v2[0].T]).astype(BF16)
    cmp_end = CMP_STRIDE * jnp.arange(ncp, dtype=jnp.int32) + (CMP_LEN - 1)
    cmp_rows, cmp_t = _compress(r, pos, w1, w2, w2t, _position_features(cmp_end))
    ones = jnp.zeros((ONES_ROWS, ncp), BF16).at[0].set(1.0)
    const_rows = jnp.broadcast_to(jnp.concatenate([ov_t, ones]), (batch, NSA_GROUPS, NS_PAD + ONES_ROWS, ncp))
    vaug = jnp.concatenate([cmp_t[1], const_rows], axis=2)
    vaug = vaug.reshape(batch, NSA_GROUPS, vaug.shape[2], ncp // NSA_TILE, NSA_TILE).transpose(0, 1, 3, 2, 4)

    oc, selt, flags = _nsa_cmp(q, cmp_rows, vaug, gt, batch, seq, n_cmp)
    counts, words = _active_tiles(flags.reshape(flags.shape[0], NS_PAD), seq)
    o_nsa = _nsa_attn(counts, words, q, selt, oc, gt, ksw, vt, batch, seq)

    split = MLA_HEADS * MLA_V
    x1 = _out_proj(x2, o_mla, o_nsa, w_o[0][:split].astype(BF16), w_o[0][split:].astype(BF16))
    out = _ffn(x1, ffn_norm_g[0][None], w_gate[0].astype(BF16), w_up[0].astype(BF16), w_down[0].astype(BF16),
               final_norm_g[None])
    return out.reshape(batch, seq, d)
```

```python
import functools
import math

import numpy as np
import jax
import jax.numpy as jnp
from jax import lax
from jax.experimental import pallas as pl
from jax.experimental.pallas import tpu as pltpu

F32 = jnp.float32
BF16 = jnp.bfloat16

EPS = 1e-6
NEG = -1e30
LOG2E = math.log2(math.e)
LANES = 128

MLA_HEADS = 8
MLA_Q_LORA = 512
MLA_KV_LORA = 256
MLA_NOPE = 128
MLA_ROPE = 64
MLA_V = 128
ROPE_THETA = 10000.0
MLA_TK = 512

NSA_HEADS = 16
NSA_GROUPS = 2
NSA_HPG = NSA_HEADS // NSA_GROUPS
NSA_DK = 64
CMP_LEN = 32
CMP_STRIDE = 16
CMP_HIDDEN = 128
SLC_LEN = 64
SLC_TOPK = 16
WINDOW = 512
FORCE_SCORE = 1e4
NSA_TILE = 128
NS_PAD = 128
CMP_TQ = 256
ATT_TQ = 256
N_GATES = 3 * NSA_HPG
POS_SPLIT = 64
ONES_ROWS = 16

VMEM_LIMIT = 56 * 1024 * 1024


def _params(n_axes):
    return pltpu.CompilerParams(dimension_semantics=("arbitrary",) * n_axes, vmem_limit_bytes=VMEM_LIMIT)


def _resident(shape):
    zeros = (0,) * len(shape)
    return pl.BlockSpec(shape, lambda *_: zeros, pipeline_mode=pl.Buffered(1))


def _rmsnorm(x, g):
    return x * lax.rsqrt(jnp.mean(x * x, axis=-1, keepdims=True) + EPS) * g


def _dot(a, b):
    return jnp.dot(a, b, preferred_element_type=F32)


def _dot_nt(a, b):
    return lax.dot_general(a, b, (((1,), (1,)), ((), ())), preferred_element_type=F32)


def _ones_rows(width):
    return jnp.where(lax.broadcasted_iota(jnp.int32, (ONES_ROWS, width), 0) == 0, 1.0, 0.0).astype(BF16)


def _flash_weights(s, m_ref, idx):
    m_prev = m_ref[idx]
    m_new = jnp.maximum(m_prev, jnp.max(s, axis=0, keepdims=True))
    m_ref[idx] = m_new
    return jnp.exp2(m_prev - m_new), jnp.exp2(s - m_new).astype(BF16)


def _stack_heads(qt_ref, qt_scr):
    tiles, rows, t = qt_ref.shape
    for h in range(rows // LANES):
        for c in range(tiles):
            qt_scr[:, (h * tiles + c) * t:(h * tiles + c + 1) * t] = qt_ref[c, h * LANES:(h + 1) * LANES, :]


def _flash_update(s, v_t, m_ref, acc_ref, idx):
    alpha, p = _flash_weights(s, m_ref, idx)
    acc_ref[idx] = alpha * acc_ref[idx] + _dot(v_t, p)


def _skewed_flash(count, tile_at, keepf_fn, k_tile, vt_tile, qt_scr, s_scr, m_ref, acc_ref, idx):
    pairs = (count + 1) // 2

    def pair(u):
        return [tile_at(jnp.minimum(2 * u + r, count - 1)) for r in range(2)]

    def scores(u, slot):
        j0, j1 = pair(jnp.minimum(u, pairs - 1))
        s_scr[slot] = _dot(jnp.concatenate([k_tile(j0), k_tile(j1)], axis=0), qt_scr[...])

    def consume(u, slot):
        j0, j1 = pair(u)
        second = jnp.where(2 * u + 1 < count, 1.0, 0.0)
        bias = jnp.concatenate([jnp.where(keepf_fn(j0) > 0.5, 0.0, NEG),
                                jnp.where(keepf_fn(j1) * second > 0.5, 0.0, NEG)], axis=0)
        s = s_scr[slot]
        s = s + jnp.concatenate([bias] * (s.shape[1] // bias.shape[1]), axis=1)
        _flash_update(s, jnp.concatenate([vt_tile(j0), vt_tile(j1)], axis=1), m_ref, acc_ref, idx)

    def body(w, carry):
        u = 2 * w
        scores(u + 1, 1)
        consume(u, 0)
        scores(u + 2, 0)
        consume(u + 1, 1)
        return carry

    scores(0, 0)
    lax.fori_loop(0, pairs // 2, body, 0)

    @pl.when(pairs % 2 == 1)
    def _():
        consume(pairs - 1, 0)


_IN_COLS = dict(cq=(0, 512), ckv=(512, 768), kra=(768, 896), krb=(896, 1024), kvc=(1024, 1280), ksw=(1280, 1792))
_QT_ROWS = NSA_HEADS * LANES
_VT_ROWS = 2 * NSA_GROUPS * NSA_DK
_VT_OUT = _VT_ROWS // NSA_DK * (NSA_DK + ONES_ROWS)


def _inproj_kernel(x_ref, g_ref, w_ref, wt_ref, qfeat_ref, cos_ref, sin_ref, kfeat_ref, cq_ref, ckv_ref, kr_ref,
                   ksw_ref, kvc_ref, qt_ref, vt_ref, gt_ref):
    h = _rmsnorm(x_ref[...], g_ref[...]).astype(BF16)

    def mm(name):
        lo, hi = _IN_COLS[name]
        return _dot(h, w_ref[:, lo:hi])

    cq_ref[...] = mm("cq")
    ckv_ref[...] = mm("ckv")
    kr_ref[...] = (mm("kra") * cos_ref[...] + mm("krb") * sin_ref[...]).astype(BF16)
    kvc = mm("kvc")
    for k in range(kvc_ref.shape[0]):
        kvc_ref[k] = kvc[:, k * NSA_DK:(k + 1) * NSA_DK]
    kfeat = kfeat_ref[...]
    ksw_ref[...] = (mm("ksw") + jnp.concatenate([kfeat] * (2 * NSA_GROUPS), axis=1)).astype(BF16)
    t = _dot_nt(wt_ref[...], h)
    ones = _ones_rows(NSA_TILE)
    qfeat = qfeat_ref[...]
    for c in range(vt_ref.shape[0]):
        cs = slice(c * NSA_TILE, (c + 1) * NSA_TILE)
        qt_ref[c] = (t[:_QT_ROWS, cs] + qfeat).astype(BF16)
        pieces = []
        for k in range(_VT_ROWS // NSA_DK):
            rows = slice(_QT_ROWS + k * NSA_DK, _QT_ROWS + (k + 1) * NSA_DK)
            pieces += [t[rows, cs].astype(BF16), ones]
        vt_ref[c] = jnp.concatenate(pieces, axis=0)
        gt_ref[c] = t[_QT_ROWS + _VT_ROWS:_QT_ROWS + _VT_ROWS + NSA_GROUPS * N_GATES, cs]


def _in_proj(x2, g, w, wt, qfeat, cos_t, sin_t, kfeat, seq, tm=256):
    n, d = x2.shape
    nt = seq // tm
    widths = {k: hi - lo for k, (lo, hi) in _IN_COLS.items()}
    row = lambda c: pl.BlockSpec((tm, c), lambda i: (i, 0))
    tab = pl.BlockSpec((tm, LANES), lambda i: (i % nt, 0))
    outs = [("cq", F32), ("ckv", F32), ("kra", BF16), ("ksw", BF16)]
    n_kvc = widths["kvc"] // NSA_DK
    tiles = tm // NSA_TILE
    tspec = lambda rows: pl.BlockSpec((tiles, rows, NSA_TILE), lambda i: (i, 0, 0))
    return pl.pallas_call(
        _inproj_kernel,
        grid=(n // tm,),
        in_specs=[row(d), _resident((1, d)), _resident(w.shape), _resident(wt.shape), _resident(qfeat.shape),
                  tab, tab, tab],
        out_specs=[row(widths[k]) for k, _ in outs] + [
            pl.BlockSpec((n_kvc, tm, NSA_DK), lambda i: (0, i, 0)),
            tspec(_QT_ROWS), tspec(_VT_OUT), tspec(NSA_GROUPS * N_GATES)],
        out_shape=[jax.ShapeDtypeStruct((n, widths[k]), dt) for k, dt in outs] + [
            jax.ShapeDtypeStruct((n_kvc, n, NSA_DK), F32),
            jax.ShapeDtypeStruct((n // NSA_TILE, _QT_ROWS, NSA_TILE), BF16),
            jax.ShapeDtypeStruct((n // NSA_TILE, _VT_OUT, NSA_TILE), BF16),
            jax.ShapeDtypeStruct((n // NSA_TILE, NSA_GROUPS * N_GATES, NSA_TILE), F32)],
        compiler_params=_params(1),
        name="in_proj",
    )(x2, g, w, wt, qfeat, cos_t, sin_t, kfeat)


def _mlaup_kernel(cq_ref, ckv_ref, gq_ref, gkv_ref, wq_ref, wk_ref, wvt_ref, cos_ref, sin_ref, qn_ref, qr_ref,
                  kn_ref, vt_ref, *, scale):
    cqn = _rmsnorm(cq_ref[...], gq_ref[...]).astype(BF16)
    ckvn = _rmsnorm(ckv_ref[...], gkv_ref[...]).astype(BF16)
    hn = MLA_HEADS * MLA_NOPE
    hr = MLA_HEADS * MLA_ROPE
    qn_ref[...] = (_dot(cqn, wq_ref[:, :hn]) * scale).astype(BF16)
    a = _dot(cqn, wq_ref[:, hn:hn + hr])
    b = _dot(cqn, wq_ref[:, hn + hr:])
    cos = cos_ref[...]
    sin = sin_ref[...]
    for p in range(hr // LANES):
        sl = slice(p * LANES, (p + 1) * LANES)
        qr_ref[:, sl] = ((a[:, sl] * cos + b[:, sl] * sin) * scale).astype(BF16)
    kn_ref[...] = _dot(ckvn, wk_ref[...]).astype(BF16)
    v_t = _dot_nt(wvt_ref[...], ckvn).astype(BF16)
    ones = _ones_rows(MLA_TK)
    for c in range(vt_ref.shape[0]):
        pieces = []
        for h in range(MLA_HEADS):
            pieces += [v_t[h * MLA_V:(h + 1) * MLA_V, c * MLA_TK:(c + 1) * MLA_TK], ones]
        vt_ref[c] = jnp.concatenate(pieces, axis=0)


def _mla_up(cq, ckv, gq, gkv, wq, wk, wvt, cos_t, sin_t, seq, tm=512):
    n = cq.shape[0]
    nt = seq // tm
    hn = MLA_HEADS * MLA_NOPE
    hr = MLA_HEADS * MLA_ROPE
    hv = MLA_HEADS * (MLA_V + ONES_ROWS)
    row = lambda c: pl.BlockSpec((tm, c), lambda i: (i, 0))
    tab = pl.BlockSpec((tm, LANES), lambda i: (i % nt, 0))
    scale = (MLA_NOPE + MLA_ROPE) ** -0.5 * LOG2E
    return pl.pallas_call(
        functools.partial(_mlaup_kernel, scale=scale),
        grid=(n // tm,),
        in_specs=[row(MLA_Q_LORA), row(MLA_KV_LORA), _resident(gq.shape), _resident(gkv.shape),
                  _resident(wq.shape), _resident(wk.shape), _resident(wvt.shape), tab, tab],
        out_specs=[row(hn), row(hr), row(hn), pl.BlockSpec((tm // MLA_TK, hv, MLA_TK), lambda i: (i, 0, 0))],
        out_shape=[jax.ShapeDtypeStruct((n, hn), BF16), jax.ShapeDtypeStruct((n, hr), BF16),
                   jax.ShapeDtypeStruct((n, hn), BF16), jax.ShapeDtypeStruct((n // MLA_TK, hv, MLA_TK), BF16)],
        compiler_params=_params(1),
        name="mla_up",
    )(cq, ckv, gq, gkv, wq, wk, wvt, cos_t, sin_t)


def _mla_attn_kernel(qn_ref, qr_ref, kn_ref, kr_ref, vt_ref, o_ref, m_ref, acc_ref, s_scr, qt_scr, *, tq):
    i = pl.program_id(2)
    tk = MLA_TK
    lane = lax.broadcasted_iota(jnp.int32, (tq, LANES), 1)
    krow = lax.broadcasted_iota(jnp.int32, (tk, tq), 0)
    qcol = lax.broadcasted_iota(jnp.int32, (tk, tq), 1)
    qr = qr_ref[...]
    zero = jnp.zeros_like(qr)
    for hh in range(2):
        q_cat = jnp.concatenate([qn_ref[:, hh * LANES:(hh + 1) * LANES],
                                 jnp.where((lane < MLA_ROPE) == (hh == 0), qr, zero)], axis=1)
        qt_scr[hh] = q_cat.astype(F32).T.astype(BF16)
    m_ref[...] = jnp.full_like(m_ref, NEG)
    acc_ref[...] = jnp.zeros_like(acc_ref)
    vrows = MLA_V + ONES_ROWS

    def scores(j, slot):
        ks = pl.ds(pl.multiple_of(j * tk, tk), tk)
        k_rope = kr_ref[ks, :]
        for hh in range(2):
            hs = slice(hh * LANES, (hh + 1) * LANES)
            s_scr[slot, hh] = _dot(jnp.concatenate([kn_ref[ks, hs], k_rope], axis=1), qt_scr[hh])

    def consume(j, slot, diagonal):
        weights = []
        for hh in range(2):
            s = s_scr[slot, hh]
            if diagonal:
                s = jnp.where(j * tk + krow <= i * tq + qcol, s, NEG)
            weights.append(_flash_weights(s, m_ref, hh))
        for hh in range(2):
            alpha, p = weights[hh]
            acc_ref[hh] = alpha * acc_ref[hh] + _dot(vt_ref[0, j, hh * vrows:(hh + 1) * vrows, :], p)

    def body(u, carry):
        j = 2 * u
        scores(j + 1, 1)
        consume(j, 0, False)
        scores(j + 2, 0)
        consume(j + 1, 1, False)
        return carry

    assert tq == tk, "the last key tile of a query tile must be its only diagonal tile"
    scores(0, 0)
    lax.fori_loop(0, i // 2, body, 0)

    @pl.when(i % 2 == 1)
    def _():
        scores(i, 1)
        consume(i - 1, 0, False)
        consume(i, 1, True)

    @pl.when(i % 2 == 0)
    def _():
        consume(i, 0, True)

    for hh in range(2):
        o_t = acc_ref[hh, :MLA_V] / acc_ref[hh, MLA_V:MLA_V + 1]
        for c in range(tq // LANES):
            o_ref[c * LANES:(c + 1) * LANES, hh * LANES:(hh + 1) * LANES] = (
                o_t[:, c * LANES:(c + 1) * LANES].T.astype(BF16))


def _mla_attn(qn, qr, kn, kr, vt, batch, seq, tq=512):
    n = qn.shape[0]
    nq = seq // tq
    nk = seq // MLA_TK
    pairs = MLA_HEADS // 2
    pw = 2 * LANES
    qspec = lambda c: pl.BlockSpec((tq, c), lambda b, p, i: (b * nq + i, p))
    vrows = MLA_V + ONES_ROWS
    vt4 = vt.reshape(batch, nk, MLA_HEADS * vrows, MLA_TK)
    return pl.pallas_call(
        functools.partial(_mla_attn_kernel, tq=tq),
        grid=(batch, pairs, nq),
        in_specs=[qspec(pw), qspec(LANES), pl.BlockSpec((seq, pw), lambda b, p, i: (b, p)),
                  pl.BlockSpec((seq, LANES), lambda b, p, i: (b, 0)),
                  pl.BlockSpec((1, nk, 2 * vrows, MLA_TK), lambda b, p, i: (b, 0, p, 0))],
        out_specs=qspec(pw),
        out_shape=jax.ShapeDtypeStruct((n, MLA_HEADS * MLA_V), BF16),
        scratch_shapes=[pltpu.VMEM((2, 1, tq), F32), pltpu.VMEM((2, vrows, tq), F32),
                        pltpu.VMEM((2, 2, MLA_TK, tq), F32), pltpu.VMEM((2, 2 * LANES, tq), BF16)],
        compiler_params=_params(3),
        name="mla_attn",
    )(qn, qr, kn, kr, vt4)


def _compress_kernel(r_ref, pos_ref, w1_ref, w2_ref, w2t_ref, feat_ref, o_ref, ot_ref):
    nc = o_ref.shape[3]
    a = jnp.zeros((nc, CMP_HIDDEN), F32)
    b = jnp.zeros((nc, CMP_HIDDEN), F32)
    for l in range(CMP_STRIDE):
        x = r_ref[0, pl.ds(l, nc, stride=CMP_STRIDE), :]
        lo = slice(l * NSA_DK, (l + 1) * NSA_DK)
        hi = slice((CMP_STRIDE + l) * NSA_DK, (CMP_STRIDE + l + 1) * NSA_DK)
        a = a + _dot((x + pos_ref[0, l:l + 1, :]).astype(BF16), w1_ref[0, lo, :])
        b = b + _dot((x + pos_ref[0, CMP_STRIDE + l:CMP_STRIDE + l + 1, :]).astype(BF16), w1_ref[0, hi, :])
    hid = a + pltpu.roll(b, nc - 1, 0)
    act = (hid * jax.nn.sigmoid(hid)).astype(BF16)
    o_ref[0, 0, 0] = (_dot(act, w2_ref[0]) + feat_ref[...]).astype(BF16)
    ot_ref[0, 0, 0] = _dot_nt(w2t_ref[0], act).astype(BF16)


def _compress(kvc, pos, w1, w2, w2t, feat, batch, seq):
    groups = NSA_GROUPS
    nc = seq // CMP_STRIDE
    return pl.pallas_call(
        _compress_kernel,
        grid=(2, batch, groups),
        in_specs=[pl.BlockSpec((1, seq, NSA_DK), lambda t, b, g: (t * groups + g, b, 0)),
                  pl.BlockSpec((1, CMP_LEN, NSA_DK), lambda t, b, g: (t, 0, 0)),
                  pl.BlockSpec((1, CMP_LEN * NSA_DK, CMP_HIDDEN), lambda t, b, g: (t, 0, 0)),
                  pl.BlockSpec((1, CMP_HIDDEN, LANES), lambda t, b, g: (t, 0, 0)),
                  pl.BlockSpec((1, NSA_DK, CMP_HIDDEN), lambda t, b, g: (t, 0, 0)),
                  pl.BlockSpec((nc, LANES), lambda t, b, g: (0, 0))],
        out_specs=[pl.BlockSpec((1, 1, 1, nc, LANES), lambda t, b, g: (t, b, g, 0, 0)),
                   pl.BlockSpec((1, 1, 1, NSA_DK, nc), lambda t, b, g: (t, b, g, 0, 0))],
        out_shape=[jax.ShapeDtypeStruct((2, batch, groups, nc, LANES), BF16),
                   jax.ShapeDtypeStruct((2, batch, groups, NSA_DK, nc), BF16)],
        compiler_params=_params(3),
        name="compress",
    )(kvc, pos, w1, w2, w2t, feat)


def _nsa_cmp_kernel(q_ref, kc_ref, vaug_ref, gt_ref, oc_ref, selt_ref, flag_ref, q_scr, m_ref, acc_ref, s_scr,
                    *, n_cmp):
    i = pl.program_id(2)
    t = NSA_TILE
    tq = CMP_TQ
    hpg = NSA_HPG
    _stack_heads(q_ref, q_scr)
    m_ref[...] = jnp.full_like(m_ref, NEG)
    acc_ref[...] = jnp.zeros_like(acc_ref)

    crow = lax.broadcasted_iota(jnp.int32, (t, tq), 0)
    qpos = i * tq + lax.broadcasted_iota(jnp.int32, (t, tq), 1)

    def keepf(j):
        c = j * t + crow
        return jnp.where(CMP_STRIDE * c + (CMP_LEN - 1) <= qpos, jnp.where(c < n_cmp, 1.0, 0.0), 0.0)

    count = jnp.minimum((i * tq + tq - CMP_LEN) // (CMP_STRIDE * t) + 1, kc_ref.shape[3] // t)
    _skewed_flash(count, lambda n: n, keepf, lambda j: kc_ref[0, 0, 0, pl.ds(pl.multiple_of(j * t, t), t), :],
                  lambda j: vaug_ref[0, 0, j], q_scr, s_scr, m_ref, acc_ref, 0)

    some = jnp.where(qpos[:1] >= CMP_LEN - 1, 1.0, 0.0)
    inv = jnp.concatenate([some] * hpg, axis=1) / acc_ref[0, NSA_DK + NS_PAD:NSA_DK + NS_PAD + 1]
    gates = jax.nn.sigmoid(jnp.concatenate([gt_ref[c] for c in range(tq // t)], axis=1))
    imp = jnp.zeros((NS_PAD, tq), F32)
    for pair in range(hpg // 2):
        halves = []
        for h in (2 * pair, 2 * pair + 1):
            hs = slice(h * tq, (h + 1) * tq)
            halves.append(gates[3 * h:3 * h + 1, :] * (acc_ref[0, :NSA_DK, hs] * inv[:, hs]))
            imp = imp + acc_ref[0, NSA_DK:NSA_DK + NS_PAD, hs] * inv[:, hs]
        oc_ref[:, pair * LANES:(pair + 1) * LANES] = jnp.concatenate(halves, axis=0).T.astype(BF16)

    blk = lax.broadcasted_iota(jnp.int32, (NS_PAD, tq), 0)
    blk_t = (i * tq + lax.broadcasted_iota(jnp.int32, (NS_PAD, tq), 1)) // SLC_LEN
    forced = jnp.where(blk == 0, 1.0, 0.0) + jnp.where(blk == blk_t, 1.0, 0.0) + jnp.where(blk == blk_t - 1, 1.0, 0.0)
    imp = jnp.where(forced > 0.5, FORCE_SCORE, jnp.where(blk <= blk_t, imp, -1.0))
    blkf = blk.astype(F32)
    sel = jnp.zeros((NS_PAD, tq), F32)
    for _ in range(SLC_TOPK):
        best = jnp.max(imp, axis=0, keepdims=True)
        pick = jnp.min(jnp.where(imp == best, blkf, float(NS_PAD)), axis=0, keepdims=True)
        hit = blkf == pick
        sel = jnp.where(hit, 1.0, sel)
        imp = jnp.where(hit, -3e38, imp)
    selt_ref[0, 0] = sel
    for c in range(tq // t):
        flag_ref[c] = jnp.max(sel[:, c * t:(c + 1) * t].T, axis=0, keepdims=True).astype(jnp.int32)


def _nsa_cmp(q, kc, vaug, gt, batch, seq, n_cmp):
    t = NSA_TILE
    tq = CMP_TQ
    nq = seq // tq
    n = batch * seq
    ncp = kc.shape[3]
    sub = tq // t
    rows = lambda c: pl.BlockSpec((tq, c), lambda b, g, i: (b * nq + i, g))
    return pl.pallas_call(
        functools.partial(_nsa_cmp_kernel, n_cmp=n_cmp),
        grid=(batch, NSA_GROUPS, nq),
        in_specs=[pl.BlockSpec((sub, NSA_HPG * LANES, t), lambda b, g, i: (b * nq + i, g, 0)),
                  pl.BlockSpec((1, 1, 1, ncp, LANES), lambda b, g, i: (0, b, g, 0, 0)),
                  pl.BlockSpec((1, 1) + vaug.shape[2:], lambda b, g, i: (b, g, 0, 0, 0)),
                  pl.BlockSpec((sub, N_GATES, t), lambda b, g, i: (b * nq + i, g, 0))],
        out_specs=[rows(NSA_HPG * NSA_DK),
                   pl.BlockSpec((1, 1, NS_PAD, tq), lambda b, g, i: (b, g, 0, i)),
                   pl.BlockSpec((sub, 1, NS_PAD), lambda b, g, i: ((b * NSA_GROUPS + g) * nq + i, 0, 0))],
        out_shape=[jax.ShapeDtypeStruct((n, NSA_HEADS * NSA_DK), BF16),
                   jax.ShapeDtypeStruct((batch, NSA_GROUPS, NS_PAD, seq), F32),
                   jax.ShapeDtypeStruct((batch * NSA_GROUPS * seq // t, 1, NS_PAD), jnp.int32)],
        scratch_shapes=[pltpu.VMEM((LANES, NSA_HPG * tq), BF16), pltpu.VMEM((1, 1, NSA_HPG * tq), F32),
                        pltpu.VMEM((1, NSA_DK + NS_PAD + ONES_ROWS, NSA_HPG * tq), F32),
                        pltpu.VMEM((2, 2 * t, NSA_HPG * tq), F32)],
        compiler_params=_params(3),
        name="nsa_cmp",
    )(q, kc, vaug, gt)


IDS_PER_WORD = 4


def _nsa_attn_kernel(counts_ref, words_ref, q_ref, selt_ref, oc_ref, gt_ref, ks_ref, kw_ref, vst_ref, vwt_ref, o_ref,
                     q_scr, m_ref, acc_ref, s_scr, *, nq, words_per_step):
    b = pl.program_id(0)
    g = pl.program_id(1)
    i = pl.program_id(2)
    t = NSA_TILE
    tq = ATT_TQ
    sub = tq // t
    hpg = NSA_HPG
    _stack_heads(q_ref, q_scr)
    m_ref[...] = jnp.full_like(m_ref, NEG)
    acc_ref[...] = jnp.zeros_like(acc_ref)

    krow = lax.broadcasted_iota(jnp.int32, (t, tq), 0)
    qpos = i * tq + lax.broadcasted_iota(jnp.int32, (t, tq), 1)
    blocks_per_tile = t // SLC_LEN
    slc, win = 0, 1
    step = (b * NSA_GROUPS + g) * nq + i

    def run_branch(branch, count, tile_at, keepf_fn, k_ref, vt_ref):
        _skewed_flash(count, tile_at, keepf_fn, lambda j: k_ref[pl.ds(pl.multiple_of(j * t, t), t), :],
                      lambda j: vt_ref[0, j], q_scr, s_scr, m_ref, acc_ref, branch)

    def slc_tile(n):
        word = words_ref[step * words_per_step + n // IDS_PER_WORD]
        return lax.shift_right_logical(word, 8 * (n % IDS_PER_WORD)) & 255

    def slc_keepf(j):
        dist = qpos - (j * t + krow)
        sel_rows = [selt_ref[0, 0, pl.ds(blocks_per_tile * j + r, 1), :] for r in range(blocks_per_tile)]
        picked = sel_rows[-1]
        for r in range(blocks_per_tile - 2, -1, -1):
            picked = jnp.where(krow < (r + 1) * SLC_LEN, sel_rows[r], picked)
        return jnp.where(dist >= 0, picked, 0.0)

    def win_keepf(j):
        dist = qpos - (j * t + krow)
        return jnp.where(dist >= 0, jnp.where(dist < WINDOW, 1.0, 0.0), 0.0)

    run_branch(slc, counts_ref[step], slc_tile, slc_keepf, ks_ref, vst_ref)
    first = jnp.maximum(i * sub - WINDOW // t, 0)
    run_branch(win, (i + 1) * sub - first, lambda n: first + n, win_keepf, kw_ref, vwt_ref)

    gates = jax.nn.sigmoid(jnp.concatenate([gt_ref[c] for c in range(sub)], axis=1))
    inv_s = 1.0 / acc_ref[slc, NSA_DK:NSA_DK + 1]
    inv_w = 1.0 / acc_ref[win, NSA_DK:NSA_DK + 1]
    for pair in range(hpg // 2):
        halves = []
        for h in (2 * pair, 2 * pair + 1):
            hs = slice(h * tq, (h + 1) * tq)
            halves.append(gates[3 * h + 1:3 * h + 2, :] * (acc_ref[slc, :NSA_DK, hs] * inv_s[:, hs])
                          + gates[3 * h + 2:3 * h + 3, :] * (acc_ref[win, :NSA_DK, hs] * inv_w[:, hs]))
        ps = slice(pair * LANES, (pair + 1) * LANES)
        o_ref[:, ps] = (oc_ref[:, ps].astype(F32) + jnp.concatenate(halves, axis=0).T).astype(BF16)


def _nsa_attn(counts, words, q, selt, oc, gt, ksw, vt, batch, seq):
    t = NSA_TILE
    tq = ATT_TQ
    sub = tq // t
    nq = seq // tq
    nk = seq // t
    n = batch * seq
    words_per_step = words.shape[0] // (batch * NSA_GROUPS * nq)
    rows = lambda c: pl.BlockSpec((tq, c), lambda b, g, i, *_: (b * nq + i, g))
    tiles = lambda r: pl.BlockSpec((sub, r, t), lambda b, g, i, *_: (b * nq + i, g, 0))
    key = lambda which: pl.BlockSpec((seq, LANES), lambda b, g, i, *_: (b, which * NSA_GROUPS + g))
    vt4 = vt.reshape(batch, nk, _VT_OUT, t)
    vrows = NSA_DK + ONES_ROWS
    val = lambda which: pl.BlockSpec((1, nk, vrows, t), lambda b, g, i, *_: (b, 0, which * NSA_GROUPS + g, 0))
    cols = NSA_HPG * tq
    grid_spec = pltpu.PrefetchScalarGridSpec(
        num_scalar_prefetch=2,
        grid=(batch, NSA_GROUPS, nq),
        in_specs=[tiles(NSA_HPG * LANES),
                  pl.BlockSpec((1, 1, NS_PAD, tq), lambda b, g, i, *_: (b, g, 0, i)),
                  rows(NSA_HPG * NSA_DK), tiles(N_GATES), key(0), key(1), val(0), val(1)],
        out_specs=rows(NSA_HPG * NSA_DK),
        scratch_shapes=[pltpu.VMEM((LANES, cols), BF16), pltpu.VMEM((2, 1, cols), F32),
                        pltpu.VMEM((2, vrows, cols), F32), pltpu.VMEM((2, 2 * t, cols), F32)],
    )
    return pl.pallas_call(
        functools.partial(_nsa_attn_kernel, nq=nq, words_per_step=words_per_step),
        grid_spec=grid_spec,
        out_shape=jax.ShapeDtypeStruct((n, NSA_HEADS * NSA_DK), BF16),
        compiler_params=_params(3),
        name="nsa_attn",
    )(counts, words, q, selt, oc, gt, ksw, ksw, vt4, vt4)


def _outproj_kernel(x_ref, om_ref, on_ref, wm_ref, wn_ref, o_ref):
    o_ref[...] = x_ref[...] + _dot(om_ref[...], wm_ref[...]) + _dot(on_ref[...], wn_ref[...])


def _out_proj(x2, o_mla, o_nsa, wm, wn, tm=512):
    n, d = x2.shape
    row = lambda c: pl.BlockSpec((tm, c), lambda i: (i, 0))
    return pl.pallas_call(
        _outproj_kernel,
        grid=(n // tm,),
        in_specs=[row(d), row(o_mla.shape[1]), row(o_nsa.shape[1]), _resident(wm.shape), _resident(wn.shape)],
        out_specs=row(d),
        out_shape=jax.ShapeDtypeStruct((n, d), F32),
        compiler_params=_params(1),
        name="out_proj",
    )(x2, o_mla, o_nsa, wm, wn)


def _ffn_kernel(x_ref, g_ref, wg_ref, wu_ref, wd_ref, gf_ref, o_ref, h_scr, acc_scr):
    f = pl.program_id(1)

    @pl.when(f == 0)
    def _():
        h_scr[...] = _rmsnorm(x_ref[...], g_ref[...]).astype(BF16)
        acc_scr[...] = jnp.zeros_like(acc_scr)

    h = h_scr[...]
    gate = _dot(h, wg_ref[...])
    act = (gate * jax.nn.sigmoid(gate)) * _dot(h, wu_ref[...])
    acc_scr[...] += _dot(act.astype(BF16), wd_ref[...])

    @pl.when(f == pl.num_programs(1) - 1)
    def _():
        o_ref[...] = _rmsnorm(x_ref[...] + acc_scr[...], gf_ref[...])


def _ffn(x1, g, wg, wu, wd, gf, tm=512, tf=512):
    n, d = x1.shape
    dff = wg.shape[1]
    return pl.pallas_call(
        _ffn_kernel,
        grid=(n // tm, dff // tf),
        in_specs=[pl.BlockSpec((tm, d), lambda i, f: (i, 0)), _resident((1, d)),
                  pl.BlockSpec((d, tf), lambda i, f: (0, f)), pl.BlockSpec((d, tf), lambda i, f: (0, f)),
                  pl.BlockSpec((tf, d), lambda i, f: (f, 0)), _resident((1, d))],
        out_specs=pl.BlockSpec((tm, d), lambda i, f: (i, 0)),
        out_shape=jax.ShapeDtypeStruct((n, d), F32),
        scratch_shapes=[pltpu.VMEM((tm, d), BF16), pltpu.VMEM((tm, d), F32)],
        compiler_params=_params(2),
        name="ffn",
    )(x1, g, wg, wu, wd, gf)


def _pad_cols(w, width):
    return jnp.pad(w, ((0, 0), (0, width - w.shape[1])))


def _rot_cols(w):
    half = w.shape[1] // 2
    return jnp.concatenate([-w[:, half:], w[:, :half]], axis=1)


def _fused_in_weights(w_in):
    sizes = (MLA_Q_LORA, MLA_KV_LORA, MLA_ROPE, NSA_HEADS * NSA_DK) + (NSA_GROUPS * NSA_DK,) * 6 + (3 * NSA_HEADS,)
    offs = np.cumsum(sizes)[:-1].tolist()
    cq, ckv, kr, q, kc, vc, ks, vs, kw, vw, gate = jnp.split(w_in, offs, axis=1)
    d = w_in.shape[0]
    q_pad = _pad_cols((q * (NSA_DK ** -0.5 * LOG2E)).reshape(d * NSA_HEADS, NSA_DK), LANES)
    q_pad = q_pad.reshape(d, NSA_HEADS * LANES)
    per_group = lambda w, c: _pad_cols(w.reshape(d * NSA_GROUPS, c), LANES).reshape(d, NSA_GROUPS * LANES)
    kr_rot = _rot_cols(kr)
    cols = [cq, ckv, kr, kr, kr_rot, kr_rot, kc, vc, per_group(ks, NSA_DK), per_group(kw, NSA_DK)]
    w = jnp.concatenate(cols, axis=1).astype(BF16)
    assert w.shape[1] == _IN_COLS["ksw"][1]
    rows_t = jnp.concatenate([q_pad, vs, vw, gate], axis=1).T
    pad = -rows_t.shape[0] % 16
    return w, jnp.pad(rows_t, ((0, pad), (0, 0))).astype(BF16)


def _slope_features(slopes2):
    s1 = slopes2.astype(BF16).astype(F32)
    s2 = (slopes2 - s1).astype(BF16).astype(F32)
    s3 = (slopes2 - s1 - s2).astype(BF16).astype(F32)
    pieces = jnp.stack([s1, s2, s3, s1, s2, s3], axis=1)
    column = jnp.pad(pieces, ((0, 0), (NSA_DK, LANES - NSA_DK - 6))).reshape(-1, 1)
    return jnp.broadcast_to(column, (column.shape[0], NSA_TILE))


def _position_features(pos):
    hi = (POS_SPLIT * (pos // POS_SPLIT)).astype(F32)
    lo = (pos % POS_SPLIT).astype(F32)
    return jnp.pad(jnp.stack([hi, hi, hi, lo, lo, lo], axis=1), ((0, 0), (NSA_DK, LANES - NSA_DK - 6)))


def _fused_uq_weight(w_uq):
    d = w_uq.shape[0]
    w = w_uq.reshape(d, MLA_HEADS, MLA_NOPE + MLA_ROPE)
    nope = w[:, :, :MLA_NOPE].reshape(d, MLA_HEADS * MLA_NOPE)
    rope = w[:, :, MLA_NOPE:]
    rope_rot = jnp.concatenate([-rope[:, :, MLA_ROPE // 2:], rope[:, :, :MLA_ROPE // 2]], axis=2)
    flat = lambda r: r.reshape(d, MLA_HEADS * MLA_ROPE)
    return jnp.concatenate([nope, flat(rope), flat(rope_rot)], axis=1).astype(BF16)


def _rope_tables(seq):
    inv = ROPE_THETA ** (-jnp.arange(0, MLA_ROPE, 2, dtype=F32) / MLA_ROPE)
    ang = jnp.arange(seq, dtype=F32)[:, None] * inv[None, :]
    reps = 2 * LANES // MLA_ROPE
    return jnp.tile(jnp.cos(ang), (1, reps)), jnp.tile(jnp.sin(ang), (1, reps))


def _overlap_matrix(seq, n_cmp, ncp):
    cmp_start = CMP_STRIDE * np.arange(n_cmp)
    slc_start = SLC_LEN * np.arange(seq // SLC_LEN)
    ov = np.clip(np.minimum(cmp_start[:, None] + CMP_LEN, slc_start[None, :] + SLC_LEN)
                 - np.maximum(cmp_start[:, None], slc_start[None, :]), 0, None).astype(np.float32) / CMP_STRIDE
    out = np.zeros((NS_PAD, ncp), np.float32)
    out[:ov.shape[1], :n_cmp] = ov.T
    return jnp.asarray(out, BF16)


def _active_tiles(flags, seq):
    t_blocks = NSA_TILE // SLC_LEN
    sub = ATT_TQ // NSA_TILE
    nq = seq // ATT_TQ
    steps = flags.shape[0] // sub
    tiles = flags.reshape(steps, sub, NS_PAD // t_blocks, t_blocks).max(axis=(1, 3))
    tile_id = jnp.arange(tiles.shape[1], dtype=jnp.int32)[None, :]
    last_tile = ((jnp.arange(steps, dtype=jnp.int32) % nq) * sub + (sub - 1))[:, None]
    active = (tiles > 0) & (tile_id <= last_tile)
    order = jnp.argsort(jnp.where(active, tile_id, tile_id + tiles.shape[1]), axis=-1).astype(jnp.int32)
    shifts = 8 * jnp.arange(IDS_PER_WORD, dtype=jnp.int32)
    words = (order.reshape(steps, -1, IDS_PER_WORD) << shifts).sum(axis=-1).astype(jnp.int32)
    return active.sum(axis=-1).astype(jnp.int32), words.reshape(-1)


def kernel(x, attn_norm_g, w_in, mla_q_norm_g, mla_kv_norm_g, w_uq, w_uk, w_uv, cmp_pos_k, cmp_pos_v, w_cmp_k1,
           w_cmp_k2, w_cmp_v1, w_cmp_v2, w_o, ffn_norm_g, w_gate, w_up, w_down, final_norm_g):
    batch, seq, d = x.shape
    n = batch * seq
    assert w_in.shape[0] == 1, "the final RMSNorm is fused into the FFN kernel of a single layer"
    assert seq % (CMP_STRIDE * NSA_TILE) == 0 and seq // SLC_LEN <= NS_PAD and seq // POS_SPLIT <= 256
    n_cmp = (seq - CMP_LEN) // CMP_STRIDE + 1
    ncp = seq // CMP_STRIDE
    cos_t, sin_t = _rope_tables(seq)
    ov_t = _overlap_matrix(seq, n_cmp, ncp)
    slopes2 = 2.0 ** (-8.0 * jnp.arange(1, NSA_HEADS + 1, dtype=F32) / NSA_HEADS) * LOG2E
    qfeat = _slope_features(slopes2)
    kfeat = _position_features(jnp.arange(seq, dtype=jnp.int32))
    x2 = x.reshape(n, d)

    w_fused, w_fused_t = _fused_in_weights(w_in[0])
    cq, ckv, kr, ksw, kvc, q, vt, gt = _in_proj(
        x2, attn_norm_g[0][None], w_fused, w_fused_t, qfeat, cos_t, sin_t, kfeat, seq)

    qn, qr, kn, vt_mla = _mla_up(cq, ckv, mla_q_norm_g[0][None], mla_kv_norm_g[0][None], _fused_uq_weight(w_uq[0]),
                                 w_uk[0].astype(BF16), w_uv[0].T.astype(BF16), cos_t, sin_t, seq)
    o_mla = _mla_attn(qn, qr, kn, kr, vt_mla, batch, seq)

    pos = jnp.stack([cmp_pos_k[0], cmp_pos_v[0]])
    w1 = jnp.stack([w_cmp_k1[0], w_cmp_v1[0]]).astype(BF16)
    w2 = jnp.stack([_pad_cols(w_cmp_k2[0], LANES), _pad_cols(w_cmp_v2[0], LANES)]).astype(BF16)
    w2t = jnp.stack([w_cmp_k2[0].T, w_cmp_v2[0].T]).astype(BF16)
    cmp_end = CMP_STRIDE * jnp.arange(ncp, dtype=jnp.int32) + (CMP_LEN - 1)
    cmp_rows, cmp_t = _compress(kvc, pos, w1, w2, w2t, _position_features(cmp_end), batch, seq)
    ones = jnp.zeros((ONES_ROWS, ncp), BF16).at[0].set(1.0)
    const_rows = jnp.broadcast_to(jnp.concatenate([ov_t, ones]), (batch, NSA_GROUPS, NS_PAD + ONES_ROWS, ncp))
    vaug = jnp.concatenate([cmp_t[1], const_rows], axis=2)
    vaug = vaug.reshape(batch, NSA_GROUPS, vaug.shape[2], ncp // NSA_TILE, NSA_TILE).transpose(0, 1, 3, 2, 4)

    oc, selt, flags = _nsa_cmp(q, cmp_rows, vaug, gt, batch, seq, n_cmp)
    counts, words = _active_tiles(flags.reshape(flags.shape[0], NS_PAD), seq)
    o_nsa = _nsa_attn(counts, words, q, selt, oc, gt, ksw, vt, batch, seq)

    split = MLA_HEADS * MLA_V
    x1 = _out_proj(x2, o_mla, o_nsa, w_o[0][:split].astype(BF16), w_o[0][split:].astype(BF16))
    out = _ffn(x1, ffn_norm_g[0][None], w_gate[0].astype(BF16), w_up[0].astype(BF16), w_down[0].astype(BF16),
               final_norm_g[None])
    return out.reshape(batch, seq, d)
```

```python
import functools
import math

import numpy as np
import jax
import jax.numpy as jnp
from jax import lax
from jax.experimental import pallas as pl
from jax.experimental.pallas import tpu as pltpu

F32 = jnp.float32
BF16 = jnp.bfloat16

EPS = 1e-6
NEG = -1e30
LOG2E = math.log2(math.e)
LANES = 128

MLA_HEADS = 8
MLA_Q_LORA = 512
MLA_KV_LORA = 256
MLA_NOPE = 128
MLA_ROPE = 64
MLA_V = 128
ROPE_THETA = 10000.0
MLA_TK = 512

NSA_HEADS = 16
NSA_GROUPS = 2
NSA_HPG = NSA_HEADS // NSA_GROUPS
NSA_DK = 64
CMP_LEN = 32
CMP_STRIDE = 16
CMP_HIDDEN = 128
SLC_LEN = 64
SLC_TOPK = 16
WINDOW = 512
FORCE_SCORE = 1e4
TAKEN = -3e38
NSA_TILE = 128
NS_PAD = 128
CMP_TQ = 512
ATT_TQ = 256
N_GATES = 3 * NSA_HPG
POS_SPLIT = 64
BF16_EXACT_INT = 256
ONES_ROWS = 16

VMEM_LIMIT = 56 * 1024 * 1024


def _params(n_axes):
    return pltpu.CompilerParams(dimension_semantics=("arbitrary",) * n_axes, vmem_limit_bytes=VMEM_LIMIT)


def _resident(shape):
    zeros = (0,) * len(shape)
    return pl.BlockSpec(shape, lambda *_: zeros, pipeline_mode=pl.Buffered(1))


def _rmsnorm(x, g):
    return x * lax.rsqrt(jnp.mean(x * x, axis=-1, keepdims=True) + EPS) * g


def _dot(a, b):
    return jnp.dot(a, b, preferred_element_type=F32)


def _dot_nt(a, b):
    return lax.dot_general(a, b, (((1,), (1,)), ((), ())), preferred_element_type=F32)


def _ones_rows(width):
    return jnp.where(lax.broadcasted_iota(jnp.int32, (ONES_ROWS, width), 0) == 0, 1.0, 0.0).astype(BF16)


def _flash_weights(s, m_ref, idx):
    m_prev = m_ref[idx]
    m_new = jnp.maximum(m_prev, jnp.max(s, axis=0, keepdims=True))
    m_ref[idx] = m_new
    return jnp.exp2(m_prev - m_new), jnp.exp2(s - m_new).astype(BF16)


def _stack_heads(qt_ref, qt_scr):
    tiles, rows, t = qt_ref.shape
    for h in range(rows // LANES):
        for c in range(tiles):
            qt_scr[:, (h * tiles + c) * t:(h * tiles + c + 1) * t] = qt_ref[c, h * LANES:(h + 1) * LANES, :]


def _flash_update(s, v_t, m_ref, acc_ref, idx):
    alpha, p = _flash_weights(s, m_ref, idx)
    acc_ref[idx] = alpha * acc_ref[idx] + _dot(v_t, p)


def _skewed_flash(count, tile_at, keepf_fn, k_tile, vt_tile, qt_scr, s_scr, m_ref, acc_ref, idx):
    pairs = (count + 1) // 2

    def pair(u):
        return [tile_at(jnp.minimum(2 * u + r, count - 1)) for r in range(2)]

    def scores(u, slot):
        j0, j1 = pair(jnp.minimum(u, pairs - 1))
        s_scr[slot] = _dot(jnp.concatenate([k_tile(j0), k_tile(j1)], axis=0), qt_scr[...])

    def consume(u, slot):
        j0, j1 = pair(u)
        second = jnp.where(2 * u + 1 < count, 1.0, 0.0)
        bias = jnp.concatenate([jnp.where(keepf_fn(j0) > 0.5, 0.0, NEG),
                                jnp.where(keepf_fn(j1) * second > 0.5, 0.0, NEG)], axis=0)
        s = s_scr[slot]
        s = s + jnp.concatenate([bias] * (s.shape[1] // bias.shape[1]), axis=1)
        _flash_update(s, jnp.concatenate([vt_tile(j0), vt_tile(j1)], axis=1), m_ref, acc_ref, idx)

    def body(w, carry):
        u = 2 * w
        scores(u + 1, 1)
        consume(u, 0)
        scores(u + 2, 0)
        consume(u + 1, 1)
        return carry

    scores(0, 0)
    lax.fori_loop(0, pairs // 2, body, 0)

    @pl.when(pairs % 2 == 1)
    def _():
        consume(pairs - 1, 0)


_IN_COLS = dict(cq=(0, 512), ckv=(512, 768), kra=(768, 896), krb=(896, 1024), kvc=(1024, 1280), ksw=(1280, 1792))
_QT_ROWS = NSA_HEADS * LANES
_VT_ROWS = 2 * NSA_GROUPS * NSA_DK
_VT_OUT = _VT_ROWS // NSA_DK * (NSA_DK + ONES_ROWS)


def _inproj_kernel(x_ref, g_ref, w_ref, wt_ref, qfeat_ref, cos_ref, sin_ref, kfeat_ref, cq_ref, ckv_ref, kr_ref,
                   ksw_ref, kvc_ref, qt_ref, vt_ref, gt_ref):
    h = _rmsnorm(x_ref[...], g_ref[...]).astype(BF16)

    def mm(name):
        lo, hi = _IN_COLS[name]
        return _dot(h, w_ref[:, lo:hi])

    cq_ref[...] = mm("cq")
    ckv_ref[...] = mm("ckv")
    kr_ref[...] = (mm("kra") * cos_ref[...] + mm("krb") * sin_ref[...]).astype(BF16)
    kvc = mm("kvc")
    for k in range(kvc_ref.shape[0]):
        kvc_ref[k] = kvc[:, k * NSA_DK:(k + 1) * NSA_DK]
    kfeat = kfeat_ref[...]
    ksw_ref[...] = (mm("ksw") + jnp.concatenate([kfeat] * (2 * NSA_GROUPS), axis=1)).astype(BF16)
    t = _dot_nt(wt_ref[...], h)
    ones = _ones_rows(NSA_TILE)
    qfeat = qfeat_ref[...]
    for c in range(vt_ref.shape[0]):
        cs = slice(c * NSA_TILE, (c + 1) * NSA_TILE)
        qt_ref[c] = (t[:_QT_ROWS, cs] + qfeat).astype(BF16)
        pieces = []
        for k in range(_VT_ROWS // NSA_DK):
            rows = slice(_QT_ROWS + k * NSA_DK, _QT_ROWS + (k + 1) * NSA_DK)
            pieces += [t[rows, cs].astype(BF16), ones]
        vt_ref[c] = jnp.concatenate(pieces, axis=0)
        gt_ref[c] = t[_QT_ROWS + _VT_ROWS:_QT_ROWS + _VT_ROWS + NSA_GROUPS * N_GATES, cs]


def _in_proj(x2, g, w, wt, qfeat, cos_t, sin_t, kfeat, seq, tm=256):
    n, d = x2.shape
    nt = seq // tm
    widths = {k: hi - lo for k, (lo, hi) in _IN_COLS.items()}
    row = lambda c: pl.BlockSpec((tm, c), lambda i: (i, 0))
    tab = pl.BlockSpec((tm, LANES), lambda i: (i % nt, 0))
    outs = [("cq", F32), ("ckv", F32), ("kra", BF16), ("ksw", BF16)]
    n_kvc = widths["kvc"] // NSA_DK
    tiles = tm // NSA_TILE
    tspec = lambda rows: pl.BlockSpec((tiles, rows, NSA_TILE), lambda i: (i, 0, 0))
    return pl.pallas_call(
        _inproj_kernel,
        grid=(n // tm,),
        in_specs=[row(d), _resident((1, d)), _resident(w.shape), _resident(wt.shape), _resident(qfeat.shape),
                  tab, tab, tab],
        out_specs=[row(widths[k]) for k, _ in outs] + [
            pl.BlockSpec((n_kvc, tm, NSA_DK), lambda i: (0, i, 0)),
            tspec(_QT_ROWS), tspec(_VT_OUT), tspec(NSA_GROUPS * N_GATES)],
        out_shape=[jax.ShapeDtypeStruct((n, widths[k]), dt) for k, dt in outs] + [
            jax.ShapeDtypeStruct((n_kvc, n, NSA_DK), F32),
            jax.ShapeDtypeStruct((n // NSA_TILE, _QT_ROWS, NSA_TILE), BF16),
            jax.ShapeDtypeStruct((n // NSA_TILE, _VT_OUT, NSA_TILE), BF16),
            jax.ShapeDtypeStruct((n // NSA_TILE, NSA_GROUPS * N_GATES, NSA_TILE), F32)],
        compiler_params=_params(1),
        name="in_proj",
    )(x2, g, w, wt, qfeat, cos_t, sin_t, kfeat)


def _mlaup_kernel(cq_ref, ckv_ref, gq_ref, gkv_ref, wq_ref, wk_ref, wvt_ref, cos_ref, sin_ref, qn_ref, qr_ref,
                  kn_ref, vt_ref, *, scale):
    cqn = _rmsnorm(cq_ref[...], gq_ref[...]).astype(BF16)
    ckvn = _rmsnorm(ckv_ref[...], gkv_ref[...]).astype(BF16)
    hn = MLA_HEADS * MLA_NOPE
    hr = MLA_HEADS * MLA_ROPE
    qn_ref[...] = (_dot(cqn, wq_ref[:, :hn]) * scale).astype(BF16)
    a = _dot(cqn, wq_ref[:, hn:hn + hr])
    b = _dot(cqn, wq_ref[:, hn + hr:])
    cos = cos_ref[...]
    sin = sin_ref[...]
    for p in range(hr // LANES):
        sl = slice(p * LANES, (p + 1) * LANES)
        qr_ref[:, sl] = ((a[:, sl] * cos + b[:, sl] * sin) * scale).astype(BF16)
    kn_ref[...] = _dot(ckvn, wk_ref[...]).astype(BF16)
    v_t = _dot_nt(wvt_ref[...], ckvn).astype(BF16)
    ones = _ones_rows(MLA_TK)
    for c in range(vt_ref.shape[0]):
        pieces = []
        for h in range(MLA_HEADS):
            pieces += [v_t[h * MLA_V:(h + 1) * MLA_V, c * MLA_TK:(c + 1) * MLA_TK], ones]
        vt_ref[c] = jnp.concatenate(pieces, axis=0)


def _mla_up(cq, ckv, gq, gkv, wq, wk, wvt, cos_t, sin_t, seq, tm=512):
    n = cq.shape[0]
    nt = seq // tm
    hn = MLA_HEADS * MLA_NOPE
    hr = MLA_HEADS * MLA_ROPE
    hv = MLA_HEADS * (MLA_V + ONES_ROWS)
    row = lambda c: pl.BlockSpec((tm, c), lambda i: (i, 0))
    tab = pl.BlockSpec((tm, LANES), lambda i: (i % nt, 0))
    scale = (MLA_NOPE + MLA_ROPE) ** -0.5 * LOG2E
    return pl.pallas_call(
        functools.partial(_mlaup_kernel, scale=scale),
        grid=(n // tm,),
        in_specs=[row(MLA_Q_LORA), row(MLA_KV_LORA), _resident(gq.shape), _resident(gkv.shape),
                  _resident(wq.shape), _resident(wk.shape), _resident(wvt.shape), tab, tab],
        out_specs=[row(hn), row(hr), row(hn), pl.BlockSpec((tm // MLA_TK, hv, MLA_TK), lambda i: (i, 0, 0))],
        out_shape=[jax.ShapeDtypeStruct((n, hn), BF16), jax.ShapeDtypeStruct((n, hr), BF16),
                   jax.ShapeDtypeStruct((n, hn), BF16), jax.ShapeDtypeStruct((n // MLA_TK, hv, MLA_TK), BF16)],
        compiler_params=_params(1),
        name="mla_up",
    )(cq, ckv, gq, gkv, wq, wk, wvt, cos_t, sin_t)


def _mla_attn_kernel(qn_ref, qr_ref, kn_ref, kr_ref, vt_ref, o_ref, m_ref, acc_ref, s_scr, qt_scr, *, tq):
    i = pl.program_id(2)
    tk = MLA_TK
    lane = lax.broadcasted_iota(jnp.int32, (tq, LANES), 1)
    krow = lax.broadcasted_iota(jnp.int32, (tk, tq), 0)
    qcol = lax.broadcasted_iota(jnp.int32, (tk, tq), 1)
    qr = qr_ref[...]
    zero = jnp.zeros_like(qr)
    for hh in range(2):
        q_cat = jnp.concatenate([qn_ref[:, hh * LANES:(hh + 1) * LANES],
                                 jnp.where((lane < MLA_ROPE) == (hh == 0), qr, zero)], axis=1)
        qt_scr[hh] = q_cat.astype(F32).T.astype(BF16)
    m_ref[...] = jnp.full_like(m_ref, NEG)
    acc_ref[...] = jnp.zeros_like(acc_ref)
    vrows = MLA_V + ONES_ROWS

    def scores(j, slot):
        ks = pl.ds(pl.multiple_of(j * tk, tk), tk)
        k_rope = kr_ref[ks, :]
        for hh in range(2):
            hs = slice(hh * LANES, (hh + 1) * LANES)
            s_scr[slot, hh] = _dot(jnp.concatenate([kn_ref[ks, hs], k_rope], axis=1), qt_scr[hh])

    def consume(j, slot, diagonal):
        weights = []
        for hh in range(2):
            s = s_scr[slot, hh]
            if diagonal:
                s = jnp.where(j * tk + krow <= i * tq + qcol, s, NEG)
            weights.append(_flash_weights(s, m_ref, hh))
        for hh in range(2):
            alpha, p = weights[hh]
            acc_ref[hh] = alpha * acc_ref[hh] + _dot(vt_ref[0, j, hh * vrows:(hh + 1) * vrows, :], p)

    def body(u, carry):
        j = 2 * u
        scores(j + 1, 1)
        consume(j, 0, False)
        scores(j + 2, 0)
        consume(j + 1, 1, False)
        return carry

    assert tq == tk, "the last key tile of a query tile must be its only diagonal tile"
    scores(0, 0)
    lax.fori_loop(0, i // 2, body, 0)

    @pl.when(i % 2 == 1)
    def _():
        scores(i, 1)
        consume(i - 1, 0, False)
        consume(i, 1, True)

    @pl.when(i % 2 == 0)
    def _():
        consume(i, 0, True)

    for hh in range(2):
        o_t = acc_ref[hh, :MLA_V] / acc_ref[hh, MLA_V:MLA_V + 1]
        for c in range(tq // LANES):
            o_ref[c * LANES:(c + 1) * LANES, hh * LANES:(hh + 1) * LANES] = (
                o_t[:, c * LANES:(c + 1) * LANES].T.astype(BF16))


def _mla_attn(qn, qr, kn, kr, vt, batch, seq, tq=512):
    n = qn.shape[0]
    nq = seq // tq
    nk = seq // MLA_TK
    pairs = MLA_HEADS // 2
    pw = 2 * LANES
    qspec = lambda c: pl.BlockSpec((tq, c), lambda b, p, i: (b * nq + i, p))
    vrows = MLA_V + ONES_ROWS
    vt4 = vt.reshape(batch, nk, MLA_HEADS * vrows, MLA_TK)
    return pl.pallas_call(
        functools.partial(_mla_attn_kernel, tq=tq),
        grid=(batch, pairs, nq),
        in_specs=[qspec(pw), qspec(LANES), pl.BlockSpec((seq, pw), lambda b, p, i: (b, p)),
                  pl.BlockSpec((seq, LANES), lambda b, p, i: (b, 0)),
                  pl.BlockSpec((1, nk, 2 * vrows, MLA_TK), lambda b, p, i: (b, 0, p, 0))],
        out_specs=qspec(pw),
        out_shape=jax.ShapeDtypeStruct((n, MLA_HEADS * MLA_V), BF16),
        scratch_shapes=[pltpu.VMEM((2, 1, tq), F32), pltpu.VMEM((2, vrows, tq), F32),
                        pltpu.VMEM((2, 2, MLA_TK, tq), F32), pltpu.VMEM((2, 2 * LANES, tq), BF16)],
        compiler_params=_params(3),
        name="mla_attn",
    )(qn, qr, kn, kr, vt4)


def _compress_kernel(r_ref, pos_ref, w1_ref, w2_ref, w2t_ref, feat_ref, o_ref, ot_ref):
    nc = o_ref.shape[3]
    a = jnp.zeros((nc, CMP_HIDDEN), F32)
    b = jnp.zeros((nc, CMP_HIDDEN), F32)
    for l in range(CMP_STRIDE):
        x = r_ref[0, pl.ds(l, nc, stride=CMP_STRIDE), :]
        lo = slice(l * NSA_DK, (l + 1) * NSA_DK)
        hi = slice((CMP_STRIDE + l) * NSA_DK, (CMP_STRIDE + l + 1) * NSA_DK)
        a = a + _dot((x + pos_ref[0, l:l + 1, :]).astype(BF16), w1_ref[0, lo, :])
        b = b + _dot((x + pos_ref[0, CMP_STRIDE + l:CMP_STRIDE + l + 1, :]).astype(BF16), w1_ref[0, hi, :])
    hid = a + pltpu.roll(b, nc - 1, 0)
    act = (hid * jax.nn.sigmoid(hid)).astype(BF16)
    o_ref[0, 0, 0] = (_dot(act, w2_ref[0]) + feat_ref[...]).astype(BF16)
    ot_ref[0, 0, 0] = _dot_nt(w2t_ref[0], act).astype(BF16)


def _compress(kvc, pos, w1, w2, w2t, feat, batch, seq):
    groups = NSA_GROUPS
    nc = seq // CMP_STRIDE
    return pl.pallas_call(
        _compress_kernel,
        grid=(2, batch, groups),
        in_specs=[pl.BlockSpec((1, seq, NSA_DK), lambda t, b, g: (t * groups + g, b, 0)),
                  pl.BlockSpec((1, CMP_LEN, NSA_DK), lambda t, b, g: (t, 0, 0)),
                  pl.BlockSpec((1, CMP_LEN * NSA_DK, CMP_HIDDEN), lambda t, b, g: (t, 0, 0)),
                  pl.BlockSpec((1, CMP_HIDDEN, LANES), lambda t, b, g: (t, 0, 0)),
                  pl.BlockSpec((1, NSA_DK, CMP_HIDDEN), lambda t, b, g: (t, 0, 0)),
                  pl.BlockSpec((nc, LANES), lambda t, b, g: (0, 0))],
        out_specs=[pl.BlockSpec((1, 1, 1, nc, LANES), lambda t, b, g: (t, b, g, 0, 0)),
                   pl.BlockSpec((1, 1, 1, NSA_DK, nc), lambda t, b, g: (t, b, g, 0, 0))],
        out_shape=[jax.ShapeDtypeStruct((2, batch, groups, nc, LANES), BF16),
                   jax.ShapeDtypeStruct((2, batch, groups, NSA_DK, nc), BF16)],
        compiler_params=_params(3),
        name="compress",
    )(kvc, pos, w1, w2, w2t, feat)


def _nsa_cmp_kernel(q_ref, kc_ref, vaug_ref, gt_ref, oc_ref, selt_ref, flag_ref, q_scr, m_ref, acc_ref, s_scr,
                    *, n_cmp):
    i = pl.program_id(2)
    t = NSA_TILE
    tq = CMP_TQ
    hpg = NSA_HPG
    _stack_heads(q_ref, q_scr)
    m_ref[...] = jnp.full_like(m_ref, NEG)
    acc_ref[...] = jnp.zeros_like(acc_ref)

    crow = lax.broadcasted_iota(jnp.int32, (t, tq), 0)
    qpos = i * tq + lax.broadcasted_iota(jnp.int32, (t, tq), 1)

    def keepf(j):
        c = j * t + crow
        return jnp.where(CMP_STRIDE * c + (CMP_LEN - 1) <= qpos, jnp.where(c < n_cmp, 1.0, 0.0), 0.0)

    count = jnp.minimum((i * tq + tq - CMP_LEN) // (CMP_STRIDE * t) + 1, kc_ref.shape[3] // t)
    _skewed_flash(count, lambda n: n, keepf, lambda j: kc_ref[0, 0, 0, pl.ds(pl.multiple_of(j * t, t), t), :],
                  lambda j: vaug_ref[0, 0, j], q_scr, s_scr, m_ref, acc_ref, 0)

    some = jnp.where(qpos[:1] >= CMP_LEN - 1, 1.0, 0.0)
    inv = jnp.concatenate([some] * hpg, axis=1) / acc_ref[0, NSA_DK + NS_PAD:NSA_DK + NS_PAD + 1]
    gates = jax.nn.sigmoid(jnp.concatenate([gt_ref[c] for c in range(tq // t)], axis=1))
    imp = jnp.zeros((NS_PAD, tq), F32)
    for pair in range(hpg // 2):
        halves = []
        for h in (2 * pair, 2 * pair + 1):
            hs = slice(h * tq, (h + 1) * tq)
            halves.append(gates[3 * h:3 * h + 1, :] * (acc_ref[0, :NSA_DK, hs] * inv[:, hs]))
            imp = imp + acc_ref[0, NSA_DK:NSA_DK + NS_PAD, hs] * inv[:, hs]
        oc_ref[:, pair * LANES:(pair + 1) * LANES] = jnp.concatenate(halves, axis=0).T.astype(BF16)

    blk = lax.broadcasted_iota(jnp.int32, (NS_PAD, tq), 0)
    blk_t = (i * tq + lax.broadcasted_iota(jnp.int32, (NS_PAD, tq), 1)) // SLC_LEN
    forced = jnp.where(blk == 0, 1.0, 0.0) + jnp.where(blk == blk_t, 1.0, 0.0) + jnp.where(blk == blk_t - 1, 1.0, 0.0)
    imp = jnp.where(forced > 0.5, FORCE_SCORE, jnp.where(blk <= blk_t, imp, -1.0))
    blkf = blk.astype(F32)
    sel = jnp.zeros((NS_PAD, tq), F32)
    for _ in range(SLC_TOPK):
        best = jnp.max(imp, axis=0, keepdims=True)
        pick = jnp.min(jnp.where(imp == best, blkf, float(NS_PAD)), axis=0, keepdims=True)
        hit = blkf == pick
        sel = jnp.where(hit, 1.0, sel)
        imp = jnp.where(hit, TAKEN, imp)
    selt_ref[0, 0] = sel
    for c in range(tq // t):
        flag_ref[c] = jnp.max(sel[:, c * t:(c + 1) * t].T, axis=0, keepdims=True).astype(jnp.int32)


def _nsa_cmp(q, kc, vaug, gt, batch, seq, n_cmp):
    t = NSA_TILE
    tq = CMP_TQ
    nq = seq // tq
    n = batch * seq
    ncp = kc.shape[3]
    sub = tq // t
    rows = lambda c: pl.BlockSpec((tq, c), lambda b, g, i: (b * nq + i, g))
    return pl.pallas_call(
        functools.partial(_nsa_cmp_kernel, n_cmp=n_cmp),
        grid=(batch, NSA_GROUPS, nq),
        in_specs=[pl.BlockSpec((sub, NSA_HPG * LANES, t), lambda b, g, i: (b * nq + i, g, 0)),
                  pl.BlockSpec((1, 1, 1, ncp, LANES), lambda b, g, i: (0, b, g, 0, 0)),
                  pl.BlockSpec((1, 1) + vaug.shape[2:], lambda b, g, i: (b, g, 0, 0, 0)),
                  pl.BlockSpec((sub, N_GATES, t), lambda b, g, i: (b * nq + i, g, 0))],
        out_specs=[rows(NSA_HPG * NSA_DK),
                   pl.BlockSpec((1, 1, NS_PAD, tq), lambda b, g, i: (b, g, 0, i)),
                   pl.BlockSpec((sub, 1, NS_PAD), lambda b, g, i: ((b * NSA_GROUPS + g) * nq + i, 0, 0))],
        out_shape=[jax.ShapeDtypeStruct((n, NSA_HEADS * NSA_DK), BF16),
                   jax.ShapeDtypeStruct((batch, NSA_GROUPS, NS_PAD, seq), F32),
                   jax.ShapeDtypeStruct((batch * NSA_GROUPS * seq // t, 1, NS_PAD), jnp.int32)],
        scratch_shapes=[pltpu.VMEM((LANES, NSA_HPG * tq), BF16), pltpu.VMEM((1, 1, NSA_HPG * tq), F32),
                        pltpu.VMEM((1, NSA_DK + NS_PAD + ONES_ROWS, NSA_HPG * tq), F32),
                        pltpu.VMEM((2, 2 * t, NSA_HPG * tq), F32)],
        compiler_params=_params(3),
        name="nsa_cmp",
    )(q, kc, vaug, gt)


ID_BITS = 8
IDS_PER_WORD = 32 // ID_BITS


def _nsa_attn_kernel(counts_ref, words_ref, q_ref, selt_ref, oc_ref, gt_ref, ks_ref, kw_ref, vst_ref, vwt_ref, o_ref,
                     q_scr, m_ref, acc_ref, s_scr, *, nq, words_per_step):
    b = pl.program_id(0)
    g = pl.program_id(1)
    i = pl.program_id(2)
    t = NSA_TILE
    tq = ATT_TQ
    sub = tq // t
    hpg = NSA_HPG
    _stack_heads(q_ref, q_scr)
    m_ref[...] = jnp.full_like(m_ref, NEG)
    acc_ref[...] = jnp.zeros_like(acc_ref)

    krow = lax.broadcasted_iota(jnp.int32, (t, tq), 0)
    qpos = i * tq + lax.broadcasted_iota(jnp.int32, (t, tq), 1)
    blocks_per_tile = t // SLC_LEN
    slc, win = 0, 1
    step = (b * NSA_GROUPS + g) * nq + i

    def run_branch(branch, count, tile_at, keepf_fn, k_ref, vt_ref):
        _skewed_flash(count, tile_at, keepf_fn, lambda j: k_ref[pl.ds(pl.multiple_of(j * t, t), t), :],
                      lambda j: vt_ref[0, j], q_scr, s_scr, m_ref, acc_ref, branch)

    def slc_tile(n):
        word = words_ref[step * words_per_step + n // IDS_PER_WORD]
        return lax.shift_right_logical(word, ID_BITS * (n % IDS_PER_WORD)) & (2 ** ID_BITS - 1)

    def slc_keepf(j):
        dist = qpos - (j * t + krow)
        sel_rows = [selt_ref[0, 0, pl.ds(blocks_per_tile * j + r, 1), :] for r in range(blocks_per_tile)]
        picked = sel_rows[-1]
        for r in range(blocks_per_tile - 2, -1, -1):
            picked = jnp.where(krow < (r + 1) * SLC_LEN, sel_rows[r], picked)
        return jnp.where(dist >= 0, picked, 0.0)

    def win_keepf(j):
        dist = qpos - (j * t + krow)
        return jnp.where(dist >= 0, jnp.where(dist < WINDOW, 1.0, 0.0), 0.0)

    run_branch(slc, counts_ref[step], slc_tile, slc_keepf, ks_ref, vst_ref)
    first = jnp.maximum(i * sub - WINDOW // t, 0)
    run_branch(win, (i + 1) * sub - first, lambda n: first + n, win_keepf, kw_ref, vwt_ref)

    gates = jax.nn.sigmoid(jnp.concatenate([gt_ref[c] for c in range(sub)], axis=1))
    inv_s = 1.0 / acc_ref[slc, NSA_DK:NSA_DK + 1]
    inv_w = 1.0 / acc_ref[win, NSA_DK:NSA_DK + 1]
    for pair in range(hpg // 2):
        halves = []
        for h in (2 * pair, 2 * pair + 1):
            hs = slice(h * tq, (h + 1) * tq)
            halves.append(gates[3 * h + 1:3 * h + 2, :] * (acc_ref[slc, :NSA_DK, hs] * inv_s[:, hs])
                          + gates[3 * h + 2:3 * h + 3, :] * (acc_ref[win, :NSA_DK, hs] * inv_w[:, hs]))
        ps = slice(pair * LANES, (pair + 1) * LANES)
        o_ref[:, ps] = (oc_ref[:, ps].astype(F32) + jnp.concatenate(halves, axis=0).T).astype(BF16)


def _nsa_attn(counts, words, q, selt, oc, gt, ksw, vt, batch, seq):
    t = NSA_TILE
    tq = ATT_TQ
    sub = tq // t
    nq = seq // tq
    nk = seq // t
    n = batch * seq
    words_per_step = words.shape[0] // (batch * NSA_GROUPS * nq)
    rows = lambda c: pl.BlockSpec((tq, c), lambda b, g, i, *_: (b * nq + i, g))
    tiles = lambda r: pl.BlockSpec((sub, r, t), lambda b, g, i, *_: (b * nq + i, g, 0))
    key = lambda which: pl.BlockSpec((seq, LANES), lambda b, g, i, *_: (b, which * NSA_GROUPS + g))
    vt4 = vt.reshape(batch, nk, _VT_OUT, t)
    vrows = NSA_DK + ONES_ROWS
    val = lambda which: pl.BlockSpec((1, nk, vrows, t), lambda b, g, i, *_: (b, 0, which * NSA_GROUPS + g, 0))
    cols = NSA_HPG * tq
    grid_spec = pltpu.PrefetchScalarGridSpec(
        num_scalar_prefetch=2,
        grid=(batch, NSA_GROUPS, nq),
        in_specs=[tiles(NSA_HPG * LANES),
                  pl.BlockSpec((1, 1, NS_PAD, tq), lambda b, g, i, *_: (b, g, 0, i)),
                  rows(NSA_HPG * NSA_DK), tiles(N_GATES), key(0), key(1), val(0), val(1)],
        out_specs=rows(NSA_HPG * NSA_DK),
        scratch_shapes=[pltpu.VMEM((LANES, cols), BF16), pltpu.VMEM((2, 1, cols), F32),
                        pltpu.VMEM((2, vrows, cols), F32), pltpu.VMEM((2, 2 * t, cols), F32)],
    )
    return pl.pallas_call(
        functools.partial(_nsa_attn_kernel, nq=nq, words_per_step=words_per_step),
        grid_spec=grid_spec,
        out_shape=jax.ShapeDtypeStruct((n, NSA_HEADS * NSA_DK), BF16),
        compiler_params=_params(3),
        name="nsa_attn",
    )(counts, words, q, selt, oc, gt, ksw, ksw, vt4, vt4)


def _outproj_kernel(x_ref, om_ref, on_ref, wm_ref, wn_ref, o_ref):
    o_ref[...] = x_ref[...] + _dot(om_ref[...], wm_ref[...]) + _dot(on_ref[...], wn_ref[...])


def _out_proj(x2, o_mla, o_nsa, wm, wn, tm=512):
    n, d = x2.shape
    row = lambda c: pl.BlockSpec((tm, c), lambda i: (i, 0))
    return pl.pallas_call(
        _outproj_kernel,
        grid=(n // tm,),
        in_specs=[row(d), row(o_mla.shape[1]), row(o_nsa.shape[1]), _resident(wm.shape), _resident(wn.shape)],
        out_specs=row(d),
        out_shape=jax.ShapeDtypeStruct((n, d), F32),
        compiler_params=_params(1),
        name="out_proj",
    )(x2, o_mla, o_nsa, wm, wn)


def _ffn_kernel(x_ref, g_ref, wg_ref, wu_ref, wd_ref, gf_ref, o_ref, h_scr, acc_scr):
    f = pl.program_id(1)

    @pl.when(f == 0)
    def _():
        h_scr[...] = _rmsnorm(x_ref[...], g_ref[...]).astype(BF16)
        acc_scr[...] = jnp.zeros_like(acc_scr)

    h = h_scr[...]
    gate = _dot(h, wg_ref[...])
    act = (gate * jax.nn.sigmoid(gate)) * _dot(h, wu_ref[...])
    acc_scr[...] += _dot(act.astype(BF16), wd_ref[...])

    @pl.when(f == pl.num_programs(1) - 1)
    def _():
        o_ref[...] = _rmsnorm(x_ref[...] + acc_scr[...], gf_ref[...])


def _ffn(x1, g, wg, wu, wd, gf, tm=512, tf=512):
    n, d = x1.shape
    dff = wg.shape[1]
    return pl.pallas_call(
        _ffn_kernel,
        grid=(n // tm, dff // tf),
        in_specs=[pl.BlockSpec((tm, d), lambda i, f: (i, 0)), _resident((1, d)),
                  pl.BlockSpec((d, tf), lambda i, f: (0, f)), pl.BlockSpec((d, tf), lambda i, f: (0, f)),
                  pl.BlockSpec((tf, d), lambda i, f: (f, 0)), _resident((1, d))],
        out_specs=pl.BlockSpec((tm, d), lambda i, f: (i, 0)),
        out_shape=jax.ShapeDtypeStruct((n, d), F32),
        scratch_shapes=[pltpu.VMEM((tm, d), BF16), pltpu.VMEM((tm, d), F32)],
        compiler_params=_params(2),
        name="ffn",
    )(x1, g, wg, wu, wd, gf)


def _pad_cols(w, width):
    return jnp.pad(w, ((0, 0), (0, width - w.shape[1])))


def _rot_cols(w):
    half = w.shape[1] // 2
    return jnp.concatenate([-w[:, half:], w[:, :half]], axis=1)


def _fused_in_weights(w_in):
    sizes = (MLA_Q_LORA, MLA_KV_LORA, MLA_ROPE, NSA_HEADS * NSA_DK) + (NSA_GROUPS * NSA_DK,) * 6 + (3 * NSA_HEADS,)
    offs = np.cumsum(sizes)[:-1].tolist()
    cq, ckv, kr, q, kc, vc, ks, vs, kw, vw, gate = jnp.split(w_in, offs, axis=1)
    d = w_in.shape[0]
    q_pad = _pad_cols((q * (NSA_DK ** -0.5 * LOG2E)).reshape(d * NSA_HEADS, NSA_DK), LANES)
    q_pad = q_pad.reshape(d, NSA_HEADS * LANES)
    per_group = lambda w, c: _pad_cols(w.reshape(d * NSA_GROUPS, c), LANES).reshape(d, NSA_GROUPS * LANES)
    kr_rot = _rot_cols(kr)
    cols = [cq, ckv, kr, kr, kr_rot, kr_rot, kc, vc, per_group(ks, NSA_DK), per_group(kw, NSA_DK)]
    w = jnp.concatenate(cols, axis=1).astype(BF16)
    assert w.shape[1] == _IN_COLS["ksw"][1]
    rows_t = jnp.concatenate([q_pad, vs, vw, gate], axis=1).T
    pad = -rows_t.shape[0] % 16
    return w, jnp.pad(rows_t, ((0, pad), (0, 0))).astype(BF16)


def _slope_features(slopes2):
    s1 = slopes2.astype(BF16).astype(F32)
    s2 = (slopes2 - s1).astype(BF16).astype(F32)
    s3 = (slopes2 - s1 - s2).astype(BF16).astype(F32)
    pieces = jnp.stack([s1, s2, s3, s1, s2, s3], axis=1)
    column = jnp.pad(pieces, ((0, 0), (NSA_DK, LANES - NSA_DK - 6))).reshape(-1, 1)
    return jnp.broadcast_to(column, (column.shape[0], NSA_TILE))


def _position_features(pos):
    hi = (POS_SPLIT * (pos // POS_SPLIT)).astype(F32)
    lo = (pos % POS_SPLIT).astype(F32)
    return jnp.pad(jnp.stack([hi, hi, hi, lo, lo, lo], axis=1), ((0, 0), (NSA_DK, LANES - NSA_DK - 6)))


def _fused_uq_weight(w_uq):
    d = w_uq.shape[0]
    w = w_uq.reshape(d, MLA_HEADS, MLA_NOPE + MLA_ROPE)
    nope = w[:, :, :MLA_NOPE].reshape(d, MLA_HEADS * MLA_NOPE)
    rope = w[:, :, MLA_NOPE:]
    rope_rot = jnp.concatenate([-rope[:, :, MLA_ROPE // 2:], rope[:, :, :MLA_ROPE // 2]], axis=2)
    flat = lambda r: r.reshape(d, MLA_HEADS * MLA_ROPE)
    return jnp.concatenate([nope, flat(rope), flat(rope_rot)], axis=1).astype(BF16)


def _rope_tables(seq):
    inv = ROPE_THETA ** (-jnp.arange(0, MLA_ROPE, 2, dtype=F32) / MLA_ROPE)
    ang = jnp.arange(seq, dtype=F32)[:, None] * inv[None, :]
    reps = 2 * LANES // MLA_ROPE
    return jnp.tile(jnp.cos(ang), (1, reps)), jnp.tile(jnp.sin(ang), (1, reps))


def _overlap_matrix(seq, n_cmp, ncp):
    cmp_start = CMP_STRIDE * np.arange(n_cmp)
    slc_start = SLC_LEN * np.arange(seq // SLC_LEN)
    ov = np.clip(np.minimum(cmp_start[:, None] + CMP_LEN, slc_start[None, :] + SLC_LEN)
                 - np.maximum(cmp_start[:, None], slc_start[None, :]), 0, None).astype(np.float32) / CMP_STRIDE
    out = np.zeros((NS_PAD, ncp), np.float32)
    out[:ov.shape[1], :n_cmp] = ov.T
    return jnp.asarray(out, BF16)


def _active_tiles(flags, seq):
    t_blocks = NSA_TILE // SLC_LEN
    sub = ATT_TQ // NSA_TILE
    nq = seq // ATT_TQ
    steps = flags.shape[0] // sub
    tiles = flags.reshape(steps, sub, NS_PAD // t_blocks, t_blocks).max(axis=(1, 3))
    tile_id = jnp.arange(tiles.shape[1], dtype=jnp.int32)[None, :]
    last_tile = ((jnp.arange(steps, dtype=jnp.int32) % nq) * sub + (sub - 1))[:, None]
    active = (tiles > 0) & (tile_id <= last_tile)
    order = jnp.argsort(jnp.where(active, tile_id, tile_id + tiles.shape[1]), axis=-1).astype(jnp.int32)
    shifts = ID_BITS * jnp.arange(IDS_PER_WORD, dtype=jnp.int32)
    words = (order.reshape(steps, -1, IDS_PER_WORD) << shifts).sum(axis=-1).astype(jnp.int32)
    return active.sum(axis=-1).astype(jnp.int32), words.reshape(-1)


def kernel(x, attn_norm_g, w_in, mla_q_norm_g, mla_kv_norm_g, w_uq, w_uk, w_uv, cmp_pos_k, cmp_pos_v, w_cmp_k1,
           w_cmp_k2, w_cmp_v1, w_cmp_v2, w_o, ffn_norm_g, w_gate, w_up, w_down, final_norm_g):
    batch, seq, d = x.shape
    n = batch * seq
    assert w_in.shape[0] == 1, "the final RMSNorm is fused into the FFN kernel of a single layer"
    assert seq % (CMP_STRIDE * NSA_TILE) == 0 and seq % CMP_TQ == 0 and seq // SLC_LEN <= NS_PAD
    assert seq // POS_SPLIT <= BF16_EXACT_INT and POS_SPLIT <= BF16_EXACT_INT and seq // NSA_TILE <= 2 ** ID_BITS
    n_cmp = (seq - CMP_LEN) // CMP_STRIDE + 1
    ncp = seq // CMP_STRIDE
    cos_t, sin_t = _rope_tables(seq)
    ov_t = _overlap_matrix(seq, n_cmp, ncp)
    slopes2 = 2.0 ** (-8.0 * jnp.arange(1, NSA_HEADS + 1, dtype=F32) / NSA_HEADS) * LOG2E
    qfeat = _slope_features(slopes2)
    kfeat = _position_features(jnp.arange(seq, dtype=jnp.int32))
    x2 = x.reshape(n, d)

    w_fused, w_fused_t = _fused_in_weights(w_in[0])
    cq, ckv, kr, ksw, kvc, q, vt, gt = _in_proj(
        x2, attn_norm_g[0][None], w_fused, w_fused_t, qfeat, cos_t, sin_t, kfeat, seq)

    qn, qr, kn, vt_mla = _mla_up(cq, ckv, mla_q_norm_g[0][None], mla_kv_norm_g[0][None], _fused_uq_weight(w_uq[0]),
                                 w_uk[0].astype(BF16), w_uv[0].T.astype(BF16), cos_t, sin_t, seq)
    o_mla = _mla_attn(qn, qr, kn, kr, vt_mla, batch, seq)

    pos = jnp.stack([cmp_pos_k[0], cmp_pos_v[0]])
    w1 = jnp.stack([w_cmp_k1[0], w_cmp_v1[0]]).astype(BF16)
    w2 = jnp.stack([_pad_cols(w_cmp_k2[0], LANES), _pad_cols(w_cmp_v2[0], LANES)]).astype(BF16)
    w2t = jnp.stack([w_cmp_k2[0].T, w_cmp_v2[0].T]).astype(BF16)
    cmp_end = CMP_STRIDE * jnp.arange(ncp, dtype=jnp.int32) + (CMP_LEN - 1)
    cmp_rows, cmp_t = _compress(kvc, pos, w1, w2, w2t, _position_features(cmp_end), batch, seq)
    ones = jnp.zeros((ONES_ROWS, ncp), BF16).at[0].set(1.0)
    const_rows = jnp.broadcast_to(jnp.concatenate([ov_t, ones]), (batch, NSA_GROUPS, NS_PAD + ONES_ROWS, ncp))
    vaug = jnp.concatenate([cmp_t[1], const_rows], axis=2)
    vaug = vaug.reshape(batch, NSA_GROUPS, vaug.shape[2], ncp // NSA_TILE, NSA_TILE).transpose(0, 1, 3, 2, 4)

    oc, selt, flags = _nsa_cmp(q, cmp_rows, vaug, gt, batch, seq, n_cmp)
    counts, words = _active_tiles(flags.reshape(flags.shape[0], NS_PAD), seq)
    o_nsa = _nsa_attn(counts, words, q, selt, oc, gt, ksw, vt, batch, seq)

    split = MLA_HEADS * MLA_V
    x1 = _out_proj(x2, o_mla, o_nsa, w_o[0][:split].astype(BF16), w_o[0][split:].astype(BF16))
    out = _ffn(x1, ffn_norm_g[0][None], w_gate[0].astype(BF16), w_up[0].astype(BF16), w_down[0].astype(BF16),
               final_norm_g[None])
    return out.reshape(batch, seq, d)
```

```python
import functools
import math

import numpy as np
import jax
import jax.numpy as jnp
from jax import lax
from jax.experimental import pallas as pl
from jax.experimental.pallas import tpu as pltpu

F32 = jnp.float32
BF16 = jnp.bfloat16

EPS = 1e-6
NEG = -1e30
LOG2E = math.log2(math.e)
LANES = 128

MLA_HEADS = 8
MLA_Q_LORA = 512
MLA_KV_LORA = 256
MLA_NOPE = 128
MLA_ROPE = 64
MLA_V = 128
ROPE_THETA = 10000.0
MLA_TK = 512

NSA_HEADS = 16
NSA_GROUPS = 2
NSA_HPG = NSA_HEADS // NSA_GROUPS
NSA_DK = 64
CMP_LEN = 32
CMP_STRIDE = 16
CMP_HIDDEN = 128
SLC_LEN = 64
SLC_TOPK = 16
WINDOW = 512
FORCE_SCORE = 1e4
TAKEN = -3e38
NSA_TILE = 128
NS_PAD = 128
CMP_TQ = 512
ATT_TQ = 256
N_GATES = 3 * NSA_HPG
POS_SPLIT = 64
BF16_EXACT_INT = 256
ONES_ROWS = 16

VMEM_LIMIT = 56 * 1024 * 1024


def _params(n_axes):
    return pltpu.CompilerParams(dimension_semantics=("arbitrary",) * n_axes, vmem_limit_bytes=VMEM_LIMIT)


def _resident(shape):
    zeros = (0,) * len(shape)
    return pl.BlockSpec(shape, lambda *_: zeros, pipeline_mode=pl.Buffered(1))


def _rmsnorm(x, g):
    return x * lax.rsqrt(jnp.mean(x * x, axis=-1, keepdims=True) + EPS) * g


def _dot(a, b):
    return jnp.dot(a, b, preferred_element_type=F32)


def _dot_nt(a, b):
    return lax.dot_general(a, b, (((1,), (1,)), ((), ())), preferred_element_type=F32)


def _ones_rows(width):
    return jnp.where(lax.broadcasted_iota(jnp.int32, (ONES_ROWS, width), 0) == 0, 1.0, 0.0).astype(BF16)


def _flash_weights(s, m_ref, idx):
    m_prev = m_ref[idx]
    m_new = jnp.maximum(m_prev, jnp.max(s, axis=0, keepdims=True))
    m_ref[idx] = m_new
    return jnp.exp2(m_prev - m_new), jnp.exp2(s - m_new).astype(BF16)


def _stack_heads(qt_ref, qt_scr):
    tiles, rows, t = qt_ref.shape
    for h in range(rows // LANES):
        for c in range(tiles):
            qt_scr[:, (h * tiles + c) * t:(h * tiles + c + 1) * t] = qt_ref[c, h * LANES:(h + 1) * LANES, :]


def _flash_update(s, v_t, m_ref, acc_ref, idx):
    alpha, p = _flash_weights(s, m_ref, idx)
    acc_ref[idx] = alpha * acc_ref[idx] + _dot(v_t, p)


def _skewed_flash(count, tile_at, keepf_fn, k_tile, vt_tile, qt_scr, s_scr, m_ref, acc_ref, idx):
    pairs = (count + 1) // 2

    def pair(u):
        return [tile_at(jnp.minimum(2 * u + r, count - 1)) for r in range(2)]

    def scores(u, slot):
        j0, j1 = pair(jnp.minimum(u, pairs - 1))
        s_scr[slot] = _dot(jnp.concatenate([k_tile(j0), k_tile(j1)], axis=0), qt_scr[...])

    def consume(u, slot):
        j0, j1 = pair(u)
        second = jnp.where(2 * u + 1 < count, 1.0, 0.0)
        bias = jnp.concatenate([jnp.where(keepf_fn(j0) > 0.5, 0.0, NEG),
                                jnp.where(keepf_fn(j1) * second > 0.5, 0.0, NEG)], axis=0)
        s = s_scr[slot]
        s = s + jnp.concatenate([bias] * (s.shape[1] // bias.shape[1]), axis=1)
        _flash_update(s, jnp.concatenate([vt_tile(j0), vt_tile(j1)], axis=1), m_ref, acc_ref, idx)

    def body(w, carry):
        u = 2 * w
        scores(u + 1, 1)
        consume(u, 0)
        scores(u + 2, 0)
        consume(u + 1, 1)
        return carry

    scores(0, 0)
    lax.fori_loop(0, pairs // 2, body, 0)

    @pl.when(pairs % 2 == 1)
    def _():
        consume(pairs - 1, 0)


_IN_COLS = dict(cq=(0, 512), ckv=(512, 768), kra=(768, 896), krb=(896, 1024), kvc=(1024, 1280), ksw=(1280, 1792))
_QT_ROWS = NSA_HEADS * LANES
_Q_ROWS = NSA_HEADS * NSA_DK
_VT_ROWS = 2 * NSA_GROUPS * NSA_DK
_VT_OUT = _VT_ROWS // NSA_DK * (NSA_DK + ONES_ROWS)


def _inproj_kernel(x_ref, g_ref, w_ref, wt_ref, qfeat_ref, cos_ref, sin_ref, kfeat_ref, cq_ref, ckv_ref, kr_ref,
                   ksw_ref, kvc_ref, qt_ref, vt_ref, gt_ref):
    h = _rmsnorm(x_ref[...], g_ref[...]).astype(BF16)

    def mm(name):
        lo, hi = _IN_COLS[name]
        return _dot(h, w_ref[:, lo:hi])

    cq_ref[...] = mm("cq")
    ckv_ref[...] = mm("ckv")
    kr_ref[...] = (mm("kra") * cos_ref[...] + mm("krb") * sin_ref[...]).astype(BF16)
    kvc = mm("kvc")
    for k in range(kvc_ref.shape[0]):
        kvc_ref[k] = kvc[:, k * NSA_DK:(k + 1) * NSA_DK]
    kfeat = kfeat_ref[...]
    ksw_ref[...] = (mm("ksw") + jnp.concatenate([kfeat] * (2 * NSA_GROUPS), axis=1)).astype(BF16)
    t = _dot_nt(wt_ref[...], h)
    ones = _ones_rows(NSA_TILE)
    qfeat = qfeat_ref[...]
    for c in range(vt_ref.shape[0]):
        cs = slice(c * NSA_TILE, (c + 1) * NSA_TILE)
        pieces = []
        for h in range(NSA_HEADS):
            pieces += [t[h * NSA_DK:(h + 1) * NSA_DK, cs].astype(BF16), qfeat[h * NSA_DK:(h + 1) * NSA_DK]]
        qt_ref[c] = jnp.concatenate(pieces, axis=0)
        pieces = []
        for k in range(_VT_ROWS // NSA_DK):
            rows = slice(_Q_ROWS + k * NSA_DK, _Q_ROWS + (k + 1) * NSA_DK)
            pieces += [t[rows, cs].astype(BF16), ones]
        vt_ref[c] = jnp.concatenate(pieces, axis=0)
        gt_ref[c] = t[_Q_ROWS + _VT_ROWS:_Q_ROWS + _VT_ROWS + NSA_GROUPS * N_GATES, cs]


def _in_proj(x2, g, w, wt, qfeat, cos_t, sin_t, kfeat, seq, tm=256):
    n, d = x2.shape
    nt = seq // tm
    widths = {k: hi - lo for k, (lo, hi) in _IN_COLS.items()}
    row = lambda c: pl.BlockSpec((tm, c), lambda i: (i, 0))
    tab = pl.BlockSpec((tm, LANES), lambda i: (i % nt, 0))
    outs = [("cq", F32), ("ckv", F32), ("kra", BF16), ("ksw", BF16)]
    n_kvc = widths["kvc"] // NSA_DK
    tiles = tm // NSA_TILE
    tspec = lambda rows: pl.BlockSpec((tiles, rows, NSA_TILE), lambda i: (i, 0, 0))
    return pl.pallas_call(
        _inproj_kernel,
        grid=(n // tm,),
        in_specs=[row(d), _resident((1, d)), _resident(w.shape), _resident(wt.shape), _resident(qfeat.shape),
                  tab, tab, tab],
        out_specs=[row(widths[k]) for k, _ in outs] + [
            pl.BlockSpec((n_kvc, tm, NSA_DK), lambda i: (0, i, 0)),
            tspec(_QT_ROWS), tspec(_VT_OUT), tspec(NSA_GROUPS * N_GATES)],
        out_shape=[jax.ShapeDtypeStruct((n, widths[k]), dt) for k, dt in outs] + [
            jax.ShapeDtypeStruct((n_kvc, n, NSA_DK), F32),
            jax.ShapeDtypeStruct((n // NSA_TILE, _QT_ROWS, NSA_TILE), BF16),
            jax.ShapeDtypeStruct((n // NSA_TILE, _VT_OUT, NSA_TILE), BF16),
            jax.ShapeDtypeStruct((n // NSA_TILE, NSA_GROUPS * N_GATES, NSA_TILE), F32)],
        compiler_params=_params(1),
        name="in_proj",
    )(x2, g, w, wt, qfeat, cos_t, sin_t, kfeat)


def _mlaup_kernel(cq_ref, ckv_ref, gq_ref, gkv_ref, wq_ref, wk_ref, wvt_ref, cos_ref, sin_ref, qn_ref, qr_ref,
                  kn_ref, vt_ref, *, scale):
    cqn = _rmsnorm(cq_ref[...], gq_ref[...]).astype(BF16)
    ckvn = _rmsnorm(ckv_ref[...], gkv_ref[...]).astype(BF16)
    hn = MLA_HEADS * MLA_NOPE
    hr = MLA_HEADS * MLA_ROPE
    qn_ref[...] = (_dot(cqn, wq_ref[:, :hn]) * scale).astype(BF16)
    a = _dot(cqn, wq_ref[:, hn:hn + hr])
    b = _dot(cqn, wq_ref[:, hn + hr:])
    cos = cos_ref[...]
    sin = sin_ref[...]
    for p in range(hr // LANES):
        sl = slice(p * LANES, (p + 1) * LANES)
        qr_ref[:, sl] = ((a[:, sl] * cos + b[:, sl] * sin) * scale).astype(BF16)
    kn_ref[...] = _dot(ckvn, wk_ref[...]).astype(BF16)
    v_t = _dot_nt(wvt_ref[...], ckvn).astype(BF16)
    ones = _ones_rows(MLA_TK)
    for c in range(vt_ref.shape[0]):
        pieces = []
        for h in range(MLA_HEADS):
            pieces += [v_t[h * MLA_V:(h + 1) * MLA_V, c * MLA_TK:(c + 1) * MLA_TK], ones]
        vt_ref[c] = jnp.concatenate(pieces, axis=0)


def _mla_up(cq, ckv, gq, gkv, wq, wk, wvt, cos_t, sin_t, seq, tm=512):
    n = cq.shape[0]
    nt = seq // tm
    hn = MLA_HEADS * MLA_NOPE
    hr = MLA_HEADS * MLA_ROPE
    hv = MLA_HEADS * (MLA_V + ONES_ROWS)
    row = lambda c: pl.BlockSpec((tm, c), lambda i: (i, 0))
    tab = pl.BlockSpec((tm, LANES), lambda i: (i % nt, 0))
    scale = (MLA_NOPE + MLA_ROPE) ** -0.5 * LOG2E
    return pl.pallas_call(
        functools.partial(_mlaup_kernel, scale=scale),
        grid=(n // tm,),
        in_specs=[row(MLA_Q_LORA), row(MLA_KV_LORA), _resident(gq.shape), _resident(gkv.shape),
                  _resident(wq.shape), _resident(wk.shape), _resident(wvt.shape), tab, tab],
        out_specs=[row(hn), row(hr), row(hn), pl.BlockSpec((tm // MLA_TK, hv, MLA_TK), lambda i: (i, 0, 0))],
        out_shape=[jax.ShapeDtypeStruct((n, hn), BF16), jax.ShapeDtypeStruct((n, hr), BF16),
                   jax.ShapeDtypeStruct((n, hn), BF16), jax.ShapeDtypeStruct((n // MLA_TK, hv, MLA_TK), BF16)],
        compiler_params=_params(1),
        name="mla_up",
    )(cq, ckv, gq, gkv, wq, wk, wvt, cos_t, sin_t)


def _mla_attn_kernel(qn_ref, qr_ref, kn_ref, kr_ref, vt_ref, o_ref, m_ref, acc_ref, s_scr, qt_scr, *, tq):
    i = pl.program_id(2)
    tk = MLA_TK
    lane = lax.broadcasted_iota(jnp.int32, (tq, LANES), 1)
    krow = lax.broadcasted_iota(jnp.int32, (tk, tq), 0)
    qcol = lax.broadcasted_iota(jnp.int32, (tk, tq), 1)
    qr = qr_ref[...]
    zero = jnp.zeros_like(qr)
    for hh in range(2):
        q_cat = jnp.concatenate([qn_ref[:, hh * LANES:(hh + 1) * LANES],
                                 jnp.where((lane < MLA_ROPE) == (hh == 0), qr, zero)], axis=1)
        qt_scr[hh] = q_cat.astype(F32).T.astype(BF16)
    m_ref[...] = jnp.full_like(m_ref, NEG)
    acc_ref[...] = jnp.zeros_like(acc_ref)
    vrows = MLA_V + ONES_ROWS

    def scores(j, slot):
        ks = pl.ds(pl.multiple_of(j * tk, tk), tk)
        k_rope = kr_ref[ks, :]
        for hh in range(2):
            hs = slice(hh * LANES, (hh + 1) * LANES)
            s_scr[slot, hh] = _dot(jnp.concatenate([kn_ref[ks, hs], k_rope], axis=1), qt_scr[hh])

    def consume(j, slot, diagonal):
        weights = []
        for hh in range(2):
            s = s_scr[slot, hh]
            if diagonal:
                s = jnp.where(j * tk + krow <= i * tq + qcol, s, NEG)
            weights.append(_flash_weights(s, m_ref, hh))
        for hh in range(2):
            alpha, p = weights[hh]
            acc_ref[hh] = alpha * acc_ref[hh] + _dot(vt_ref[0, j, hh * vrows:(hh + 1) * vrows, :], p)

    def body(u, carry):
        j = 2 * u
        scores(j + 1, 1)
        consume(j, 0, False)
        scores(j + 2, 0)
        consume(j + 1, 1, False)
        return carry

    assert tq == tk, "the last key tile of a query tile must be its only diagonal tile"
    scores(0, 0)
    lax.fori_loop(0, i // 2, body, 0)

    @pl.when(i % 2 == 1)
    def _():
        scores(i, 1)
        consume(i - 1, 0, False)
        consume(i, 1, True)

    @pl.when(i % 2 == 0)
    def _():
        consume(i, 0, True)

    for hh in range(2):
        o_t = acc_ref[hh, :MLA_V] / acc_ref[hh, MLA_V:MLA_V + 1]
        for c in range(tq // LANES):
            o_ref[c * LANES:(c + 1) * LANES, hh * LANES:(hh + 1) * LANES] = (
                o_t[:, c * LANES:(c + 1) * LANES].T.astype(BF16))


def _mla_attn(qn, qr, kn, kr, vt, batch, seq, tq=512):
    n = qn.shape[0]
    nq = seq // tq
    nk = seq // MLA_TK
    pairs = MLA_HEADS // 2
    pw = 2 * LANES
    qspec = lambda c: pl.BlockSpec((tq, c), lambda b, p, i: (b * nq + i, p))
    vrows = MLA_V + ONES_ROWS
    vt4 = vt.reshape(batch, nk, MLA_HEADS * vrows, MLA_TK)
    return pl.pallas_call(
        functools.partial(_mla_attn_kernel, tq=tq),
        grid=(batch, pairs, nq),
        in_specs=[qspec(pw), qspec(LANES), pl.BlockSpec((seq, pw), lambda b, p, i: (b, p)),
                  pl.BlockSpec((seq, LANES), lambda b, p, i: (b, 0)),
                  pl.BlockSpec((1, nk, 2 * vrows, MLA_TK), lambda b, p, i: (b, 0, p, 0))],
        out_specs=qspec(pw),
        out_shape=jax.ShapeDtypeStruct((n, MLA_HEADS * MLA_V), BF16),
        scratch_shapes=[pltpu.VMEM((2, 1, tq), F32), pltpu.VMEM((2, vrows, tq), F32),
                        pltpu.VMEM((2, 2, MLA_TK, tq), F32), pltpu.VMEM((2, 2 * LANES, tq), BF16)],
        compiler_params=_params(3),
        name="mla_attn",
    )(qn, qr, kn, kr, vt4)


def _compress_kernel(r_ref, pos_ref, w1_ref, w2_ref, w2t_ref, feat_ref, o_ref, ot_ref):
    nc = o_ref.shape[3]
    a = jnp.zeros((nc, CMP_HIDDEN), F32)
    b = jnp.zeros((nc, CMP_HIDDEN), F32)
    for l in range(CMP_STRIDE):
        x = r_ref[0, pl.ds(l, nc, stride=CMP_STRIDE), :]
        lo = slice(l * NSA_DK, (l + 1) * NSA_DK)
        hi = slice((CMP_STRIDE + l) * NSA_DK, (CMP_STRIDE + l + 1) * NSA_DK)
        a = a + _dot((x + pos_ref[0, l:l + 1, :]).astype(BF16), w1_ref[0, lo, :])
        b = b + _dot((x + pos_ref[0, CMP_STRIDE + l:CMP_STRIDE + l + 1, :]).astype(BF16), w1_ref[0, hi, :])
    hid = a + pltpu.roll(b, nc - 1, 0)
    act = (hid * jax.nn.sigmoid(hid)).astype(BF16)
    o_ref[0, 0, 0] = (_dot(act, w2_ref[0]) + feat_ref[...]).astype(BF16)
    ot_ref[0, 0, 0] = _dot_nt(w2t_ref[0], act).astype(BF16)


def _compress(kvc, pos, w1, w2, w2t, feat, batch, seq):
    groups = NSA_GROUPS
    nc = seq // CMP_STRIDE
    return pl.pallas_call(
        _compress_kernel,
        grid=(2, batch, groups),
        in_specs=[pl.BlockSpec((1, seq, NSA_DK), lambda t, b, g: (t * groups + g, b, 0)),
                  pl.BlockSpec((1, CMP_LEN, NSA_DK), lambda t, b, g: (t, 0, 0)),
                  pl.BlockSpec((1, CMP_LEN * NSA_DK, CMP_HIDDEN), lambda t, b, g: (t, 0, 0)),
                  pl.BlockSpec((1, CMP_HIDDEN, LANES), lambda t, b, g: (t, 0, 0)),
                  pl.BlockSpec((1, NSA_DK, CMP_HIDDEN), lambda t, b, g: (t, 0, 0)),
                  pl.BlockSpec((nc, LANES), lambda t, b, g: (0, 0))],
        out_specs=[pl.BlockSpec((1, 1, 1, nc, LANES), lambda t, b, g: (t, b, g, 0, 0)),
                   pl.BlockSpec((1, 1, 1, NSA_DK, nc), lambda t, b, g: (t, b, g, 0, 0))],
        out_shape=[jax.ShapeDtypeStruct((2, batch, groups, nc, LANES), BF16),
                   jax.ShapeDtypeStruct((2, batch, groups, NSA_DK, nc), BF16)],
        compiler_params=_params(3),
        name="compress",
    )(kvc, pos, w1, w2, w2t, feat)


def _nsa_cmp_kernel(q_ref, kc_ref, vaug_ref, gt_ref, oc_ref, selt_ref, flag_ref, q_scr, m_ref, acc_ref, s_scr,
                    *, n_cmp):
    i = pl.program_id(2)
    t = NSA_TILE
    tq = CMP_TQ
    hpg = NSA_HPG
    _stack_heads(q_ref, q_scr)
    m_ref[...] = jnp.full_like(m_ref, NEG)
    acc_ref[...] = jnp.zeros_like(acc_ref)

    crow = lax.broadcasted_iota(jnp.int32, (t, tq), 0)
    qpos = i * tq + lax.broadcasted_iota(jnp.int32, (t, tq), 1)

    def keepf(j):
        c = j * t + crow
        return jnp.where(CMP_STRIDE * c + (CMP_LEN - 1) <= qpos, jnp.where(c < n_cmp, 1.0, 0.0), 0.0)

    count = jnp.minimum((i * tq + tq - CMP_LEN) // (CMP_STRIDE * t) + 1, kc_ref.shape[3] // t)
    _skewed_flash(count, lambda n: n, keepf, lambda j: kc_ref[0, 0, 0, pl.ds(pl.multiple_of(j * t, t), t), :],
                  lambda j: vaug_ref[0, 0, j], q_scr, s_scr, m_ref, acc_ref, 0)

    some = jnp.where(qpos[:1] >= CMP_LEN - 1, 1.0, 0.0)
    inv = jnp.concatenate([some] * hpg, axis=1) / acc_ref[0, NSA_DK + NS_PAD:NSA_DK + NS_PAD + 1]
    gates = jax.nn.sigmoid(jnp.concatenate([gt_ref[c] for c in range(tq // t)], axis=1))
    imp = jnp.zeros((NS_PAD, tq), F32)
    for pair in range(hpg // 2):
        halves = []
        for h in (2 * pair, 2 * pair + 1):
            hs = slice(h * tq, (h + 1) * tq)
            halves.append(gates[3 * h:3 * h + 1, :] * (acc_ref[0, :NSA_DK, hs] * inv[:, hs]))
            imp = imp + acc_ref[0, NSA_DK:NSA_DK + NS_PAD, hs] * inv[:, hs]
        oc_ref[:, pair * LANES:(pair + 1) * LANES] = jnp.concatenate(halves, axis=0).T.astype(BF16)

    blk = lax.broadcasted_iota(jnp.int32, (NS_PAD, tq), 0)
    blk_t = (i * tq + lax.broadcasted_iota(jnp.int32, (NS_PAD, tq), 1)) // SLC_LEN
    forced = jnp.where(blk == 0, 1.0, 0.0) + jnp.where(blk == blk_t, 1.0, 0.0) + jnp.where(blk == blk_t - 1, 1.0, 0.0)
    imp = jnp.where(forced > 0.5, FORCE_SCORE, jnp.where(blk <= blk_t, imp, -1.0))
    blkf = blk.astype(F32)
    sel = jnp.zeros((NS_PAD, tq), F32)
    for _ in range(SLC_TOPK):
        best = jnp.max(imp, axis=0, keepdims=True)
        pick = jnp.min(jnp.where(imp == best, blkf, float(NS_PAD)), axis=0, keepdims=True)
        hit = blkf == pick
        sel = jnp.where(hit, 1.0, sel)
        imp = jnp.where(hit, TAKEN, imp)
    selt_ref[0, 0] = sel
    for c in range(tq // t):
        flag_ref[c] = jnp.max(sel[:, c * t:(c + 1) * t].T, axis=0, keepdims=True).astype(jnp.int32)


def _nsa_cmp(q, kc, vaug, gt, batch, seq, n_cmp):
    t = NSA_TILE
    tq = CMP_TQ
    nq = seq // tq
    n = batch * seq
    ncp = kc.shape[3]
    sub = tq // t
    rows = lambda c: pl.BlockSpec((tq, c), lambda b, g, i: (b * nq + i, g))
    return pl.pallas_call(
        functools.partial(_nsa_cmp_kernel, n_cmp=n_cmp),
        grid=(batch, NSA_GROUPS, nq),
        in_specs=[pl.BlockSpec((sub, NSA_HPG * LANES, t), lambda b, g, i: (b * nq + i, g, 0)),
                  pl.BlockSpec((1, 1, 1, ncp, LANES), lambda b, g, i: (0, b, g, 0, 0)),
                  pl.BlockSpec((1, 1) + vaug.shape[2:], lambda b, g, i: (b, g, 0, 0, 0)),
                  pl.BlockSpec((sub, N_GATES, t), lambda b, g, i: (b * nq + i, g, 0))],
        out_specs=[rows(NSA_HPG * NSA_DK),
                   pl.BlockSpec((1, 1, NS_PAD, tq), lambda b, g, i: (b, g, 0, i)),
                   pl.BlockSpec((sub, 1, NS_PAD), lambda b, g, i: ((b * NSA_GROUPS + g) * nq + i, 0, 0))],
        out_shape=[jax.ShapeDtypeStruct((n, NSA_HEADS * NSA_DK), BF16),
                   jax.ShapeDtypeStruct((batch, NSA_GROUPS, NS_PAD, seq), F32),
                   jax.ShapeDtypeStruct((batch * NSA_GROUPS * seq // t, 1, NS_PAD), jnp.int32)],
        scratch_shapes=[pltpu.VMEM((LANES, NSA_HPG * tq), BF16), pltpu.VMEM((1, 1, NSA_HPG * tq), F32),
                        pltpu.VMEM((1, NSA_DK + NS_PAD + ONES_ROWS, NSA_HPG * tq), F32),
                        pltpu.VMEM((2, 2 * t, NSA_HPG * tq), F32)],
        compiler_params=_params(3),
        name="nsa_cmp",
    )(q, kc, vaug, gt)


ID_BITS = 8
IDS_PER_WORD = 32 // ID_BITS


def _nsa_attn_kernel(counts_ref, words_ref, q_ref, selt_ref, oc_ref, gt_ref, ks_ref, kw_ref, vst_ref, vwt_ref, o_ref,
                     q_scr, m_ref, acc_ref, s_scr, *, nq, words_per_step):
    b = pl.program_id(0)
    g = pl.program_id(1)
    i = pl.program_id(2)
    t = NSA_TILE
    tq = ATT_TQ
    sub = tq // t
    hpg = NSA_HPG
    _stack_heads(q_ref, q_scr)
    m_ref[...] = jnp.full_like(m_ref, NEG)
    acc_ref[...] = jnp.zeros_like(acc_ref)

    krow = lax.broadcasted_iota(jnp.int32, (t, tq), 0)
    qpos = i * tq + lax.broadcasted_iota(jnp.int32, (t, tq), 1)
    blocks_per_tile = t // SLC_LEN
    slc, win = 0, 1
    step = (b * NSA_GROUPS + g) * nq + i

    def run_branch(branch, count, tile_at, keepf_fn, k_ref, vt_ref):
        _skewed_flash(count, tile_at, keepf_fn, lambda j: k_ref[pl.ds(pl.multiple_of(j * t, t), t), :],
                      lambda j: vt_ref[0, j], q_scr, s_scr, m_ref, acc_ref, branch)

    def slc_tile(n):
        word = words_ref[step * words_per_step + n // IDS_PER_WORD]
        return lax.shift_right_logical(word, ID_BITS * (n % IDS_PER_WORD)) & (2 ** ID_BITS - 1)

    def slc_keepf(j):
        dist = qpos - (j * t + krow)
        sel_rows = [selt_ref[0, 0, pl.ds(blocks_per_tile * j + r, 1), :] for r in range(blocks_per_tile)]
        picked = sel_rows[-1]
        for r in range(blocks_per_tile - 2, -1, -1):
            picked = jnp.where(krow < (r + 1) * SLC_LEN, sel_rows[r], picked)
        return jnp.where(dist >= 0, picked, 0.0)

    def win_keepf(j):
        dist = qpos - (j * t + krow)
        return jnp.where(dist >= 0, jnp.where(dist < WINDOW, 1.0, 0.0), 0.0)

    run_branch(slc, counts_ref[step], slc_tile, slc_keepf, ks_ref, vst_ref)
    first = jnp.maximum(i * sub - WINDOW // t, 0)
    run_branch(win, (i + 1) * sub - first, lambda n: first + n, win_keepf, kw_ref, vwt_ref)

    gates = jax.nn.sigmoid(jnp.concatenate([gt_ref[c] for c in range(sub)], axis=1))
    inv_s = 1.0 / acc_ref[slc, NSA_DK:NSA_DK + 1]
    inv_w = 1.0 / acc_ref[win, NSA_DK:NSA_DK + 1]
    for pair in range(hpg // 2):
        halves = []
        for h in (2 * pair, 2 * pair + 1):
            hs = slice(h * tq, (h + 1) * tq)
            halves.append(gates[3 * h + 1:3 * h + 2, :] * (acc_ref[slc, :NSA_DK, hs] * inv_s[:, hs])
                          + gates[3 * h + 2:3 * h + 3, :] * (acc_ref[win, :NSA_DK, hs] * inv_w[:, hs]))
        ps = slice(pair * LANES, (pair + 1) * LANES)
        o_ref[:, ps] = (oc_ref[:, ps].astype(F32) + jnp.concatenate(halves, axis=0).T).astype(BF16)


def _nsa_attn(counts, words, q, selt, oc, gt, ksw, vt, batch, seq):
    t = NSA_TILE
    tq = ATT_TQ
    sub = tq // t
    nq = seq // tq
    nk = seq // t
    n = batch * seq
    words_per_step = words.shape[0] // (batch * NSA_GROUPS * nq)
    rows = lambda c: pl.BlockSpec((tq, c), lambda b, g, i, *_: (b * nq + i, g))
    tiles = lambda r: pl.BlockSpec((sub, r, t), lambda b, g, i, *_: (b * nq + i, g, 0))
    key = lambda which: pl.BlockSpec((seq, LANES), lambda b, g, i, *_: (b, which * NSA_GROUPS + g))
    vt4 = vt.reshape(batch, nk, _VT_OUT, t)
    vrows = NSA_DK + ONES_ROWS
    val = lambda which: pl.BlockSpec((1, nk, vrows, t), lambda b, g, i, *_: (b, 0, which * NSA_GROUPS + g, 0))
    cols = NSA_HPG * tq
    grid_spec = pltpu.PrefetchScalarGridSpec(
        num_scalar_prefetch=2,
        grid=(batch, NSA_GROUPS, nq),
        in_specs=[tiles(NSA_HPG * LANES),
                  pl.BlockSpec((1, 1, NS_PAD, tq), lambda b, g, i, *_: (b, g, 0, i)),
                  rows(NSA_HPG * NSA_DK), tiles(N_GATES), key(0), key(1), val(0), val(1)],
        out_specs=rows(NSA_HPG * NSA_DK),
        scratch_shapes=[pltpu.VMEM((LANES, cols), BF16), pltpu.VMEM((2, 1, cols), F32),
                        pltpu.VMEM((2, vrows, cols), F32), pltpu.VMEM((2, 2 * t, cols), F32)],
    )
    return pl.pallas_call(
        functools.partial(_nsa_attn_kernel, nq=nq, words_per_step=words_per_step),
        grid_spec=grid_spec,
        out_shape=jax.ShapeDtypeStruct((n, NSA_HEADS * NSA_DK), BF16),
        compiler_params=_params(3),
        name="nsa_attn",
    )(counts, words, q, selt, oc, gt, ksw, ksw, vt4, vt4)


def _outproj_kernel(x_ref, om_ref, on_ref, wm_ref, wn_ref, o_ref):
    o_ref[...] = x_ref[...] + _dot(om_ref[...], wm_ref[...]) + _dot(on_ref[...], wn_ref[...])


def _out_proj(x2, o_mla, o_nsa, wm, wn, tm=512):
    n, d = x2.shape
    row = lambda c: pl.BlockSpec((tm, c), lambda i: (i, 0))
    return pl.pallas_call(
        _outproj_kernel,
        grid=(n // tm,),
        in_specs=[row(d), row(o_mla.shape[1]), row(o_nsa.shape[1]), _resident(wm.shape), _resident(wn.shape)],
        out_specs=row(d),
        out_shape=jax.ShapeDtypeStruct((n, d), F32),
        compiler_params=_params(1),
        name="out_proj",
    )(x2, o_mla, o_nsa, wm, wn)


def _ffn_kernel(x_ref, g_ref, wg_ref, wu_ref, wd_ref, gf_ref, o_ref, h_scr, acc_scr):
    f = pl.program_id(1)

    @pl.when(f == 0)
    def _():
        h_scr[...] = _rmsnorm(x_ref[...], g_ref[...]).astype(BF16)
        acc_scr[...] = jnp.zeros_like(acc_scr)

    h = h_scr[...]
    gate = _dot(h, wg_ref[...])
    act = (gate * jax.nn.sigmoid(gate)) * _dot(h, wu_ref[...])
    acc_scr[...] += _dot(act.astype(BF16), wd_ref[...])

    @pl.when(f == pl.num_programs(1) - 1)
    def _():
        o_ref[...] = _rmsnorm(x_ref[...] + acc_scr[...], gf_ref[...])


def _ffn(x1, g, wg, wu, wd, gf, tm=512, tf=512):
    n, d = x1.shape
    dff = wg.shape[1]
    return pl.pallas_call(
        _ffn_kernel,
        grid=(n // tm, dff // tf),
        in_specs=[pl.BlockSpec((tm, d), lambda i, f: (i, 0)), _resident((1, d)),
                  pl.BlockSpec((d, tf), lambda i, f: (0, f)), pl.BlockSpec((d, tf), lambda i, f: (0, f)),
                  pl.BlockSpec((tf, d), lambda i, f: (f, 0)), _resident((1, d))],
        out_specs=pl.BlockSpec((tm, d), lambda i, f: (i, 0)),
        out_shape=jax.ShapeDtypeStruct((n, d), F32),
        scratch_shapes=[pltpu.VMEM((tm, d), BF16), pltpu.VMEM((tm, d), F32)],
        compiler_params=_params(2),
        name="ffn",
    )(x1, g, wg, wu, wd, gf)


def _pad_cols(w, width):
    return jnp.pad(w, ((0, 0), (0, width - w.shape[1])))


def _rot_cols(w):
    half = w.shape[1] // 2
    return jnp.concatenate([-w[:, half:], w[:, :half]], axis=1)


def _fused_in_weights(w_in):
    sizes = (MLA_Q_LORA, MLA_KV_LORA, MLA_ROPE, NSA_HEADS * NSA_DK) + (NSA_GROUPS * NSA_DK,) * 6 + (3 * NSA_HEADS,)
    offs = np.cumsum(sizes)[:-1].tolist()
    cq, ckv, kr, q, kc, vc, ks, vs, kw, vw, gate = jnp.split(w_in, offs, axis=1)
    d = w_in.shape[0]
    q_scaled = q * (NSA_DK ** -0.5 * LOG2E)
    per_group = lambda w, c: _pad_cols(w.reshape(d * NSA_GROUPS, c), LANES).reshape(d, NSA_GROUPS * LANES)
    kr_rot = _rot_cols(kr)
    cols = [cq, ckv, kr, kr, kr_rot, kr_rot, kc, vc, per_group(ks, NSA_DK), per_group(kw, NSA_DK)]
    w = jnp.concatenate(cols, axis=1).astype(BF16)
    assert w.shape[1] == _IN_COLS["ksw"][1]
    rows_t = jnp.concatenate([q_scaled, vs, vw, gate], axis=1).T
    pad = -rows_t.shape[0] % 16
    return w, jnp.pad(rows_t, ((0, pad), (0, 0))).astype(BF16)


def _slope_features(slopes2):
    s1 = slopes2.astype(BF16).astype(F32)
    s2 = (slopes2 - s1).astype(BF16).astype(F32)
    s3 = (slopes2 - s1 - s2).astype(BF16).astype(F32)
    pieces = jnp.stack([s1, s2, s3, s1, s2, s3], axis=1)
    column = jnp.pad(pieces, ((0, 0), (0, LANES - NSA_DK - 6))).reshape(-1, 1)
    return jnp.broadcast_to(column, (column.shape[0], NSA_TILE)).astype(BF16)


def _position_features(pos):
    hi = (POS_SPLIT * (pos // POS_SPLIT)).astype(F32)
    lo = (pos % POS_SPLIT).astype(F32)
    return jnp.pad(jnp.stack([hi, hi, hi, lo, lo, lo], axis=1), ((0, 0), (NSA_DK, LANES - NSA_DK - 6)))


def _fused_uq_weight(w_uq):
    d = w_uq.shape[0]
    w = w_uq.reshape(d, MLA_HEADS, MLA_NOPE + MLA_ROPE)
    nope = w[:, :, :MLA_NOPE].reshape(d, MLA_HEADS * MLA_NOPE)
    rope = w[:, :, MLA_NOPE:]
    rope_rot = jnp.concatenate([-rope[:, :, MLA_ROPE // 2:], rope[:, :, :MLA_ROPE // 2]], axis=2)
    flat = lambda r: r.reshape(d, MLA_HEADS * MLA_ROPE)
    return jnp.concatenate([nope, flat(rope), flat(rope_rot)], axis=1).astype(BF16)


def _rope_tables(seq):
    inv = ROPE_THETA ** (-jnp.arange(0, MLA_ROPE, 2, dtype=F32) / MLA_ROPE)
    ang = jnp.arange(seq, dtype=F32)[:, None] * inv[None, :]
    reps = 2 * LANES // MLA_ROPE
    return jnp.tile(jnp.cos(ang), (1, reps)), jnp.tile(jnp.sin(ang), (1, reps))


def _overlap_matrix(seq, n_cmp, ncp):
    cmp_start = CMP_STRIDE * np.arange(n_cmp)
    slc_start = SLC_LEN * np.arange(seq // SLC_LEN)
    ov = np.clip(np.minimum(cmp_start[:, None] + CMP_LEN, slc_start[None, :] + SLC_LEN)
                 - np.maximum(cmp_start[:, None], slc_start[None, :]), 0, None).astype(np.float32) / CMP_STRIDE
    out = np.zeros((NS_PAD, ncp), np.float32)
    out[:ov.shape[1], :n_cmp] = ov.T
    return jnp.asarray(out, BF16)


def _active_tiles(flags, seq):
    t_blocks = NSA_TILE // SLC_LEN
    sub = ATT_TQ // NSA_TILE
    nq = seq // ATT_TQ
    steps = flags.shape[0] // sub
    tiles = flags.reshape(steps, sub, NS_PAD // t_blocks, t_blocks).max(axis=(1, 3))
    tile_id = jnp.arange(tiles.shape[1], dtype=jnp.int32)[None, :]
    last_tile = ((jnp.arange(steps, dtype=jnp.int32) % nq) * sub + (sub - 1))[:, None]
    active = (tiles > 0) & (tile_id <= last_tile)
    order = jnp.argsort(jnp.where(active, tile_id, tile_id + tiles.shape[1]), axis=-1).astype(jnp.int32)
    shifts = ID_BITS * jnp.arange(IDS_PER_WORD, dtype=jnp.int32)
    words = (order.reshape(steps, -1, IDS_PER_WORD) << shifts).sum(axis=-1).astype(jnp.int32)
    return active.sum(axis=-1).astype(jnp.int32), words.reshape(-1)


def kernel(x, attn_norm_g, w_in, mla_q_norm_g, mla_kv_norm_g, w_uq, w_uk, w_uv, cmp_pos_k, cmp_pos_v, w_cmp_k1,
           w_cmp_k2, w_cmp_v1, w_cmp_v2, w_o, ffn_norm_g, w_gate, w_up, w_down, final_norm_g):
    batch, seq, d = x.shape
    n = batch * seq
    assert w_in.shape[0] == 1, "the final RMSNorm is fused into the FFN kernel of a single layer"
    assert seq % (CMP_STRIDE * NSA_TILE) == 0 and seq % CMP_TQ == 0 and seq // SLC_LEN <= NS_PAD
    assert seq // POS_SPLIT <= BF16_EXACT_INT and POS_SPLIT <= BF16_EXACT_INT and seq // NSA_TILE <= 2 ** ID_BITS
    n_cmp = (seq - CMP_LEN) // CMP_STRIDE + 1
    ncp = seq // CMP_STRIDE
    cos_t, sin_t = _rope_tables(seq)
    ov_t = _overlap_matrix(seq, n_cmp, ncp)
    slopes2 = 2.0 ** (-8.0 * jnp.arange(1, NSA_HEADS + 1, dtype=F32) / NSA_HEADS) * LOG2E
    qfeat = _slope_features(slopes2)
    kfeat = _position_features(jnp.arange(seq, dtype=jnp.int32))
    x2 = x.reshape(n, d)

    w_fused, w_fused_t = _fused_in_weights(w_in[0])
    cq, ckv, kr, ksw, kvc, q, vt, gt = _in_proj(
        x2, attn_norm_g[0][None], w_fused, w_fused_t, qfeat, cos_t, sin_t, kfeat, seq)

    qn, qr, kn, vt_mla = _mla_up(cq, ckv, mla_q_norm_g[0][None], mla_kv_norm_g[0][None], _fused_uq_weight(w_uq[0]),
                                 w_uk[0].astype(BF16), w_uv[0].T.astype(BF16), cos_t, sin_t, seq)
    o_mla = _mla_attn(qn, qr, kn, kr, vt_mla, batch, seq)

    pos = jnp.stack([cmp_pos_k[0], cmp_pos_v[0]])
    w1 = jnp.stack([w_cmp_k1[0], w_cmp_v1[0]]).astype(BF16)
    w2 = jnp.stack([_pad_cols(w_cmp_k2[0], LANES), _pad_cols(w_cmp_v2[0], LANES)]).astype(BF16)
    w2t = jnp.stack([w_cmp_k2[0].T, w_cmp_v2[0].T]).astype(BF16)
    cmp_end = CMP_STRIDE * jnp.arange(ncp, dtype=jnp.int32) + (CMP_LEN - 1)
    cmp_rows, cmp_t = _compress(kvc, pos, w1, w2, w2t, _position_features(cmp_end), batch, seq)
    ones = jnp.zeros((ONES_ROWS, ncp), BF16).at[0].set(1.0)
    const_rows = jnp.broadcast_to(jnp.concatenate([ov_t, ones]), (batch, NSA_GROUPS, NS_PAD + ONES_ROWS, ncp))
    vaug = jnp.concatenate([cmp_t[1], const_rows], axis=2)
    vaug = vaug.reshape(batch, NSA_GROUPS, vaug.shape[2], ncp // NSA_TILE, NSA_TILE).transpose(0, 1, 3, 2, 4)

    oc, selt, flags = _nsa_cmp(q, cmp_rows, vaug, gt, batch, seq, n_cmp)
    counts, words = _active_tiles(flags.reshape(flags.shape[0], NS_PAD), seq)
    o_nsa = _nsa_attn(counts, words, q, selt, oc, gt, ksw, vt, batch, seq)

    split = MLA_HEADS * MLA_V
    x1 = _out_proj(x2, o_mla, o_nsa, w_o[0][:split].astype(BF16), w_o[0][split:].astype(BF16))
    out = _ffn(x1, ffn_norm_g[0][None], w_gate[0].astype(BF16), w_up[0].astype(BF16), w_down[0].astype(BF16),
               final_norm_g[None])
    return out.reshape(batch, seq, d)
```

```python
import functools
import math

import numpy as np
import jax
import jax.numpy as jnp
from jax import lax
from jax.experimental import pallas as pl
from jax.experimental.pallas import tpu as pltpu

F32 = jnp.float32
BF16 = jnp.bfloat16

EPS = 1e-6
NEG = -1e30
LOG2E = math.log2(math.e)
LANES = 128

MLA_HEADS = 8
MLA_Q_LORA = 512
MLA_KV_LORA = 256
MLA_NOPE = 128
MLA_ROPE = 64
MLA_V = 128
ROPE_THETA = 10000.0
MLA_TK = 512

NSA_HEADS = 16
NSA_GROUPS = 2
NSA_HPG = NSA_HEADS // NSA_GROUPS
NSA_DK = 64
CMP_LEN = 32
CMP_STRIDE = 16
CMP_HIDDEN = 128
SLC_LEN = 64
SLC_TOPK = 16
WINDOW = 512
FORCE_SCORE = 1e4
TAKEN = -3e38
NSA_TILE = 128
NS_PAD = 128
CMP_TQ = 512
ATT_TQ = 256
N_GATES = 3 * NSA_HPG
POS_SPLIT = 64
BF16_EXACT_INT = 256
ONES_ROWS = 16

VMEM_LIMIT = 56 * 1024 * 1024


def _params(n_axes):
    return pltpu.CompilerParams(dimension_semantics=("arbitrary",) * n_axes, vmem_limit_bytes=VMEM_LIMIT)


def _resident(shape):
    zeros = (0,) * len(shape)
    return pl.BlockSpec(shape, lambda *_: zeros, pipeline_mode=pl.Buffered(1))


def _rmsnorm(x, g):
    return x * lax.rsqrt(jnp.mean(x * x, axis=-1, keepdims=True) + EPS) * g


def _dot(a, b):
    return jnp.dot(a, b, preferred_element_type=F32)


def _dot_nt(a, b):
    return lax.dot_general(a, b, (((1,), (1,)), ((), ())), preferred_element_type=F32)


def _ones_rows(width):
    return jnp.where(lax.broadcasted_iota(jnp.int32, (ONES_ROWS, width), 0) == 0, 1.0, 0.0).astype(BF16)


def _flash_weights(s, m_ref, idx):
    m_prev = m_ref[idx]
    m_new = jnp.maximum(m_prev, jnp.max(s, axis=0, keepdims=True))
    m_ref[idx] = m_new
    return jnp.exp2(m_prev - m_new), jnp.exp2((s - m_new).astype(BF16))


def _stack_heads(qt_ref, qt_scr):
    tiles, rows, t = qt_ref.shape
    for h in range(rows // LANES):
        for c in range(tiles):
            qt_scr[:, (h * tiles + c) * t:(h * tiles + c + 1) * t] = qt_ref[c, h * LANES:(h + 1) * LANES, :]


def _flash_update(s, v_t, m_ref, acc_ref, idx):
    alpha, p = _flash_weights(s, m_ref, idx)
    acc_ref[idx] = alpha * acc_ref[idx] + _dot(v_t, p)


def _skewed_flash(count, tile_at, keepf_fn, k_tile, vt_tile, qt_scr, s_scr, m_ref, acc_ref, idx):
    pairs = (count + 1) // 2

    def pair(u):
        return [tile_at(jnp.minimum(2 * u + r, count - 1)) for r in range(2)]

    def scores(u, slot):
        j0, j1 = pair(jnp.minimum(u, pairs - 1))
        s_scr[slot] = _dot(jnp.concatenate([k_tile(j0), k_tile(j1)], axis=0), qt_scr[...])

    def consume(u, slot):
        j0, j1 = pair(u)
        second = jnp.where(2 * u + 1 < count, 1.0, 0.0)
        bias = jnp.concatenate([jnp.where(keepf_fn(j0) > 0.5, 0.0, NEG),
                                jnp.where(keepf_fn(j1) * second > 0.5, 0.0, NEG)], axis=0)
        s = s_scr[slot]
        s = s + jnp.concatenate([bias] * (s.shape[1] // bias.shape[1]), axis=1)
        _flash_update(s, jnp.concatenate([vt_tile(j0), vt_tile(j1)], axis=1), m_ref, acc_ref, idx)

    def body(w, carry):
        u = 2 * w
        scores(u + 1, 1)
        consume(u, 0)
        scores(u + 2, 0)
        consume(u + 1, 1)
        return carry

    scores(0, 0)
    lax.fori_loop(0, pairs // 2, body, 0)

    @pl.when(pairs % 2 == 1)
    def _():
        consume(pairs - 1, 0)


_IN_COLS = dict(cq=(0, 512), ckv=(512, 768), kra=(768, 896), krb=(896, 1024), kvc=(1024, 1280), ksw=(1280, 1792))
_QT_ROWS = NSA_HEADS * LANES
_Q_ROWS = NSA_HEADS * NSA_DK
_VT_ROWS = 2 * NSA_GROUPS * NSA_DK
_VT_OUT = _VT_ROWS // NSA_DK * (NSA_DK + ONES_ROWS)


def _inproj_kernel(x_ref, g_ref, w_ref, wt_ref, qfeat_ref, cos_ref, sin_ref, kfeat_ref, cq_ref, ckv_ref, kr_ref,
                   ksw_ref, kvc_ref, qt_ref, vt_ref, gt_ref):
    h = _rmsnorm(x_ref[...], g_ref[...]).astype(BF16)

    def mm(name):
        lo, hi = _IN_COLS[name]
        return _dot(h, w_ref[:, lo:hi])

    cq_ref[...] = mm("cq")
    ckv_ref[...] = mm("ckv")
    kr_ref[...] = (mm("kra") * cos_ref[...] + mm("krb") * sin_ref[...]).astype(BF16)
    kvc = mm("kvc")
    for k in range(kvc_ref.shape[0]):
        kvc_ref[k] = kvc[:, k * NSA_DK:(k + 1) * NSA_DK]
    kfeat = kfeat_ref[...]
    ksw_ref[...] = (mm("ksw") + jnp.concatenate([kfeat] * (2 * NSA_GROUPS), axis=1)).astype(BF16)
    t = _dot_nt(wt_ref[...], h)
    ones = _ones_rows(NSA_TILE)
    qfeat = qfeat_ref[...]
    for c in range(vt_ref.shape[0]):
        cs = slice(c * NSA_TILE, (c + 1) * NSA_TILE)
        pieces = []
        for h in range(NSA_HEADS):
            pieces += [t[h * NSA_DK:(h + 1) * NSA_DK, cs].astype(BF16), qfeat[h * NSA_DK:(h + 1) * NSA_DK]]
        qt_ref[c] = jnp.concatenate(pieces, axis=0)
        pieces = []
        for k in range(_VT_ROWS // NSA_DK):
            rows = slice(_Q_ROWS + k * NSA_DK, _Q_ROWS + (k + 1) * NSA_DK)
            pieces += [t[rows, cs].astype(BF16), ones]
        vt_ref[c] = jnp.concatenate(pieces, axis=0)
        gt_ref[c] = t[_Q_ROWS + _VT_ROWS:_Q_ROWS + _VT_ROWS + NSA_GROUPS * N_GATES, cs]


def _in_proj(x2, g, w, wt, qfeat, cos_t, sin_t, kfeat, seq, tm=256):
    n, d = x2.shape
    nt = seq // tm
    widths = {k: hi - lo for k, (lo, hi) in _IN_COLS.items()}
    row = lambda c: pl.BlockSpec((tm, c), lambda i: (i, 0))
    tab = pl.BlockSpec((tm, LANES), lambda i: (i % nt, 0))
    outs = [("cq", F32), ("ckv", F32), ("kra", BF16), ("ksw", BF16)]
    n_kvc = widths["kvc"] // NSA_DK
    tiles = tm // NSA_TILE
    tspec = lambda rows: pl.BlockSpec((tiles, rows, NSA_TILE), lambda i: (i, 0, 0))
    return pl.pallas_call(
        _inproj_kernel,
        grid=(n // tm,),
        in_specs=[row(d), _resident((1, d)), _resident(w.shape), _resident(wt.shape), _resident(qfeat.shape),
                  tab, tab, tab],
        out_specs=[row(widths[k]) for k, _ in outs] + [
            pl.BlockSpec((n_kvc, tm, NSA_DK), lambda i: (0, i, 0)),
            tspec(_QT_ROWS), tspec(_VT_OUT), tspec(NSA_GROUPS * N_GATES)],
        out_shape=[jax.ShapeDtypeStruct((n, widths[k]), dt) for k, dt in outs] + [
            jax.ShapeDtypeStruct((n_kvc, n, NSA_DK), F32),
            jax.ShapeDtypeStruct((n // NSA_TILE, _QT_ROWS, NSA_TILE), BF16),
            jax.ShapeDtypeStruct((n // NSA_TILE, _VT_OUT, NSA_TILE), BF16),
            jax.ShapeDtypeStruct((n // NSA_TILE, NSA_GROUPS * N_GATES, NSA_TILE), F32)],
        compiler_params=_params(1),
        name="in_proj",
    )(x2, g, w, wt, qfeat, cos_t, sin_t, kfeat)


def _mlaup_kernel(cq_ref, ckv_ref, gq_ref, gkv_ref, wq_ref, wk_ref, wvt_ref, cos_ref, sin_ref, qn_ref, qr_ref,
                  kn_ref, vt_ref, *, scale):
    cqn = _rmsnorm(cq_ref[...], gq_ref[...]).astype(BF16)
    ckvn = _rmsnorm(ckv_ref[...], gkv_ref[...]).astype(BF16)
    hn = MLA_HEADS * MLA_NOPE
    hr = MLA_HEADS * MLA_ROPE
    qn_ref[...] = (_dot(cqn, wq_ref[:, :hn]) * scale).astype(BF16)
    a = _dot(cqn, wq_ref[:, hn:hn + hr])
    b = _dot(cqn, wq_ref[:, hn + hr:])
    cos = cos_ref[...]
    sin = sin_ref[...]
    for p in range(hr // LANES):
        sl = slice(p * LANES, (p + 1) * LANES)
        qr_ref[:, sl] = ((a[:, sl] * cos + b[:, sl] * sin) * scale).astype(BF16)
    kn_ref[...] = _dot(ckvn, wk_ref[...]).astype(BF16)
    v_t = _dot_nt(wvt_ref[...], ckvn).astype(BF16)
    ones = _ones_rows(MLA_TK)
    for c in range(vt_ref.shape[0]):
        pieces = []
        for h in range(MLA_HEADS):
            pieces += [v_t[h * MLA_V:(h + 1) * MLA_V, c * MLA_TK:(c + 1) * MLA_TK], ones]
        vt_ref[c] = jnp.concatenate(pieces, axis=0)


def _mla_up(cq, ckv, gq, gkv, wq, wk, wvt, cos_t, sin_t, seq, tm=512):
    n = cq.shape[0]
    nt = seq // tm
    hn = MLA_HEADS * MLA_NOPE
    hr = MLA_HEADS * MLA_ROPE
    hv = MLA_HEADS * (MLA_V + ONES_ROWS)
    row = lambda c: pl.BlockSpec((tm, c), lambda i: (i, 0))
    tab = pl.BlockSpec((tm, LANES), lambda i: (i % nt, 0))
    scale = (MLA_NOPE + MLA_ROPE) ** -0.5 * LOG2E
    return pl.pallas_call(
        functools.partial(_mlaup_kernel, scale=scale),
        grid=(n // tm,),
        in_specs=[row(MLA_Q_LORA), row(MLA_KV_LORA), _resident(gq.shape), _resident(gkv.shape),
                  _resident(wq.shape), _resident(wk.shape), _resident(wvt.shape), tab, tab],
        out_specs=[row(hn), row(hr), row(hn), pl.BlockSpec((tm // MLA_TK, hv, MLA_TK), lambda i: (i, 0, 0))],
        out_shape=[jax.ShapeDtypeStruct((n, hn), BF16), jax.ShapeDtypeStruct((n, hr), BF16),
                   jax.ShapeDtypeStruct((n, hn), BF16), jax.ShapeDtypeStruct((n // MLA_TK, hv, MLA_TK), BF16)],
        compiler_params=_params(1),
        name="mla_up",
    )(cq, ckv, gq, gkv, wq, wk, wvt, cos_t, sin_t)


def _mla_attn_kernel(qn_ref, qr_ref, kn_ref, kr_ref, vt_ref, o_ref, m_ref, acc_ref, s_scr, qt_scr, *, tq):
    i = pl.program_id(2)
    tk = MLA_TK
    lane = lax.broadcasted_iota(jnp.int32, (tq, LANES), 1)
    krow = lax.broadcasted_iota(jnp.int32, (tk, tq), 0)
    qcol = lax.broadcasted_iota(jnp.int32, (tk, tq), 1)
    qr = qr_ref[...]
    zero = jnp.zeros_like(qr)
    for hh in range(2):
        q_cat = jnp.concatenate([qn_ref[:, hh * LANES:(hh + 1) * LANES],
                                 jnp.where((lane < MLA_ROPE) == (hh == 0), qr, zero)], axis=1)
        qt_scr[hh] = q_cat.astype(F32).T.astype(BF16)
    m_ref[...] = jnp.full_like(m_ref, NEG)
    acc_ref[...] = jnp.zeros_like(acc_ref)
    vrows = MLA_V + ONES_ROWS

    def scores(j, slot):
        ks = pl.ds(pl.multiple_of(j * tk, tk), tk)
        k_rope = kr_ref[ks, :]
        for hh in range(2):
            hs = slice(hh * LANES, (hh + 1) * LANES)
            s_scr[slot, hh] = _dot(jnp.concatenate([kn_ref[ks, hs], k_rope], axis=1), qt_scr[hh])

    def consume(j, slot, diagonal):
        weights = []
        for hh in range(2):
            s = s_scr[slot, hh]
            if diagonal:
                s = jnp.where(j * tk + krow <= i * tq + qcol, s, NEG)
            weights.append(_flash_weights(s, m_ref, hh))
        for hh in range(2):
            alpha, p = weights[hh]
            acc_ref[hh] = alpha * acc_ref[hh] + _dot(vt_ref[0, j, hh * vrows:(hh + 1) * vrows, :], p)

    def body(u, carry):
        j = 2 * u
        scores(j + 1, 1)
        consume(j, 0, False)
        scores(j + 2, 0)
        consume(j + 1, 1, False)
        return carry

    assert tq == tk, "the last key tile of a query tile must be its only diagonal tile"
    scores(0, 0)
    lax.fori_loop(0, i // 2, body, 0)

    @pl.when(i % 2 == 1)
    def _():
        scores(i, 1)
        consume(i - 1, 0, False)
        consume(i, 1, True)

    @pl.when(i % 2 == 0)
    def _():
        consume(i, 0, True)

    for hh in range(2):
        o_t = acc_ref[hh, :MLA_V] / acc_ref[hh, MLA_V:MLA_V + 1]
        for c in range(tq // LANES):
            o_ref[c * LANES:(c + 1) * LANES, hh * LANES:(hh + 1) * LANES] = (
                o_t[:, c * LANES:(c + 1) * LANES].T.astype(BF16))


def _mla_attn(qn, qr, kn, kr, vt, batch, seq, tq=512):
    n = qn.shape[0]
    nq = seq // tq
    nk = seq // MLA_TK
    pairs = MLA_HEADS // 2
    pw = 2 * LANES
    qspec = lambda c: pl.BlockSpec((tq, c), lambda b, p, i: (b * nq + i, p))
    vrows = MLA_V + ONES_ROWS
    vt4 = vt.reshape(batch, nk, MLA_HEADS * vrows, MLA_TK)
    return pl.pallas_call(
        functools.partial(_mla_attn_kernel, tq=tq),
        grid=(batch, pairs, nq),
        in_specs=[qspec(pw), qspec(LANES), pl.BlockSpec((seq, pw), lambda b, p, i: (b, p)),
                  pl.BlockSpec((seq, LANES), lambda b, p, i: (b, 0)),
                  pl.BlockSpec((1, nk, 2 * vrows, MLA_TK), lambda b, p, i: (b, 0, p, 0))],
        out_specs=qspec(pw),
        out_shape=jax.ShapeDtypeStruct((n, MLA_HEADS * MLA_V), BF16),
        scratch_shapes=[pltpu.VMEM((2, 1, tq), F32), pltpu.VMEM((2, vrows, tq), F32),
                        pltpu.VMEM((2, 2, MLA_TK, tq), F32), pltpu.VMEM((2, 2 * LANES, tq), BF16)],
        compiler_params=_params(3),
        name="mla_attn",
    )(qn, qr, kn, kr, vt4)


def _compress_kernel(r_ref, pos_ref, w1_ref, w2_ref, w2t_ref, feat_ref, o_ref, ot_ref):
    nc = o_ref.shape[3]
    a = jnp.zeros((nc, CMP_HIDDEN), F32)
    b = jnp.zeros((nc, CMP_HIDDEN), F32)
    for l in range(CMP_STRIDE):
        x = r_ref[0, pl.ds(l, nc, stride=CMP_STRIDE), :]
        lo = slice(l * NSA_DK, (l + 1) * NSA_DK)
        hi = slice((CMP_STRIDE + l) * NSA_DK, (CMP_STRIDE + l + 1) * NSA_DK)
        a = a + _dot((x + pos_ref[0, l:l + 1, :]).astype(BF16), w1_ref[0, lo, :])
        b = b + _dot((x + pos_ref[0, CMP_STRIDE + l:CMP_STRIDE + l + 1, :]).astype(BF16), w1_ref[0, hi, :])
    hid = a + pltpu.roll(b, nc - 1, 0)
    act = (hid * jax.nn.sigmoid(hid)).astype(BF16)
    o_ref[0, 0, 0] = (_dot(act, w2_ref[0]) + feat_ref[...]).astype(BF16)
    ot_ref[0, 0, 0] = _dot_nt(w2t_ref[0], act).astype(BF16)


def _compress(kvc, pos, w1, w2, w2t, feat, batch, seq):
    groups = NSA_GROUPS
    nc = seq // CMP_STRIDE
    return pl.pallas_call(
        _compress_kernel,
        grid=(2, batch, groups),
        in_specs=[pl.BlockSpec((1, seq, NSA_DK), lambda t, b, g: (t * groups + g, b, 0)),
                  pl.BlockSpec((1, CMP_LEN, NSA_DK), lambda t, b, g: (t, 0, 0)),
                  pl.BlockSpec((1, CMP_LEN * NSA_DK, CMP_HIDDEN), lambda t, b, g: (t, 0, 0)),
                  pl.BlockSpec((1, CMP_HIDDEN, LANES), lambda t, b, g: (t, 0, 0)),
                  pl.BlockSpec((1, NSA_DK, CMP_HIDDEN), lambda t, b, g: (t, 0, 0)),
                  pl.BlockSpec((nc, LANES), lambda t, b, g: (0, 0))],
        out_specs=[pl.BlockSpec((1, 1, 1, nc, LANES), lambda t, b, g: (t, b, g, 0, 0)),
                   pl.BlockSpec((1, 1, 1, NSA_DK, nc), lambda t, b, g: (t, b, g, 0, 0))],
        out_shape=[jax.ShapeDtypeStruct((2, batch, groups, nc, LANES), BF16),
                   jax.ShapeDtypeStruct((2, batch, groups, NSA_DK, nc), BF16)],
        compiler_params=_params(3),
        name="compress",
    )(kvc, pos, w1, w2, w2t, feat)


def _nsa_cmp_kernel(q_ref, kc_ref, vaug_ref, gt_ref, oc_ref, selt_ref, flag_ref, q_scr, m_ref, acc_ref, s_scr,
                    *, n_cmp):
    i = pl.program_id(2)
    t = NSA_TILE
    tq = CMP_TQ
    hpg = NSA_HPG
    _stack_heads(q_ref, q_scr)
    m_ref[...] = jnp.full_like(m_ref, NEG)
    acc_ref[...] = jnp.zeros_like(acc_ref)

    crow = lax.broadcasted_iota(jnp.int32, (t, tq), 0)
    qpos = i * tq + lax.broadcasted_iota(jnp.int32, (t, tq), 1)

    def keepf(j):
        c = j * t + crow
        return jnp.where(CMP_STRIDE * c + (CMP_LEN - 1) <= qpos, jnp.where(c < n_cmp, 1.0, 0.0), 0.0)

    count = jnp.minimum((i * tq + tq - CMP_LEN) // (CMP_STRIDE * t) + 1, kc_ref.shape[3] // t)
    _skewed_flash(count, lambda n: n, keepf, lambda j: kc_ref[0, 0, 0, pl.ds(pl.multiple_of(j * t, t), t), :],
                  lambda j: vaug_ref[0, 0, j], q_scr, s_scr, m_ref, acc_ref, 0)

    some = jnp.where(qpos[:1] >= CMP_LEN - 1, 1.0, 0.0)
    inv = jnp.concatenate([some] * hpg, axis=1) / acc_ref[0, NSA_DK + NS_PAD:NSA_DK + NS_PAD + 1]
    gates = jax.nn.sigmoid(jnp.concatenate([gt_ref[c] for c in range(tq // t)], axis=1))
    imp = jnp.zeros((NS_PAD, tq), F32)
    for pair in range(hpg // 2):
        halves = []
        for h in (2 * pair, 2 * pair + 1):
            hs = slice(h * tq, (h + 1) * tq)
            halves.append(gates[3 * h:3 * h + 1, :] * (acc_ref[0, :NSA_DK, hs] * inv[:, hs]))
            imp = imp + acc_ref[0, NSA_DK:NSA_DK + NS_PAD, hs] * inv[:, hs]
        oc_ref[:, pair * LANES:(pair + 1) * LANES] = jnp.concatenate(halves, axis=0).T.astype(BF16)

    blk = lax.broadcasted_iota(jnp.int32, (NS_PAD, tq), 0)
    blk_t = (i * tq + lax.broadcasted_iota(jnp.int32, (NS_PAD, tq), 1)) // SLC_LEN
    forced = jnp.where(blk == 0, 1.0, 0.0) + jnp.where(blk == blk_t, 1.0, 0.0) + jnp.where(blk == blk_t - 1, 1.0, 0.0)
    imp = jnp.where(forced > 0.5, FORCE_SCORE, jnp.where(blk <= blk_t, imp, -1.0))
    blkf = blk.astype(F32)
    sel = jnp.zeros((NS_PAD, tq), F32)
    for _ in range(SLC_TOPK):
        best = jnp.max(imp, axis=0, keepdims=True)
        pick = jnp.min(jnp.where(imp == best, blkf, float(NS_PAD)), axis=0, keepdims=True)
        hit = blkf == pick
        sel = jnp.where(hit, 1.0, sel)
        imp = jnp.where(hit, TAKEN, imp)
    selt_ref[0, 0] = sel
    for c in range(tq // t):
        flag_ref[c] = jnp.max(sel[:, c * t:(c + 1) * t].T, axis=0, keepdims=True).astype(jnp.int32)


def _nsa_cmp(q, kc, vaug, gt, batch, seq, n_cmp):
    t = NSA_TILE
    tq = CMP_TQ
    nq = seq // tq
    n = batch * seq
    ncp = kc.shape[3]
    sub = tq // t
    rows = lambda c: pl.BlockSpec((tq, c), lambda b, g, i: (b * nq + i, g))
    return pl.pallas_call(
        functools.partial(_nsa_cmp_kernel, n_cmp=n_cmp),
        grid=(batch, NSA_GROUPS, nq),
        in_specs=[pl.BlockSpec((sub, NSA_HPG * LANES, t), lambda b, g, i: (b * nq + i, g, 0)),
                  pl.BlockSpec((1, 1, 1, ncp, LANES), lambda b, g, i: (0, b, g, 0, 0)),
                  pl.BlockSpec((1, 1) + vaug.shape[2:], lambda b, g, i: (b, g, 0, 0, 0)),
                  pl.BlockSpec((sub, N_GATES, t), lambda b, g, i: (b * nq + i, g, 0))],
        out_specs=[rows(NSA_HPG * NSA_DK),
                   pl.BlockSpec((1, 1, NS_PAD, tq), lambda b, g, i: (b, g, 0, i)),
                   pl.BlockSpec((sub, 1, NS_PAD), lambda b, g, i: ((b * NSA_GROUPS + g) * nq + i, 0, 0))],
        out_shape=[jax.ShapeDtypeStruct((n, NSA_HEADS * NSA_DK), BF16),
                   jax.ShapeDtypeStruct((batch, NSA_GROUPS, NS_PAD, seq), F32),
                   jax.ShapeDtypeStruct((batch * NSA_GROUPS * seq // t, 1, NS_PAD), jnp.int32)],
        scratch_shapes=[pltpu.VMEM((LANES, NSA_HPG * tq), BF16), pltpu.VMEM((1, 1, NSA_HPG * tq), F32),
                        pltpu.VMEM((1, NSA_DK + NS_PAD + ONES_ROWS, NSA_HPG * tq), F32),
                        pltpu.VMEM((2, 2 * t, NSA_HPG * tq), F32)],
        compiler_params=_params(3),
        name="nsa_cmp",
    )(q, kc, vaug, gt)


ID_BITS = 8
IDS_PER_WORD = 32 // ID_BITS


def _nsa_attn_kernel(counts_ref, words_ref, q_ref, selt_ref, oc_ref, gt_ref, ks_ref, kw_ref, vst_ref, vwt_ref, o_ref,
                     q_scr, m_ref, acc_ref, s_scr, *, nq, words_per_step):
    b = pl.program_id(0)
    g = pl.program_id(1)
    i = pl.program_id(2)
    t = NSA_TILE
    tq = ATT_TQ
    sub = tq // t
    hpg = NSA_HPG
    _stack_heads(q_ref, q_scr)
    m_ref[...] = jnp.full_like(m_ref, NEG)
    acc_ref[...] = jnp.zeros_like(acc_ref)

    krow = lax.broadcasted_iota(jnp.int32, (t, tq), 0)
    qpos = i * tq + lax.broadcasted_iota(jnp.int32, (t, tq), 1)
    blocks_per_tile = t // SLC_LEN
    slc, win = 0, 1
    step = (b * NSA_GROUPS + g) * nq + i

    def run_branch(branch, count, tile_at, keepf_fn, k_ref, vt_ref):
        _skewed_flash(count, tile_at, keepf_fn, lambda j: k_ref[pl.ds(pl.multiple_of(j * t, t), t), :],
                      lambda j: vt_ref[0, j], q_scr, s_scr, m_ref, acc_ref, branch)

    def slc_tile(n):
        word = words_ref[step * words_per_step + n // IDS_PER_WORD]
        return lax.shift_right_logical(word, ID_BITS * (n % IDS_PER_WORD)) & (2 ** ID_BITS - 1)

    def slc_keepf(j):
        dist = qpos - (j * t + krow)
        sel_rows = [selt_ref[0, 0, pl.ds(blocks_per_tile * j + r, 1), :] for r in range(blocks_per_tile)]
        picked = sel_rows[-1]
        for r in range(blocks_per_tile - 2, -1, -1):
            picked = jnp.where(krow < (r + 1) * SLC_LEN, sel_rows[r], picked)
        return jnp.where(dist >= 0, picked, 0.0)

    def win_keepf(j):
        dist = qpos - (j * t + krow)
        return jnp.where(dist >= 0, jnp.where(dist < WINDOW, 1.0, 0.0), 0.0)

    run_branch(slc, counts_ref[step], slc_tile, slc_keepf, ks_ref, vst_ref)
    first = jnp.maximum(i * sub - WINDOW // t, 0)
    run_branch(win, (i + 1) * sub - first, lambda n: first + n, win_keepf, kw_ref, vwt_ref)

    gates = jax.nn.sigmoid(jnp.concatenate([gt_ref[c] for c in range(sub)], axis=1))
    inv_s = 1.0 / acc_ref[slc, NSA_DK:NSA_DK + 1]
    inv_w = 1.0 / acc_ref[win, NSA_DK:NSA_DK + 1]
    for pair in range(hpg // 2):
        halves = []
        for h in (2 * pair, 2 * pair + 1):
            hs = slice(h * tq, (h + 1) * tq)
            halves.append(gates[3 * h + 1:3 * h + 2, :] * (acc_ref[slc, :NSA_DK, hs] * inv_s[:, hs])
                          + gates[3 * h + 2:3 * h + 3, :] * (acc_ref[win, :NSA_DK, hs] * inv_w[:, hs]))
        ps = slice(pair * LANES, (pair + 1) * LANES)
        o_ref[:, ps] = (oc_ref[:, ps].astype(F32) + jnp.concatenate(halves, axis=0).T).astype(BF16)


def _nsa_attn(counts, words, q, selt, oc, gt, ksw, vt, batch, seq):
    t = NSA_TILE
    tq = ATT_TQ
    sub = tq // t
    nq = seq // tq
    nk = seq // t
    n = batch * seq
    words_per_step = words.shape[0] // (batch * NSA_GROUPS * nq)
    rows = lambda c: pl.BlockSpec((tq, c), lambda b, g, i, *_: (b * nq + i, g))
    tiles = lambda r: pl.BlockSpec((sub, r, t), lambda b, g, i, *_: (b * nq + i, g, 0))
    key = lambda which: pl.BlockSpec((seq, LANES), lambda b, g, i, *_: (b, which * NSA_GROUPS + g))
    vt4 = vt.reshape(batch, nk, _VT_OUT, t)
    vrows = NSA_DK + ONES_ROWS
    val = lambda which: pl.BlockSpec((1, nk, vrows, t), lambda b, g, i, *_: (b, 0, which * NSA_GROUPS + g, 0))
    cols = NSA_HPG * tq
    grid_spec = pltpu.PrefetchScalarGridSpec(
        num_scalar_prefetch=2,
        grid=(batch, NSA_GROUPS, nq),
        in_specs=[tiles(NSA_HPG * LANES),
                  pl.BlockSpec((1, 1, NS_PAD, tq), lambda b, g, i, *_: (b, g, 0, i)),
                  rows(NSA_HPG * NSA_DK), tiles(N_GATES), key(0), key(1), val(0), val(1)],
        out_specs=rows(NSA_HPG * NSA_DK),
        scratch_shapes=[pltpu.VMEM((LANES, cols), BF16), pltpu.VMEM((2, 1, cols), F32),
                        pltpu.VMEM((2, vrows, cols), F32), pltpu.VMEM((2, 2 * t, cols), F32)],
    )
    return pl.pallas_call(
        functools.partial(_nsa_attn_kernel, nq=nq, words_per_step=words_per_step),
        grid_spec=grid_spec,
        out_shape=jax.ShapeDtypeStruct((n, NSA_HEADS * NSA_DK), BF16),
        compiler_params=_params(3),
        name="nsa_attn",
    )(counts, words, q, selt, oc, gt, ksw, ksw, vt4, vt4)


def _outproj_kernel(x_ref, om_ref, on_ref, wm_ref, wn_ref, o_ref):
    o_ref[...] = x_ref[...] + _dot(om_ref[...], wm_ref[...]) + _dot(on_ref[...], wn_ref[...])


def _out_proj(x2, o_mla, o_nsa, wm, wn, tm=512):
    n, d = x2.shape
    row = lambda c: pl.BlockSpec((tm, c), lambda i: (i, 0))
    return pl.pallas_call(
        _outproj_kernel,
        grid=(n // tm,),
        in_specs=[row(d), row(o_mla.shape[1]), row(o_nsa.shape[1]), _resident(wm.shape), _resident(wn.shape)],
        out_specs=row(d),
        out_shape=jax.ShapeDtypeStruct((n, d), F32),
        compiler_params=_params(1),
        name="out_proj",
    )(x2, o_mla, o_nsa, wm, wn)


def _ffn_kernel(x_ref, g_ref, wg_ref, wu_ref, wd_ref, gf_ref, o_ref, h_scr, acc_scr):
    f = pl.program_id(1)

    @pl.when(f == 0)
    def _():
        h_scr[...] = _rmsnorm(x_ref[...], g_ref[...]).astype(BF16)
        acc_scr[...] = jnp.zeros_like(acc_scr)

    h = h_scr[...]
    gate = _dot(h, wg_ref[...])
    act = (gate * jax.nn.sigmoid(gate)) * _dot(h, wu_ref[...])
    acc_scr[...] += _dot(act.astype(BF16), wd_ref[...])

    @pl.when(f == pl.num_programs(1) - 1)
    def _():
        o_ref[...] = _rmsnorm(x_ref[...] + acc_scr[...], gf_ref[...])


def _ffn(x1, g, wg, wu, wd, gf, tm=512, tf=512):
    n, d = x1.shape
    dff = wg.shape[1]
    return pl.pallas_call(
        _ffn_kernel,
        grid=(n // tm, dff // tf),
        in_specs=[pl.BlockSpec((tm, d), lambda i, f: (i, 0)), _resident((1, d)),
                  pl.BlockSpec((d, tf), lambda i, f: (0, f)), pl.BlockSpec((d, tf), lambda i, f: (0, f)),
                  pl.BlockSpec((tf, d), lambda i, f: (f, 0)), _resident((1, d))],
        out_specs=pl.BlockSpec((tm, d), lambda i, f: (i, 0)),
        out_shape=jax.ShapeDtypeStruct((n, d), F32),
        scratch_shapes=[pltpu.VMEM((tm, d), BF16), pltpu.VMEM((tm, d), F32)],
        compiler_params=_params(2),
        name="ffn",
    )(x1, g, wg, wu, wd, gf)


def _pad_cols(w, width):
    return jnp.pad(w, ((0, 0), (0, width - w.shape[1])))


def _rot_cols(w):
    half = w.shape[1] // 2
    return jnp.concatenate([-w[:, half:], w[:, :half]], axis=1)


def _fused_in_weights(w_in):
    sizes = (MLA_Q_LORA, MLA_KV_LORA, MLA_ROPE, NSA_HEADS * NSA_DK) + (NSA_GROUPS * NSA_DK,) * 6 + (3 * NSA_HEADS,)
    offs = np.cumsum(sizes)[:-1].tolist()
    cq, ckv, kr, q, kc, vc, ks, vs, kw, vw, gate = jnp.split(w_in, offs, axis=1)
    d = w_in.shape[0]
    q_scaled = q * (NSA_DK ** -0.5 * LOG2E)
    per_group = lambda w, c: _pad_cols(w.reshape(d * NSA_GROUPS, c), LANES).reshape(d, NSA_GROUPS * LANES)
    kr_rot = _rot_cols(kr)
    cols = [cq, ckv, kr, kr, kr_rot, kr_rot, kc, vc, per_group(ks, NSA_DK), per_group(kw, NSA_DK)]
    w = jnp.concatenate(cols, axis=1).astype(BF16)
    assert w.shape[1] == _IN_COLS["ksw"][1]
    rows_t = jnp.concatenate([q_scaled, vs, vw, gate], axis=1).T
    pad = -rows_t.shape[0] % 16
    return w, jnp.pad(rows_t, ((0, pad), (0, 0))).astype(BF16)


def _slope_features(slopes2):
    s1 = slopes2.astype(BF16).astype(F32)
    s2 = (slopes2 - s1).astype(BF16).astype(F32)
    s3 = (slopes2 - s1 - s2).astype(BF16).astype(F32)
    pieces = jnp.stack([s1, s2, s3, s1, s2, s3], axis=1)
    column = jnp.pad(pieces, ((0, 0), (0, LANES - NSA_DK - 6))).reshape(-1, 1)
    return jnp.broadcast_to(column, (column.shape[0], NSA_TILE)).astype(BF16)


def _position_features(pos):
    hi = (POS_SPLIT * (pos // POS_SPLIT)).astype(F32)
    lo = (pos % POS_SPLIT).astype(F32)
    return jnp.pad(jnp.stack([hi, hi, hi, lo, lo, lo], axis=1), ((0, 0), (NSA_DK, LANES - NSA_DK - 6)))


def _fused_uq_weight(w_uq):
    d = w_uq.shape[0]
    w = w_uq.reshape(d, MLA_HEADS, MLA_NOPE + MLA_ROPE)
    nope = w[:, :, :MLA_NOPE].reshape(d, MLA_HEADS * MLA_NOPE)
    rope = w[:, :, MLA_NOPE:]
    rope_rot = jnp.concatenate([-rope[:, :, MLA_ROPE // 2:], rope[:, :, :MLA_ROPE // 2]], axis=2)
    flat = lambda r: r.reshape(d, MLA_HEADS * MLA_ROPE)
    return jnp.concatenate([nope, flat(rope), flat(rope_rot)], axis=1).astype(BF16)


def _rope_tables(seq):
    inv = ROPE_THETA ** (-jnp.arange(0, MLA_ROPE, 2, dtype=F32) / MLA_ROPE)
    ang = jnp.arange(seq, dtype=F32)[:, None] * inv[None, :]
    reps = 2 * LANES // MLA_ROPE
    return jnp.tile(jnp.cos(ang), (1, reps)), jnp.tile(jnp.sin(ang), (1, reps))


def _overlap_matrix(seq, n_cmp, ncp):
    cmp_start = CMP_STRIDE * np.arange(n_cmp)
    slc_start = SLC_LEN * np.arange(seq // SLC_LEN)
    ov = np.clip(np.minimum(cmp_start[:, None] + CMP_LEN, slc_start[None, :] + SLC_LEN)
                 - np.maximum(cmp_start[:, None], slc_start[None, :]), 0, None).astype(np.float32) / CMP_STRIDE
    out = np.zeros((NS_PAD, ncp), np.float32)
    out[:ov.shape[1], :n_cmp] = ov.T
    return jnp.asarray(out, BF16)


def _active_tiles(flags, seq):
    t_blocks = NSA_TILE // SLC_LEN
    sub = ATT_TQ // NSA_TILE
    nq = seq // ATT_TQ
    steps = flags.shape[0] // sub
    tiles = flags.reshape(steps, sub, NS_PAD // t_blocks, t_blocks).max(axis=(1, 3))
    tile_id = jnp.arange(tiles.shape[1], dtype=jnp.int32)[None, :]
    last_tile = ((jnp.arange(steps, dtype=jnp.int32) % nq) * sub + (sub - 1))[:, None]
    active = (tiles > 0) & (tile_id <= last_tile)
    order = jnp.argsort(jnp.where(active, tile_id, tile_id + tiles.shape[1]), axis=-1).astype(jnp.int32)
    shifts = ID_BITS * jnp.arange(IDS_PER_WORD, dtype=jnp.int32)
    words = (order.reshape(steps, -1, IDS_PER_WORD) << shifts).sum(axis=-1).astype(jnp.int32)
    return active.sum(axis=-1).astype(jnp.int32), words.reshape(-1)


def kernel(x, attn_norm_g, w_in, mla_q_norm_g, mla_kv_norm_g, w_uq, w_uk, w_uv, cmp_pos_k, cmp_pos_v, w_cmp_k1,
           w_cmp_k2, w_cmp_v1, w_cmp_v2, w_o, ffn_norm_g, w_gate, w_up, w_down, final_norm_g):
    batch, seq, d = x.shape
    n = batch * seq
    assert w_in.shape[0] == 1, "the final RMSNorm is fused into the FFN kernel of a single layer"
    assert seq % (CMP_STRIDE * NSA_TILE) == 0 and seq % CMP_TQ == 0 and seq // SLC_LEN <= NS_PAD
    assert seq // POS_SPLIT <= BF16_EXACT_INT and POS_SPLIT <= BF16_EXACT_INT and seq // NSA_TILE <= 2 ** ID_BITS
    n_cmp = (seq - CMP_LEN) // CMP_STRIDE + 1
    ncp = seq // CMP_STRIDE
    cos_t, sin_t = _rope_tables(seq)
    ov_t = _overlap_matrix(seq, n_cmp, ncp)
    slopes2 = 2.0 ** (-8.0 * jnp.arange(1, NSA_HEADS + 1, dtype=F32) / NSA_HEADS) * LOG2E
    qfeat = _slope_features(slopes2)
    kfeat = _position_features(jnp.arange(seq, dtype=jnp.int32))
    x2 = x.reshape(n, d)

    w_fused, w_fused_t = _fused_in_weights(w_in[0])
    cq, ckv, kr, ksw, kvc, q, vt, gt = _in_proj(
        x2, attn_norm_g[0][None], w_fused, w_fused_t, qfeat, cos_t, sin_t, kfeat, seq)

    qn, qr, kn, vt_mla = _mla_up(cq, ckv, mla_q_norm_g[0][None], mla_kv_norm_g[0][None], _fused_uq_weight(w_uq[0]),
                                 w_uk[0].astype(BF16), w_uv[0].T.astype(BF16), cos_t, sin_t, seq)
    o_mla = _mla_attn(qn, qr, kn, kr, vt_mla, batch, seq)

    pos = jnp.stack([cmp_pos_k[0], cmp_pos_v[0]])
    w1 = jnp.stack([w_cmp_k1[0], w_cmp_v1[0]]).astype(BF16)
    w2 = jnp.stack([_pad_cols(w_cmp_k2[0], LANES), _pad_cols(w_cmp_v2[0], LANES)]).astype(BF16)
    w2t = jnp.stack([w_cmp_k2[0].T, w_cmp_v2[0].T]).astype(BF16)
    cmp_end = CMP_STRIDE * jnp.arange(ncp, dtype=jnp.int32) + (CMP_LEN - 1)
    cmp_rows, cmp_t = _compress(kvc, pos, w1, w2, w2t, _position_features(cmp_end), batch, seq)
    ones = jnp.zeros((ONES_ROWS, ncp), BF16).at[0].set(1.0)
    const_rows = jnp.broadcast_to(jnp.concatenate([ov_t, ones]), (batch, NSA_GROUPS, NS_PAD + ONES_ROWS, ncp))
    vaug = jnp.concatenate([cmp_t[1], const_rows], axis=2)
    vaug = vaug.reshape(batch, NSA_GROUPS, vaug.shape[2], ncp // NSA_TILE, NSA_TILE).transpose(0, 1, 3, 2, 4)

    oc, selt, flags = _nsa_cmp(q, cmp_rows, vaug, gt, batch, seq, n_cmp)
    counts, words = _active_tiles(flags.reshape(flags.shape[0], NS_PAD), seq)
    o_nsa = _nsa_attn(counts, words, q, selt, oc, gt, ksw, vt, batch, seq)

    split = MLA_HEADS * MLA_V
    x1 = _out_proj(x2, o_mla, o_nsa, w_o[0][:split].astype(BF16), w_o[0][split:].astype(BF16))
    out = _ffn(x1, ffn_norm_g[0][None], w_gate[0].astype(BF16), w_up[0].astype(BF16), w_down[0].astype(BF16),
               final_norm_g[None])
    return out.reshape(batch, seq, d)
```

```python
import functools
import math

import numpy as np
import jax
import jax.numpy as jnp
from jax import lax
from jax.experimental import pallas as pl
from jax.experimental.pallas import tpu as pltpu

F32 = jnp.float32
BF16 = jnp.bfloat16

EPS = 1e-6
NEG = -1e30
LOG2E = math.log2(math.e)
LANES = 128

MLA_HEADS = 8
MLA_Q_LORA = 512
MLA_KV_LORA = 256
MLA_NOPE = 128
MLA_ROPE = 64
MLA_V = 128
ROPE_THETA = 10000.0
MLA_TK = 1024

NSA_HEADS = 16
NSA_GROUPS = 2
NSA_HPG = NSA_HEADS // NSA_GROUPS
NSA_DK = 64
CMP_LEN = 32
CMP_STRIDE = 16
CMP_HIDDEN = 128
SLC_LEN = 64
SLC_TOPK = 16
WINDOW = 512
FORCE_SCORE = 1e4
TAKEN = -3e38
NSA_TILE = 128
NS_PAD = 128
CMP_TQ = 1024
ATT_TQ = 256
N_GATES = 3 * NSA_HPG
POS_SPLIT = 64
BF16_EXACT_INT = 256
ONES_ROWS = 16

VMEM_LIMIT = 56 * 1024 * 1024


def _params(n_axes):
    return pltpu.CompilerParams(dimension_semantics=("arbitrary",) * n_axes, vmem_limit_bytes=VMEM_LIMIT)


def _resident(shape):
    zeros = (0,) * len(shape)
    return pl.BlockSpec(shape, lambda *_: zeros, pipeline_mode=pl.Buffered(1))


def _rmsnorm(x, g):
    return x * lax.rsqrt(jnp.mean(x * x, axis=-1, keepdims=True) + EPS) * g


def _dot(a, b):
    return jnp.dot(a, b, preferred_element_type=F32)


def _dot_nt(a, b):
    return lax.dot_general(a, b, (((1,), (1,)), ((), ())), preferred_element_type=F32)


def _ones_rows(width):
    return jnp.where(lax.broadcasted_iota(jnp.int32, (ONES_ROWS, width), 0) == 0, 1.0, 0.0).astype(BF16)


def _flash_weights(s, m_ref, idx):
    m_prev = m_ref[idx]
    m_new = jnp.maximum(m_prev, jnp.max(s, axis=0, keepdims=True))
    m_ref[idx] = m_new
    return jnp.exp2(m_prev - m_new), jnp.exp2(s - m_new).astype(BF16)


def _stack_heads(qt_ref, qt_scr):
    tiles, rows, t = qt_ref.shape
    for h in range(rows // LANES):
        for c in range(tiles):
            qt_scr[:, (h * tiles + c) * t:(h * tiles + c + 1) * t] = qt_ref[c, h * LANES:(h + 1) * LANES, :]


def _flash_update(s, v_t, m_ref, acc_ref, idx):
    alpha, p = _flash_weights(s, m_ref, idx)
    acc_ref[idx] = alpha * acc_ref[idx] + _dot(v_t, p)


def _skewed_flash(count, tile_at, keepf_fn, k_tile, vt_tile, qt_scr, s_scr, m_ref, acc_ref, idx):
    pairs = (count + 1) // 2

    def pair(u):
        return [tile_at(jnp.minimum(2 * u + r, count - 1)) for r in range(2)]

    def scores(u, slot):
        j0, j1 = pair(jnp.minimum(u, pairs - 1))
        s_scr[slot] = _dot(jnp.concatenate([k_tile(j0), k_tile(j1)], axis=0), qt_scr[...])

    def consume(u, slot):
        j0, j1 = pair(u)
        second = jnp.where(2 * u + 1 < count, 1.0, 0.0)
        bias = jnp.concatenate([jnp.where(keepf_fn(j0) > 0.5, 0.0, NEG),
                                jnp.where(keepf_fn(j1) * second > 0.5, 0.0, NEG)], axis=0)
        s = s_scr[slot]
        s = s + jnp.concatenate([bias] * (s.shape[1] // bias.shape[1]), axis=1)
        _flash_update(s, jnp.concatenate([vt_tile(j0), vt_tile(j1)], axis=1), m_ref, acc_ref, idx)

    def body(w, carry):
        u = 2 * w
        scores(u + 1, 1)
        consume(u, 0)
        scores(u + 2, 0)
        consume(u + 1, 1)
        return carry

    scores(0, 0)
    lax.fori_loop(0, pairs // 2, body, 0)

    @pl.when(pairs % 2 == 1)
    def _():
        consume(pairs - 1, 0)


_IN_COLS = dict(cq=(0, 512), ckv=(512, 768), kra=(768, 896), krb=(896, 1024), kvc=(1024, 1280), ksw=(1280, 1792))
_QT_ROWS = NSA_HEADS * LANES
_Q_ROWS = NSA_HEADS * NSA_DK
_VT_ROWS = 2 * NSA_GROUPS * NSA_DK
_VT_OUT = _VT_ROWS // NSA_DK * (NSA_DK + ONES_ROWS)


def _inproj_kernel(x_ref, g_ref, w_ref, wt_ref, qfeat_ref, cos_ref, sin_ref, kfeat_ref, cq_ref, ckv_ref, kr_ref,
                   ksw_ref, kvc_ref, qt_ref, vt_ref, gt_ref):
    h = _rmsnorm(x_ref[...], g_ref[...]).astype(BF16)

    def mm(name):
        lo, hi = _IN_COLS[name]
        return _dot(h, w_ref[:, lo:hi])

    cq_ref[...] = mm("cq")
    ckv_ref[...] = mm("ckv")
    kr_ref[...] = (mm("kra") * cos_ref[...] + mm("krb") * sin_ref[...]).astype(BF16)
    kvc = mm("kvc")
    for k in range(kvc_ref.shape[0]):
        kvc_ref[k] = kvc[:, k * NSA_DK:(k + 1) * NSA_DK]
    kfeat = kfeat_ref[...]
    ksw_ref[...] = (mm("ksw") + jnp.concatenate([kfeat] * (2 * NSA_GROUPS), axis=1)).astype(BF16)
    t = _dot_nt(wt_ref[...], h)
    ones = _ones_rows(NSA_TILE)
    qfeat = qfeat_ref[...]
    for c in range(vt_ref.shape[0]):
        cs = slice(c * NSA_TILE, (c + 1) * NSA_TILE)
        pieces = []
        for h in range(NSA_HEADS):
            pieces += [t[h * NSA_DK:(h + 1) * NSA_DK, cs].astype(BF16), qfeat[h * NSA_DK:(h + 1) * NSA_DK]]
        qt_ref[c] = jnp.concatenate(pieces, axis=0)
        pieces = []
        for k in range(_VT_ROWS // NSA_DK):
            rows = slice(_Q_ROWS + k * NSA_DK, _Q_ROWS + (k + 1) * NSA_DK)
            pieces += [t[rows, cs].astype(BF16), ones]
        vt_ref[c] = jnp.concatenate(pieces, axis=0)
        gt_ref[c] = t[_Q_ROWS + _VT_ROWS:_Q_ROWS + _VT_ROWS + NSA_GROUPS * N_GATES, cs]


def _in_proj(x2, g, w, wt, qfeat, cos_t, sin_t, kfeat, seq, tm=256):
    n, d = x2.shape
    nt = seq // tm
    widths = {k: hi - lo for k, (lo, hi) in _IN_COLS.items()}
    row = lambda c: pl.BlockSpec((tm, c), lambda i: (i, 0))
    tab = pl.BlockSpec((tm, LANES), lambda i: (i % nt, 0))
    outs = [("cq", F32), ("ckv", F32), ("kra", BF16), ("ksw", BF16)]
    n_kvc = widths["kvc"] // NSA_DK
    tiles = tm // NSA_TILE
    tspec = lambda rows: pl.BlockSpec((tiles, rows, NSA_TILE), lambda i: (i, 0, 0))
    return pl.pallas_call(
        _inproj_kernel,
        grid=(n // tm,),
        in_specs=[row(d), _resident((1, d)), _resident(w.shape), _resident(wt.shape), _resident(qfeat.shape),
                  tab, tab, tab],
        out_specs=[row(widths[k]) for k, _ in outs] + [
            pl.BlockSpec((n_kvc, tm, NSA_DK), lambda i: (0, i, 0)),
            tspec(_QT_ROWS), tspec(_VT_OUT), tspec(NSA_GROUPS * N_GATES)],
        out_shape=[jax.ShapeDtypeStruct((n, widths[k]), dt) for k, dt in outs] + [
            jax.ShapeDtypeStruct((n_kvc, n, NSA_DK), F32),
            jax.ShapeDtypeStruct((n // NSA_TILE, _QT_ROWS, NSA_TILE), BF16),
            jax.ShapeDtypeStruct((n // NSA_TILE, _VT_OUT, NSA_TILE), BF16),
            jax.ShapeDtypeStruct((n // NSA_TILE, NSA_GROUPS * N_GATES, NSA_TILE), F32)],
        compiler_params=_params(1),
        name="in_proj",
    )(x2, g, w, wt, qfeat, cos_t, sin_t, kfeat)


def _mlaup_kernel(cq_ref, ckv_ref, gq_ref, gkv_ref, wq_ref, wk_ref, wvt_ref, cos_ref, sin_ref, qn_ref, qr_ref,
                  kn_ref, vt_ref, *, scale):
    cqn = _rmsnorm(cq_ref[...], gq_ref[...]).astype(BF16)
    ckvn = _rmsnorm(ckv_ref[...], gkv_ref[...]).astype(BF16)
    hn = MLA_HEADS * MLA_NOPE
    hr = MLA_HEADS * MLA_ROPE
    qn_ref[...] = (_dot(cqn, wq_ref[:, :hn]) * scale).astype(BF16)
    a = _dot(cqn, wq_ref[:, hn:hn + hr])
    b = _dot(cqn, wq_ref[:, hn + hr:])
    cos = cos_ref[...]
    sin = sin_ref[...]
    for p in range(hr // LANES):
        sl = slice(p * LANES, (p + 1) * LANES)
        qr_ref[:, sl] = ((a[:, sl] * cos + b[:, sl] * sin) * scale).astype(BF16)
    kn_ref[...] = _dot(ckvn, wk_ref[...]).astype(BF16)
    v_t = _dot_nt(wvt_ref[...], ckvn).astype(BF16)
    ones = _ones_rows(MLA_TK)
    for c in range(vt_ref.shape[0]):
        pieces = []
        for h in range(MLA_HEADS):
            pieces += [v_t[h * MLA_V:(h + 1) * MLA_V, c * MLA_TK:(c + 1) * MLA_TK], ones]
        vt_ref[c] = jnp.concatenate(pieces, axis=0)


def _mla_up(cq, ckv, gq, gkv, wq, wk, wvt, cos_t, sin_t, seq, tm=MLA_TK):
    n = cq.shape[0]
    nt = seq // tm
    hn = MLA_HEADS * MLA_NOPE
    hr = MLA_HEADS * MLA_ROPE
    hv = MLA_HEADS * (MLA_V + ONES_ROWS)
    row = lambda c: pl.BlockSpec((tm, c), lambda i: (i, 0))
    tab = pl.BlockSpec((tm, LANES), lambda i: (i % nt, 0))
    scale = (MLA_NOPE + MLA_ROPE) ** -0.5 * LOG2E
    return pl.pallas_call(
        functools.partial(_mlaup_kernel, scale=scale),
        grid=(n // tm,),
        in_specs=[row(MLA_Q_LORA), row(MLA_KV_LORA), _resident(gq.shape), _resident(gkv.shape),
                  _resident(wq.shape), _resident(wk.shape), _resident(wvt.shape), tab, tab],
        out_specs=[row(hn), row(hr), row(hn), pl.BlockSpec((tm // MLA_TK, hv, MLA_TK), lambda i: (i, 0, 0))],
        out_shape=[jax.ShapeDtypeStruct((n, hn), BF16), jax.ShapeDtypeStruct((n, hr), BF16),
                   jax.ShapeDtypeStruct((n, hn), BF16), jax.ShapeDtypeStruct((n // MLA_TK, hv, MLA_TK), BF16)],
        compiler_params=_params(1),
        name="mla_up",
    )(cq, ckv, gq, gkv, wq, wk, wvt, cos_t, sin_t)


def _mla_attn_kernel(qn_ref, qr_ref, kn_ref, kr_ref, vt_ref, o_ref, m_ref, acc_ref, s_scr, qt_scr, *, tq):
    i = pl.program_id(2)
    tk = MLA_TK
    lane = lax.broadcasted_iota(jnp.int32, (tq, LANES), 1)
    krow = lax.broadcasted_iota(jnp.int32, (tk, tq), 0)
    qcol = lax.broadcasted_iota(jnp.int32, (tk, tq), 1)
    qr = qr_ref[...]
    zero = jnp.zeros_like(qr)
    for hh in range(2):
        q_cat = jnp.concatenate([qn_ref[:, hh * LANES:(hh + 1) * LANES],
                                 jnp.where((lane < MLA_ROPE) == (hh == 0), qr, zero)], axis=1)
        qt_scr[hh] = q_cat.astype(F32).T.astype(BF16)
    m_ref[...] = jnp.full_like(m_ref, NEG)
    acc_ref[...] = jnp.zeros_like(acc_ref)
    vrows = MLA_V + ONES_ROWS

    def scores(j, slot):
        ks = pl.ds(pl.multiple_of(j * tk, tk), tk)
        k_rope = kr_ref[ks, :]
        for hh in range(2):
            hs = slice(hh * LANES, (hh + 1) * LANES)
            s_scr[slot, hh] = _dot(jnp.concatenate([kn_ref[ks, hs], k_rope], axis=1), qt_scr[hh])

    def consume(j, slot, diagonal):
        weights = []
        for hh in range(2):
            s = s_scr[slot, hh]
            if diagonal:
                s = jnp.where(j * tk + krow <= i * tq + qcol, s, NEG)
            weights.append(_flash_weights(s, m_ref, hh))
        for hh in range(2):
            alpha, p = weights[hh]
            acc_ref[hh] = alpha * acc_ref[hh] + _dot(vt_ref[0, j, hh * vrows:(hh + 1) * vrows, :], p)

    def body(u, carry):
        j = 2 * u
        scores(j + 1, 1)
        consume(j, 0, False)
        scores(j + 2, 0)
        consume(j + 1, 1, False)
        return carry

    assert tq == tk, "the last key tile of a query tile must be its only diagonal tile"
    scores(0, 0)
    lax.fori_loop(0, i // 2, body, 0)

    @pl.when(i % 2 == 1)
    def _():
        scores(i, 1)
        consume(i - 1, 0, False)
        consume(i, 1, True)

    @pl.when(i % 2 == 0)
    def _():
        consume(i, 0, True)

    for hh in range(2):
        o_t = acc_ref[hh, :MLA_V] / acc_ref[hh, MLA_V:MLA_V + 1]
        for c in range(tq // LANES):
            o_ref[c * LANES:(c + 1) * LANES, hh * LANES:(hh + 1) * LANES] = (
                o_t[:, c * LANES:(c + 1) * LANES].T.astype(BF16))


def _mla_attn(qn, qr, kn, kr, vt, batch, seq, tq=MLA_TK):
    n = qn.shape[0]
    nq = seq // tq
    nk = seq // MLA_TK
    pairs = MLA_HEADS // 2
    pw = 2 * LANES
    qspec = lambda c: pl.BlockSpec((tq, c), lambda b, p, i: (b * nq + i, p))
    vrows = MLA_V + ONES_ROWS
    vt4 = vt.reshape(batch, nk, MLA_HEADS * vrows, MLA_TK)
    return pl.pallas_call(
        functools.partial(_mla_attn_kernel, tq=tq),
        grid=(batch, pairs, nq),
        in_specs=[qspec(pw), qspec(LANES), pl.BlockSpec((seq, pw), lambda b, p, i: (b, p)),
                  pl.BlockSpec((seq, LANES), lambda b, p, i: (b, 0)),
                  pl.BlockSpec((1, nk, 2 * vrows, MLA_TK), lambda b, p, i: (b, 0, p, 0))],
        out_specs=qspec(pw),
        out_shape=jax.ShapeDtypeStruct((n, MLA_HEADS * MLA_V), BF16),
        scratch_shapes=[pltpu.VMEM((2, 1, tq), F32), pltpu.VMEM((2, vrows, tq), F32),
                        pltpu.VMEM((2, 2, MLA_TK, tq), F32), pltpu.VMEM((2, 2 * LANES, tq), BF16)],
        compiler_params=_params(3),
        name="mla_attn",
    )(qn, qr, kn, kr, vt4)


def _compress_kernel(r_ref, pos_ref, w1_ref, w2_ref, w2t_ref, feat_ref, o_ref, ot_ref):
    nc = o_ref.shape[3]
    a = jnp.zeros((nc, CMP_HIDDEN), F32)
    b = jnp.zeros((nc, CMP_HIDDEN), F32)
    for l in range(CMP_STRIDE):
        x = r_ref[0, pl.ds(l, nc, stride=CMP_STRIDE), :]
        lo = slice(l * NSA_DK, (l + 1) * NSA_DK)
        hi = slice((CMP_STRIDE + l) * NSA_DK, (CMP_STRIDE + l + 1) * NSA_DK)
        a = a + _dot((x + pos_ref[0, l:l + 1, :]).astype(BF16), w1_ref[0, lo, :])
        b = b + _dot((x + pos_ref[0, CMP_STRIDE + l:CMP_STRIDE + l + 1, :]).astype(BF16), w1_ref[0, hi, :])
    hid = a + pltpu.roll(b, nc - 1, 0)
    act = (hid * jax.nn.sigmoid(hid)).astype(BF16)
    o_ref[0, 0, 0] = (_dot(act, w2_ref[0]) + feat_ref[...]).astype(BF16)
    ot_ref[0, 0, 0] = _dot_nt(w2t_ref[0], act).astype(BF16)


def _compress(kvc, pos, w1, w2, w2t, feat, batch, seq):
    groups = NSA_GROUPS
    nc = seq // CMP_STRIDE
    return pl.pallas_call(
        _compress_kernel,
        grid=(2, batch, groups),
        in_specs=[pl.BlockSpec((1, seq, NSA_DK), lambda t, b, g: (t * groups + g, b, 0)),
                  pl.BlockSpec((1, CMP_LEN, NSA_DK), lambda t, b, g: (t, 0, 0)),
                  pl.BlockSpec((1, CMP_LEN * NSA_DK, CMP_HIDDEN), lambda t, b, g: (t, 0, 0)),
                  pl.BlockSpec((1, CMP_HIDDEN, LANES), lambda t, b, g: (t, 0, 0)),
                  pl.BlockSpec((1, NSA_DK, CMP_HIDDEN), lambda t, b, g: (t, 0, 0)),
                  pl.BlockSpec((nc, LANES), lambda t, b, g: (0, 0))],
        out_specs=[pl.BlockSpec((1, 1, 1, nc, LANES), lambda t, b, g: (t, b, g, 0, 0)),
                   pl.BlockSpec((1, 1, 1, NSA_DK, nc), lambda t, b, g: (t, b, g, 0, 0))],
        out_shape=[jax.ShapeDtypeStruct((2, batch, groups, nc, LANES), BF16),
                   jax.ShapeDtypeStruct((2, batch, groups, NSA_DK, nc), BF16)],
        compiler_params=_params(3),
        name="compress",
    )(kvc, pos, w1, w2, w2t, feat)


def _nsa_cmp_kernel(q_ref, kc_ref, vaug_ref, gt_ref, oc_ref, selt_ref, flag_ref, q_scr, m_ref, acc_ref, s_scr,
                    *, n_cmp):
    i = pl.program_id(2)
    t = NSA_TILE
    tq = CMP_TQ
    hpg = NSA_HPG
    _stack_heads(q_ref, q_scr)
    m_ref[...] = jnp.full_like(m_ref, NEG)
    acc_ref[...] = jnp.zeros_like(acc_ref)

    crow = lax.broadcasted_iota(jnp.int32, (t, tq), 0)
    qpos = i * tq + lax.broadcasted_iota(jnp.int32, (t, tq), 1)

    def keepf(j):
        c = j * t + crow
        return jnp.where(CMP_STRIDE * c + (CMP_LEN - 1) <= qpos, jnp.where(c < n_cmp, 1.0, 0.0), 0.0)

    count = jnp.minimum((i * tq + tq - CMP_LEN) // (CMP_STRIDE * t) + 1, kc_ref.shape[3] // t)
    _skewed_flash(count, lambda n: n, keepf, lambda j: kc_ref[0, 0, 0, pl.ds(pl.multiple_of(j * t, t), t), :],
                  lambda j: vaug_ref[0, 0, j], q_scr, s_scr, m_ref, acc_ref, 0)

    some = jnp.where(qpos[:1] >= CMP_LEN - 1, 1.0, 0.0)
    inv = jnp.concatenate([some] * hpg, axis=1) / acc_ref[0, NSA_DK + NS_PAD:NSA_DK + NS_PAD + 1]
    gates = jax.nn.sigmoid(jnp.concatenate([gt_ref[c] for c in range(tq // t)], axis=1))
    imp = jnp.zeros((NS_PAD, tq), F32)
    for pair in range(hpg // 2):
        halves = []
        for h in (2 * pair, 2 * pair + 1):
            hs = slice(h * tq, (h + 1) * tq)
            halves.append(gates[3 * h:3 * h + 1, :] * (acc_ref[0, :NSA_DK, hs] * inv[:, hs]))
            imp = imp + acc_ref[0, NSA_DK:NSA_DK + NS_PAD, hs] * inv[:, hs]
        oc_ref[:, pair * LANES:(pair + 1) * LANES] = jnp.concatenate(halves, axis=0).T.astype(BF16)

    blk = lax.broadcasted_iota(jnp.int32, (NS_PAD, tq), 0)
    blk_t = (i * tq + lax.broadcasted_iota(jnp.int32, (NS_PAD, tq), 1)) // SLC_LEN
    forced = jnp.where(blk == 0, 1.0, 0.0) + jnp.where(blk == blk_t, 1.0, 0.0) + jnp.where(blk == blk_t - 1, 1.0, 0.0)
    imp = jnp.where(forced > 0.5, FORCE_SCORE, jnp.where(blk <= blk_t, imp, -1.0))
    blkf = blk.astype(F32)
    sel = jnp.zeros((NS_PAD, tq), F32)
    for _ in range(SLC_TOPK):
        best = jnp.max(imp, axis=0, keepdims=True)
        pick = jnp.min(jnp.where(imp == best, blkf, float(NS_PAD)), axis=0, keepdims=True)
        hit = blkf == pick
        sel = jnp.where(hit, 1.0, sel)
        imp = jnp.where(hit, TAKEN, imp)
    selt_ref[0, 0] = sel
    for c in range(tq // t):
        flag_ref[c] = jnp.max(sel[:, c * t:(c + 1) * t].T, axis=0, keepdims=True).astype(jnp.int32)


def _nsa_cmp(q, kc, vaug, gt, batch, seq, n_cmp):
    t = NSA_TILE
    tq = CMP_TQ
    nq = seq // tq
    n = batch * seq
    ncp = kc.shape[3]
    sub = tq // t
    rows = lambda c: pl.BlockSpec((tq, c), lambda b, g, i: (b * nq + i, g))
    return pl.pallas_call(
        functools.partial(_nsa_cmp_kernel, n_cmp=n_cmp),
        grid=(batch, NSA_GROUPS, nq),
        in_specs=[pl.BlockSpec((sub, NSA_HPG * LANES, t), lambda b, g, i: (b * nq + i, g, 0)),
                  pl.BlockSpec((1, 1, 1, ncp, LANES), lambda b, g, i: (0, b, g, 0, 0)),
                  pl.BlockSpec((1, 1) + vaug.shape[2:], lambda b, g, i: (b, g, 0, 0, 0)),
                  pl.BlockSpec((sub, N_GATES, t), lambda b, g, i: (b * nq + i, g, 0))],
        out_specs=[rows(NSA_HPG * NSA_DK),
                   pl.BlockSpec((1, 1, NS_PAD, tq), lambda b, g, i: (b, g, 0, i)),
                   pl.BlockSpec((sub, 1, NS_PAD), lambda b, g, i: ((b * NSA_GROUPS + g) * nq + i, 0, 0))],
        out_shape=[jax.ShapeDtypeStruct((n, NSA_HEADS * NSA_DK), BF16),
                   jax.ShapeDtypeStruct((batch, NSA_GROUPS, NS_PAD, seq), F32),
                   jax.ShapeDtypeStruct((batch * NSA_GROUPS * seq // t, 1, NS_PAD), jnp.int32)],
        scratch_shapes=[pltpu.VMEM((LANES, NSA_HPG * tq), BF16), pltpu.VMEM((1, 1, NSA_HPG * tq), F32),
                        pltpu.VMEM((1, NSA_DK + NS_PAD + ONES_ROWS, NSA_HPG * tq), F32),
                        pltpu.VMEM((2, 2 * t, NSA_HPG * tq), F32)],
        compiler_params=_params(3),
        name="nsa_cmp",
    )(q, kc, vaug, gt)


ID_BITS = 8
IDS_PER_WORD = 32 // ID_BITS


def _nsa_attn_kernel(counts_ref, words_ref, q_ref, selt_ref, oc_ref, gt_ref, ks_ref, kw_ref, vst_ref, vwt_ref, o_ref,
                     q_scr, m_ref, acc_ref, s_scr, *, nq, words_per_step):
    b = pl.program_id(0)
    g = pl.program_id(1)
    i = pl.program_id(2)
    t = NSA_TILE
    tq = ATT_TQ
    sub = tq // t
    hpg = NSA_HPG
    _stack_heads(q_ref, q_scr)
    m_ref[...] = jnp.full_like(m_ref, NEG)
    acc_ref[...] = jnp.zeros_like(acc_ref)

    krow = lax.broadcasted_iota(jnp.int32, (t, tq), 0)
    qpos = i * tq + lax.broadcasted_iota(jnp.int32, (t, tq), 1)
    blocks_per_tile = t // SLC_LEN
    slc, win = 0, 1
    step = (b * NSA_GROUPS + g) * nq + i

    def run_branch(branch, count, tile_at, keepf_fn, k_ref, vt_ref):
        _skewed_flash(count, tile_at, keepf_fn, lambda j: k_ref[pl.ds(pl.multiple_of(j * t, t), t), :],
                      lambda j: vt_ref[0, j], q_scr, s_scr, m_ref, acc_ref, branch)

    def slc_tile(n):
        word = words_ref[step * words_per_step + n // IDS_PER_WORD]
        return lax.shift_right_logical(word, ID_BITS * (n % IDS_PER_WORD)) & (2 ** ID_BITS - 1)

    def slc_keepf(j):
        dist = qpos - (j * t + krow)
        sel_rows = [selt_ref[0, 0, pl.ds(blocks_per_tile * j + r, 1), :] for r in range(blocks_per_tile)]
        picked = sel_rows[-1]
        for r in range(blocks_per_tile - 2, -1, -1):
            picked = jnp.where(krow < (r + 1) * SLC_LEN, sel_rows[r], picked)
        return jnp.where(dist >= 0, picked, 0.0)

    def win_keepf(j):
        dist = qpos - (j * t + krow)
        return jnp.where(dist >= 0, jnp.where(dist < WINDOW, 1.0, 0.0), 0.0)

    run_branch(slc, counts_ref[step], slc_tile, slc_keepf, ks_ref, vst_ref)
    first = jnp.maximum(i * sub - WINDOW // t, 0)
    run_branch(win, (i + 1) * sub - first, lambda n: first + n, win_keepf, kw_ref, vwt_ref)

    gates = jax.nn.sigmoid(jnp.concatenate([gt_ref[c] for c in range(sub)], axis=1))
    inv_s = 1.0 / acc_ref[slc, NSA_DK:NSA_DK + 1]
    inv_w = 1.0 / acc_ref[win, NSA_DK:NSA_DK + 1]
    for pair in range(hpg // 2):
        halves = []
        for h in (2 * pair, 2 * pair + 1):
            hs = slice(h * tq, (h + 1) * tq)
            halves.append(gates[3 * h + 1:3 * h + 2, :] * (acc_ref[slc, :NSA_DK, hs] * inv_s[:, hs])
                          + gates[3 * h + 2:3 * h + 3, :] * (acc_ref[win, :NSA_DK, hs] * inv_w[:, hs]))
        ps = slice(pair * LANES, (pair + 1) * LANES)
        o_ref[:, ps] = (oc_ref[:, ps].astype(F32) + jnp.concatenate(halves, axis=0).T).astype(BF16)


def _nsa_attn(counts, words, q, selt, oc, gt, ksw, vt, batch, seq):
    t = NSA_TILE
    tq = ATT_TQ
    sub = tq // t
    nq = seq // tq
    nk = seq // t
    n = batch * seq
    words_per_step = words.shape[0] // (batch * NSA_GROUPS * nq)
    rows = lambda c: pl.BlockSpec((tq, c), lambda b, g, i, *_: (b * nq + i, g))
    tiles = lambda r: pl.BlockSpec((sub, r, t), lambda b, g, i, *_: (b * nq + i, g, 0))
    key = lambda which: pl.BlockSpec((seq, LANES), lambda b, g, i, *_: (b, which * NSA_GROUPS + g))
    vt4 = vt.reshape(batch, nk, _VT_OUT, t)
    vrows = NSA_DK + ONES_ROWS
    val = lambda which: pl.BlockSpec((1, nk, vrows, t), lambda b, g, i, *_: (b, 0, which * NSA_GROUPS + g, 0))
    cols = NSA_HPG * tq
    grid_spec = pltpu.PrefetchScalarGridSpec(
        num_scalar_prefetch=2,
        grid=(batch, NSA_GROUPS, nq),
        in_specs=[tiles(NSA_HPG * LANES),
                  pl.BlockSpec((1, 1, NS_PAD, tq), lambda b, g, i, *_: (b, g, 0, i)),
                  rows(NSA_HPG * NSA_DK), tiles(N_GATES), key(0), key(1), val(0), val(1)],
        out_specs=rows(NSA_HPG * NSA_DK),
        scratch_shapes=[pltpu.VMEM((LANES, cols), BF16), pltpu.VMEM((2, 1, cols), F32),
                        pltpu.VMEM((2, vrows, cols), F32), pltpu.VMEM((2, 2 * t, cols), F32)],
    )
    return pl.pallas_call(
        functools.partial(_nsa_attn_kernel, nq=nq, words_per_step=words_per_step),
        grid_spec=grid_spec,
        out_shape=jax.ShapeDtypeStruct((n, NSA_HEADS * NSA_DK), BF16),
        compiler_params=_params(3),
        name="nsa_attn",
    )(counts, words, q, selt, oc, gt, ksw, ksw, vt4, vt4)


def _outproj_kernel(x_ref, om_ref, on_ref, wm_ref, wn_ref, o_ref):
    o_ref[...] = x_ref[...] + _dot(om_ref[...], wm_ref[...]) + _dot(on_ref[...], wn_ref[...])


def _out_proj(x2, o_mla, o_nsa, wm, wn, tm=512):
    n, d = x2.shape
    row = lambda c: pl.BlockSpec((tm, c), lambda i: (i, 0))
    return pl.pallas_call(
        _outproj_kernel,
        grid=(n // tm,),
        in_specs=[row(d), row(o_mla.shape[1]), row(o_nsa.shape[1]), _resident(wm.shape), _resident(wn.shape)],
        out_specs=row(d),
        out_shape=jax.ShapeDtypeStruct((n, d), F32),
        compiler_params=_params(1),
        name="out_proj",
    )(x2, o_mla, o_nsa, wm, wn)


def _ffn_kernel(x_ref, g_ref, wg_ref, wu_ref, wd_ref, gf_ref, o_ref, h_scr, acc_scr):
    f = pl.program_id(1)

    @pl.when(f == 0)
    def _():
        h_scr[...] = _rmsnorm(x_ref[...], g_ref[...]).astype(BF16)
        acc_scr[...] = jnp.zeros_like(acc_scr)

    h = h_scr[...]
    gate = _dot(h, wg_ref[...])
    act = (gate * jax.nn.sigmoid(gate)) * _dot(h, wu_ref[...])
    acc_scr[...] += _dot(act.astype(BF16), wd_ref[...])

    @pl.when(f == pl.num_programs(1) - 1)
    def _():
        o_ref[...] = _rmsnorm(x_ref[...] + acc_scr[...], gf_ref[...])


def _ffn(x1, g, wg, wu, wd, gf, tm=512, tf=512):
    n, d = x1.shape
    dff = wg.shape[1]
    return pl.pallas_call(
        _ffn_kernel,
        grid=(n // tm, dff // tf),
        in_specs=[pl.BlockSpec((tm, d), lambda i, f: (i, 0)), _resident((1, d)),
                  pl.BlockSpec((d, tf), lambda i, f: (0, f)), pl.BlockSpec((d, tf), lambda i, f: (0, f)),
                  pl.BlockSpec((tf, d), lambda i, f: (f, 0)), _resident((1, d))],
        out_specs=pl.BlockSpec((tm, d), lambda i, f: (i, 0)),
        out_shape=jax.ShapeDtypeStruct((n, d), F32),
        scratch_shapes=[pltpu.VMEM((tm, d), BF16), pltpu.VMEM((tm, d), F32)],
        compiler_params=_params(2),
        name="ffn",
    )(x1, g, wg, wu, wd, gf)


def _pad_cols(w, width):
    return jnp.pad(w, ((0, 0), (0, width - w.shape[1])))


def _rot_cols(w):
    half = w.shape[1] // 2
    return jnp.concatenate([-w[:, half:], w[:, :half]], axis=1)


def _fused_in_weights(w_in):
    sizes = (MLA_Q_LORA, MLA_KV_LORA, MLA_ROPE, NSA_HEADS * NSA_DK) + (NSA_GROUPS * NSA_DK,) * 6 + (3 * NSA_HEADS,)
    offs = np.cumsum(sizes)[:-1].tolist()
    cq, ckv, kr, q, kc, vc, ks, vs, kw, vw, gate = jnp.split(w_in, offs, axis=1)
    d = w_in.shape[0]
    q_scaled = q * (NSA_DK ** -0.5 * LOG2E)
    per_group = lambda w, c: _pad_cols(w.reshape(d * NSA_GROUPS, c), LANES).reshape(d, NSA_GROUPS * LANES)
    kr_rot = _rot_cols(kr)
    cols = [cq, ckv, kr, kr, kr_rot, kr_rot, kc, vc, per_group(ks, NSA_DK), per_group(kw, NSA_DK)]
    w = jnp.concatenate(cols, axis=1).astype(BF16)
    assert w.shape[1] == _IN_COLS["ksw"][1]
    rows_t = jnp.concatenate([q_scaled, vs, vw, gate], axis=1).T
    pad = -rows_t.shape[0] % 16
    return w, jnp.pad(rows_t, ((0, pad), (0, 0))).astype(BF16)


def _slope_features(slopes2):
    s1 = slopes2.astype(BF16).astype(F32)
    s2 = (slopes2 - s1).astype(BF16).astype(F32)
    s3 = (slopes2 - s1 - s2).astype(BF16).astype(F32)
    pieces = jnp.stack([s1, s2, s3, s1, s2, s3], axis=1)
    column = jnp.pad(pieces, ((0, 0), (0, LANES - NSA_DK - 6))).reshape(-1, 1)
    return jnp.broadcast_to(column, (column.shape[0], NSA_TILE)).astype(BF16)


def _position_features(pos):
    hi = (POS_SPLIT * (pos // POS_SPLIT)).astype(F32)
    lo = (pos % POS_SPLIT).astype(F32)
    return jnp.pad(jnp.stack([hi, hi, hi, lo, lo, lo], axis=1), ((0, 0), (NSA_DK, LANES - NSA_DK - 6)))


def _fused_uq_weight(w_uq):
    d = w_uq.shape[0]
    w = w_uq.reshape(d, MLA_HEADS, MLA_NOPE + MLA_ROPE)
    nope = w[:, :, :MLA_NOPE].reshape(d, MLA_HEADS * MLA_NOPE)
    rope = w[:, :, MLA_NOPE:]
    rope_rot = jnp.concatenate([-rope[:, :, MLA_ROPE // 2:], rope[:, :, :MLA_ROPE // 2]], axis=2)
    flat = lambda r: r.reshape(d, MLA_HEADS * MLA_ROPE)
    return jnp.concatenate([nope, flat(rope), flat(rope_rot)], axis=1).astype(BF16)


def _rope_tables(seq):
    inv = ROPE_THETA ** (-jnp.arange(0, MLA_ROPE, 2, dtype=F32) / MLA_ROPE)
    ang = jnp.arange(seq, dtype=F32)[:, None] * inv[None, :]
    reps = 2 * LANES // MLA_ROPE
    return jnp.tile(jnp.cos(ang), (1, reps)), jnp.tile(jnp.sin(ang), (1, reps))


def _overlap_matrix(seq, n_cmp, ncp):
    cmp_start = CMP_STRIDE * np.arange(n_cmp)
    slc_start = SLC_LEN * np.arange(seq // SLC_LEN)
    ov = np.clip(np.minimum(cmp_start[:, None] + CMP_LEN, slc_start[None, :] + SLC_LEN)
                 - np.maximum(cmp_start[:, None], slc_start[None, :]), 0, None).astype(np.float32) / CMP_STRIDE
    out = np.zeros((NS_PAD, ncp), np.float32)
    out[:ov.shape[1], :n_cmp] = ov.T
    return jnp.asarray(out, BF16)


def _active_tiles(flags, seq):
    t_blocks = NSA_TILE // SLC_LEN
    sub = ATT_TQ // NSA_TILE
    nq = seq // ATT_TQ
    steps = flags.shape[0] // sub
    tiles = flags.reshape(steps, sub, NS_PAD // t_blocks, t_blocks).max(axis=(1, 3))
    tile_id = jnp.arange(tiles.shape[1], dtype=jnp.int32)[None, :]
    last_tile = ((jnp.arange(steps, dtype=jnp.int32) % nq) * sub + (sub - 1))[:, None]
    active = (tiles > 0) & (tile_id <= last_tile)
    order = jnp.argsort(jnp.where(active, tile_id, tile_id + tiles.shape[1]), axis=-1).astype(jnp.int32)
    shifts = ID_BITS * jnp.arange(IDS_PER_WORD, dtype=jnp.int32)
    words = (order.reshape(steps, -1, IDS_PER_WORD) << shifts).sum(axis=-1).astype(jnp.int32)
    return active.sum(axis=-1).astype(jnp.int32), words.reshape(-1)


def kernel(x, attn_norm_g, w_in, mla_q_norm_g, mla_kv_norm_g, w_uq, w_uk, w_uv, cmp_pos_k, cmp_pos_v, w_cmp_k1,
           w_cmp_k2, w_cmp_v1, w_cmp_v2, w_o, ffn_norm_g, w_gate, w_up, w_down, final_norm_g):
    batch, seq, d = x.shape
    n = batch * seq
    assert w_in.shape[0] == 1, "the final RMSNorm is fused into the FFN kernel of a single layer"
    assert seq % (CMP_STRIDE * NSA_TILE) == 0 and seq % CMP_TQ == 0 and seq // SLC_LEN <= NS_PAD
    assert seq // POS_SPLIT <= BF16_EXACT_INT and POS_SPLIT <= BF16_EXACT_INT and seq // NSA_TILE <= 2 ** ID_BITS
    n_cmp = (seq - CMP_LEN) // CMP_STRIDE + 1
    ncp = seq // CMP_STRIDE
    cos_t, sin_t = _rope_tables(seq)
    ov_t = _overlap_matrix(seq, n_cmp, ncp)
    slopes2 = 2.0 ** (-8.0 * jnp.arange(1, NSA_HEADS + 1, dtype=F32) / NSA_HEADS) * LOG2E
    qfeat = _slope_features(slopes2)
    kfeat = _position_features(jnp.arange(seq, dtype=jnp.int32))
    x2 = x.reshape(n, d)

    w_fused, w_fused_t = _fused_in_weights(w_in[0])
    cq, ckv, kr, ksw, kvc, q, vt, gt = _in_proj(
        x2, attn_norm_g[0][None], w_fused, w_fused_t, qfeat, cos_t, sin_t, kfeat, seq)

    qn, qr, kn, vt_mla = _mla_up(cq, ckv, mla_q_norm_g[0][None], mla_kv_norm_g[0][None], _fused_uq_weight(w_uq[0]),
                                 w_uk[0].astype(BF16), w_uv[0].T.astype(BF16), cos_t, sin_t, seq)
    o_mla = _mla_attn(qn, qr, kn, kr, vt_mla, batch, seq)

    pos = jnp.stack([cmp_pos_k[0], cmp_pos_v[0]])
    w1 = jnp.stack([w_cmp_k1[0], w_cmp_v1[0]]).astype(BF16)
    w2 = jnp.stack([_pad_cols(w_cmp_k2[0], LANES), _pad_cols(w_cmp_v2[0], LANES)]).astype(BF16)
    w2t = jnp.stack([w_cmp_k2[0].T, w_cmp_v2[0].T]).astype(BF16)
    cmp_end = CMP_STRIDE * jnp.arange(ncp, dtype=jnp.int32) + (CMP_LEN - 1)
    cmp_rows, cmp_t = _compress(kvc, pos, w1, w2, w2t, _position_features(cmp_end), batch, seq)
    ones = jnp.zeros((ONES_ROWS, ncp), BF16).at[0].set(1.0)
    const_rows = jnp.broadcast_to(jnp.concatenate([ov_t, ones]), (batch, NSA_GROUPS, NS_PAD + ONES_ROWS, ncp))
    vaug = jnp.concatenate([cmp_t[1], const_rows], axis=2)
    vaug = vaug.reshape(batch, NSA_GROUPS, vaug.shape[2], ncp // NSA_TILE, NSA_TILE).transpose(0, 1, 3, 2, 4)

    oc, selt, flags = _nsa_cmp(q, cmp_rows, vaug, gt, batch, seq, n_cmp)
    counts, words = _active_tiles(flags.reshape(flags.shape[0], NS_PAD), seq)
    o_nsa = _nsa_attn(counts, words, q, selt, oc, gt, ksw, vt, batch, seq)

    split = MLA_HEADS * MLA_V
    x1 = _out_proj(x2, o_mla, o_nsa, w_o[0][:split].astype(BF16), w_o[0][split:].astype(BF16))
    out = _ffn(x1, ffn_norm_g[0][None], w_gate[0].astype(BF16), w_up[0].astype(BF16), w_down[0].astype(BF16),
               final_norm_g[None])
    return out.reshape(batch, seq, d)
```
